```python
import jax
import jax.numpy as jnp
from jax import lax
import numpy as np

D_MODEL = 1024
BATCH = 4
SEQ = 8192
DEPTH = 1
DEC_BATCH = 16
DEC_SEQ = 16
PAST_LEN = 2048

CHUNK = 64
EPS = 1e-6
NEG_INF = -1e30

N_HEADS_A = 8
N_KV_A = 2
GROUP_A = N_HEADS_A // N_KV_A
HEAD_DIM_A = 64
WINDOW = 128
WIN_CHUNKS = WINDOW // CHUNK
BAND = (WIN_CHUNKS + 1) * CHUNK

N_HEADS_B = 4
DK_B = 128
DV_B = 256

Q_A = N_HEADS_A * HEAD_DIM_A
KV_A = N_KV_A * HEAD_DIM_A
Q_B = N_HEADS_B * DK_B
V_B = N_HEADS_B * DV_B
D_IN = Q_A + 2 * KV_A + 2 * Q_B + 2 * V_B + 2 * D_MODEL
SPLITS = (Q_A,
          Q_A + KV_A,
          Q_A + 2 * KV_A,
          Q_A + 2 * KV_A + Q_B,
          Q_A + 2 * KV_A + 2 * Q_B,
          Q_A + 2 * KV_A + 2 * Q_B + V_B,
          Q_A + 2 * KV_A + 2 * Q_B + 2 * V_B)

N_KEYS = 128
N_EXPERTS = N_KEYS * N_KEYS
PEER_HEADS = 8
PEER_QDIM = 256
PEER_HALF = PEER_QDIM // 2
PEER_TOPK = 16
PEER_BLOCK = 128

kernel_name = 'streaming_swa_retention_peer_step'


def rmsnorm(x, g):
    xf = x.astype(jnp.float32)
    y = xf * lax.rsqrt(jnp.mean(xf * xf, axis=-1, keepdims=True) + EPS)
    return y.astype(x.dtype) * g


def alibi_slopes():
    return 2.0 ** (-8.0 * jnp.arange(1, N_HEADS_A + 1, dtype=jnp.float32) / N_HEADS_A)


def retention_log_decay():
    return jnp.log1p(-(2.0 ** (-5.0 - jnp.arange(N_HEADS_B, dtype=jnp.float32))))


def project(h, w_in, b_gate):
    lead = h.shape[:-1]
    z = h @ w_in
    qa, ka, va, qb, kb, vb, rg, bg = jnp.split(z, SPLITS, axis=-1)
    ga, gb = jnp.split(bg + b_gate, 2, axis=-1)
    qa = qa.reshape(*lead, N_KV_A, GROUP_A, HEAD_DIM_A)
    ka = ka.reshape(*lead, N_KV_A, HEAD_DIM_A)
    va = va.reshape(*lead, N_KV_A, HEAD_DIM_A)
    qb = qb.reshape(*lead, N_HEADS_B, DK_B)
    kb = kb.reshape(*lead, N_HEADS_B, DK_B) * (DK_B ** -0.5)
    vb = vb.reshape(*lead, N_HEADS_B, DV_B)
    rg = rg.reshape(*lead, N_HEADS_B, DV_B)
    return qa, ka, va, qb, kb, vb, rg, ga, gb


def swa_attend(q, k, v, q_pos, k_pos, k_valid, sinks):
    s = jnp.einsum('...qkgd,...skd->...kgqs', q, k).astype(jnp.float32) * (HEAD_DIM_A ** -0.5)
    qp = q_pos[..., :, None]
    kp = k_pos[..., None, :]
    dist = jnp.abs(qp - kp).astype(jnp.float32)
    slopes = alibi_slopes().reshape(N_KV_A, GROUP_A, 1, 1)
    s = s - slopes * dist[..., None, None, :, :]
    qc = qp // CHUNK
    kc = kp // CHUNK
    allowed = k_valid[..., None, :] & (kc <= qc) & (kc >= qc - WIN_CHUNKS)
    s = jnp.where(allowed[..., None, None, :, :], s, NEG_INF)
    sink = sinks.astype(jnp.float32).reshape(N_KV_A, GROUP_A, 1, 1)
    m = jnp.maximum(jnp.max(s, axis=-1, keepdims=True), sink)
    p = jnp.exp(s - m)
    denom = jnp.sum(p, axis=-1, keepdims=True) + jnp.exp(sink - m)
    p = (p / denom).astype(v.dtype)
    return jnp.einsum('...kgqs,...skd->...qkgd', p, v)


def swa_prompt(qa, ka, va, sinks):
    b, s_len = ka.shape[:2]
    n_c = s_len // CHUNK
    q = qa.reshape(b, n_c, CHUNK, N_KV_A, GROUP_A, HEAD_DIM_A)
    pad = ((0, 0), (WIN_CHUNKS * CHUNK, 0), (0, 0), (0, 0))
    kp = jnp.pad(ka, pad).reshape(b, n_c + WIN_CHUNKS, CHUNK, N_KV_A, HEAD_DIM_A)
    vp = jnp.pad(va, pad).reshape(b, n_c + WIN_CHUNKS, CHUNK, N_KV_A, HEAD_DIM_A)
    k_band = jnp.concatenate([kp[:, j:j + n_c] for j in range(WIN_CHUNKS + 1)], axis=2)
    v_band = jnp.concatenate([vp[:, j:j + n_c] for j in range(WIN_CHUNKS + 1)], axis=2)
    q_pos = jnp.arange(s_len, dtype=jnp.int32).reshape(n_c, CHUNK)
    k_pos = (jnp.arange(n_c, dtype=jnp.int32)[:, None] - WIN_CHUNKS) * CHUNK + jnp.arange(BAND, dtype=jnp.int32)[None, :]
    o = swa_attend(q, k_band, v_band, q_pos, k_pos, k_pos >= 0, sinks)
    return o.reshape(b, s_len, Q_A), ka[:, s_len - WINDOW:], va[:, s_len - WINDOW:]


def swa_sample(qa, ka, va, cache_k, cache_v, sinks):
    b, n_new = ka.shape[:2]
    k_all = jnp.concatenate([cache_k.astype(ka.dtype), ka], axis=1)
    v_all = jnp.concatenate([cache_v.astype(va.dtype), va], axis=1)
    q_pos = PAST_LEN + jnp.arange(n_new, dtype=jnp.int32)
    k_pos = jnp.concatenate([PAST_LEN - WINDOW + jnp.arange(WINDOW, dtype=jnp.int32), q_pos])
    o = swa_attend(qa, k_all, v_all, q_pos, k_pos, k_pos >= 0, sinks)
    return o.reshape(b, n_new, Q_A), k_all[:, n_new:], v_all[:, n_new:]


def retention_chunk(q, k, v, state):
    n_len = q.shape[1]
    lg = retention_log_decay()
    n = jnp.arange(n_len, dtype=jnp.float32)
    diff = n[:, None] - n[None, :]
    intra = jnp.where(diff >= 0, jnp.exp(jnp.maximum(diff, 0.0) * lg[:, None, None]), 0.0)
    qf = q.astype(jnp.float32)
    kf = k.astype(jnp.float32)
    vf = v.astype(jnp.float32)
    st = state.astype(jnp.float32)
    scores = jnp.einsum('blhd,bmhd->bhlm', qf, kf) * intra
    o = jnp.einsum('bhlm,bmhv->blhv', scores, vf)
    o = o + jnp.einsum('blhd,bhdv->blhv', qf, st) * jnp.exp((n[:, None] + 1.0) * lg)[None, :, :, None]
    k_dec = kf * jnp.exp((n_len - 1.0 - n)[:, None] * lg)[None, :, :, None]
    new_state = jnp.exp(n_len * lg)[None, :, None, None] * st + jnp.einsum('blhd,blhv->bhdv', k_dec, vf)
    return o, new_state


def retention_prompt(qb, kb, vb):
    b, s_len = qb.shape[:2]
    n_c = s_len // CHUNK

    def to_chunks(a):
        return jnp.swapaxes(a.reshape(b, n_c, CHUNK, *a.shape[2:]), 0, 1)

    def step(carry, inp):
        qi, ki, vi = inp
        o, carry = retention_chunk(qi, ki, vi, carry)
        return carry, o

    s0 = jnp.zeros((b, N_HEADS_B, DK_B, DV_B), jnp.float32)
    s_fin, o = lax.scan(step, s0, (to_chunks(qb), to_chunks(kb), to_chunks(vb)))
    return jnp.swapaxes(o, 0, 1).reshape(b, s_len, N_HEADS_B, DV_B), s_fin


def retention_output(o, rg):
    mu = jnp.mean(o, axis=-1, keepdims=True)
    var = jnp.mean(jnp.square(o - mu), axis=-1, keepdims=True)
    on = ((o - mu) * lax.rsqrt(var + EPS)).astype(rg.dtype)
    y = on * jax.nn.silu(rg)
    return y.reshape(*y.shape[:-2], V_B)


def merge_branches(oa, ob, ga, gb, w_branch_a, w_branch_b, w_out):
    mix = jax.nn.sigmoid(ga) * (oa @ w_branch_a) + jax.nn.sigmoid(gb) * (ob @ w_branch_b)
    return mix @ w_out


def peer_ffn(h, w_q, sub_k1, sub_k2, expert_u, expert_v):
    lead = h.shape[:-1]
    ht = h.reshape(-1, D_MODEL)
    n_tok = ht.shape[0]
    n_blk = -(-n_tok // PEER_BLOCK)
    ht = jnp.pad(ht, ((0, n_blk * PEER_BLOCK - n_tok), (0, 0)))
    q = (ht @ w_q).reshape(-1, PEER_HEADS, PEER_QDIM)
    s1 = jnp.einsum('thd,nd->thn', q[..., :PEER_HALF], sub_k1).astype(jnp.float32)
    s2 = jnp.einsum('thd,nd->thn', q[..., PEER_HALF:], sub_k2).astype(jnp.float32)
    v1, i1 = lax.top_k(s1, PEER_TOPK)
    v2, i2 = lax.top_k(s2, PEER_TOPK)
    cand = (v1[..., :, None] + v2[..., None, :]).reshape(-1, PEER_HEADS, PEER_TOPK * PEER_TOPK)
    cidx = (i1[..., :, None] * N_KEYS + i2[..., None, :]).reshape(-1, PEER_HEADS, PEER_TOPK * PEER_TOPK)
    best, pos = lax.top_k(cand, PEER_TOPK)
    idx = jnp.take_along_axis(cidx, pos, axis=-1)
    gate = jax.nn.softmax(best, axis=-1).astype(h.dtype)

    def expert_block(args):
        hb, ib, gb = args
        u = jnp.take(expert_u, ib, axis=0)
        act = jax.nn.gelu(jnp.einsum('td,tkd->tk', hb, u))
        vrows = jnp.take(expert_v, ib, axis=0)
        return jnp.einsum('tk,tkd->td', gb * act, vrows)

    y = lax.map(expert_block, (ht.reshape(n_blk, PEER_BLOCK, D_MODEL),
                               idx.reshape(n_blk, PEER_BLOCK, PEER_HEADS * PEER_TOPK),
                               gate.reshape(n_blk, PEER_BLOCK, PEER_HEADS * PEER_TOPK)))
    return y.reshape(-1, D_MODEL)[:n_tok].reshape(*lead, D_MODEL)


def setup_inputs(seed: int = 0) -> dict:
    key = jax.random.key(seed)
    ks = jax.random.split(key, 20)
    f32 = jnp.float32

    def nrm(k, shape, scale):
        return jax.random.normal(k, shape, f32) * scale

    return {
        'x_prompt': nrm(ks[0], (BATCH, SEQ, D_MODEL), 1.0),
        'x_sample': nrm(ks[1], (DEC_BATCH, DEC_SEQ, D_MODEL), 1.0),
        'cache_swa_k': nrm(ks[2], (DEPTH, DEC_BATCH, WINDOW, N_KV_A, HEAD_DIM_A), 1.0),
        'cache_swa_v': nrm(ks[3], (DEPTH, DEC_BATCH, WINDOW, N_KV_A, HEAD_DIM_A), 1.0),
        'state_ret': nrm(ks[4], (DEPTH, DEC_BATCH, N_HEADS_B, DK_B, DV_B), 0.5),
        'norm_mix': 1.0 + nrm(ks[5], (DEPTH, D_MODEL), 0.01),
        'w_in': nrm(ks[6], (DEPTH, D_MODEL, D_IN), D_MODEL ** -0.5),
        'b_gate': nrm(ks[7], (DEPTH, 2 * D_MODEL), 0.02),
        'attn_sinks': nrm(ks[8], (DEPTH, N_HEADS_A), 0.5),
        'w_branch_a': nrm(ks[9], (DEPTH, Q_A, D_MODEL), Q_A ** -0.5),
        'w_branch_b': nrm(ks[10], (DEPTH, V_B, D_MODEL), V_B ** -0.5),
        'w_out': nrm(ks[11], (DEPTH, D_MODEL, D_MODEL), D_MODEL ** -0.5),
        'norm_ffn': 1.0 + nrm(ks[12], (DEPTH, D_MODEL), 0.01),
        'peer_w_q': nrm(ks[13], (DEPTH, D_MODEL, PEER_HEADS * PEER_QDIM), D_MODEL ** -0.5),
        'peer_sub_k1': nrm(ks[14], (DEPTH, N_KEYS, PEER_HALF), PEER_HALF ** -0.5),
        'peer_sub_k2': nrm(ks[15], (DEPTH, N_KEYS, PEER_HALF), PEER_HALF ** -0.5),
        'peer_u': nrm(ks[16], (DEPTH, N_EXPERTS, D_MODEL), D_MODEL ** -0.5),
        'peer_v': nrm(ks[17], (DEPTH, N_EXPERTS, D_MODEL), PEER_HEADS ** -0.5),
        'norm_final': 1.0 + nrm(ks[18], (D_MODEL,), 0.01),
    }


def reference(x_prompt, x_sample, cache_swa_k, cache_swa_v, state_ret,
              norm_mix, w_in, b_gate, attn_sinks, w_branch_a, w_branch_b, w_out,
              norm_ffn, peer_w_q, peer_sub_k1, peer_sub_k2, peer_u, peer_v, norm_final):
    xp = x_prompt
    xs = x_sample
    kp_rows, vp_rows, sp_fin, ks_rows, vs_rows, ss_fin = [], [], [], [], [], []
    for l in range(DEPTH):
        h = rmsnorm(xp, norm_mix[l])
        qa, ka, va, qb, kb, vb, rg, ga, gb = project(h, w_in[l], b_gate[l])
        oa, k_new, v_new = swa_prompt(qa, ka, va, attn_sinks[l])
        ob, s_new = retention_prompt(qb, kb, vb)
        ob = retention_output(ob, rg)
        xp = xp + merge_branches(oa, ob, ga, gb, w_branch_a[l], w_branch_b[l], w_out[l])
        xp = xp + peer_ffn(rmsnorm(xp, norm_ffn[l]), peer_w_q[l], peer_sub_k1[l], peer_sub_k2[l], peer_u[l], peer_v[l])
        kp_rows.append(k_new)
        vp_rows.append(v_new)
        sp_fin.append(s_new)
        h = rmsnorm(xs, norm_mix[l])
        qa, ka, va, qb, kb, vb, rg, ga, gb = project(h, w_in[l], b_gate[l])
        oa, k_new, v_new = swa_sample(qa, ka, va, cache_swa_k[l], cache_swa_v[l], attn_sinks[l])
        ob, s_new = retention_chunk(qb, kb, vb, state_ret[l])
        ob = retention_output(ob, rg)
        xs = xs + merge_branches(oa, ob, ga, gb, w_branch_a[l], w_branch_b[l], w_out[l])
        xs = xs + peer_ffn(rmsnorm(xs, norm_ffn[l]), peer_w_q[l], peer_sub_k1[l], peer_sub_k2[l], peer_u[l], peer_v[l])
        ks_rows.append(k_new)
        vs_rows.append(v_new)
        ss_fin.append(s_new)
    y_prompt = rmsnorm(xp, norm_final)
    y_sample = rmsnorm(xs, norm_final)
    return (y_prompt, y_sample,
            jnp.stack(kp_rows), jnp.stack(vp_rows), jnp.stack(sp_fin),
            jnp.stack(ks_rows), jnp.stack(vs_rows), jnp.stack(ss_fin))
```

```python
import functools

import jax
import jax.numpy as jnp
from jax import lax
from jax.experimental import pallas as pl
from jax.experimental.pallas import tpu as pltpu

F32 = jnp.float32
BF16 = jnp.bfloat16
I32 = jnp.int32

D_MODEL = 1024
CHUNK = 64
EPS = 1e-6
NEG_INF = -1e30
PAST_LEN = 2048

N_HEADS_A = 8
N_KV_A = 2
GROUP_A = N_HEADS_A // N_KV_A
HEAD_DIM_A = 64
WINDOW = 128
N_HEADS_B = 4
DK_B = 128
DV_B = 256
Q_A = N_HEADS_A * HEAD_DIM_A
KV_A = N_KV_A * HEAD_DIM_A
Q_B = N_HEADS_B * DK_B
V_B = N_HEADS_B * DV_B
D_IN = Q_A + 2 * KV_A + 2 * Q_B + 2 * V_B + 2 * D_MODEL
OFF_QA = 0
OFF_KA = OFF_QA + Q_A
OFF_VA = OFF_KA + KV_A
OFF_QB = OFF_VA + KV_A
OFF_KB = OFF_QB + Q_B
OFF_VB = OFF_KB + Q_B
OFF_RG = OFF_VB + V_B
OFF_GA = OFF_RG + V_B
OFF_GB = OFF_GA + D_MODEL

N_KEYS = 128
N_EXPERTS = N_KEYS * N_KEYS
PEER_HEADS = 8
PEER_QDIM = 256
PEER_HALF = PEER_QDIM // 2
PEER_TOPK = 16
PICKS = PEER_HEADS * PEER_TOPK

LANES = 128
ATT_KEYS = 2 * WINDOW
VMEM_LIMIT = 56 * 1024 * 1024


def _rmsnorm(x, g):
    return x * lax.rsqrt(jnp.mean(x * x, axis=-1, keepdims=True) + EPS) * g


def _dot(a, b):
    return jnp.dot(a, b, preferred_element_type=F32)


def _dot_nt(a, b):
    return lax.dot_general(a, b, (((1,), (1,)), ((), ())), preferred_element_type=F32)


def _dot_tn(a, b):
    return lax.dot_general(a, b, (((0,), (0,)), ((), ())), preferred_element_type=F32)


def _params(sem):
    return pltpu.CompilerParams(dimension_semantics=sem, vmem_limit_bytes=VMEM_LIMIT)


def _proj_body(x_ref, g_ref, w_ref, bg_ref, qa_ref, ka_ref, va_ref, qb_ref, kb_ref, vb_ref,
               rg_ref, ga_ref, gb_ref):
    h = _rmsnorm(x_ref[...], g_ref[...]).astype(BF16)

    def mm(lo, width):
        return _dot(h, w_ref[:, lo:lo + width])

    qa_ref[...] = (mm(OFF_QA, Q_A) * (HEAD_DIM_A ** -0.5)).astype(BF16)
    ka_ref[...] = mm(OFF_KA, KV_A)
    va_ref[...] = mm(OFF_VA, KV_A)
    qb_ref[...] = mm(OFF_QB, Q_B).astype(BF16)
    kb_ref[...] = (mm(OFF_KB, Q_B) * (DK_B ** -0.5)).astype(BF16)
    vb_ref[...] = mm(OFF_VB, V_B).astype(BF16)
    rg_ref[...] = mm(OFF_RG, V_B)
    ga_ref[...] = mm(OFF_GA, D_MODEL) + bg_ref[:, :D_MODEL]
    gb_ref[...] = mm(OFF_GB, D_MODEL) + bg_ref[:, D_MODEL:]


def _project(x, g, w_bf16, b_gate, tm):
    n = x.shape[0]
    row = lambda width: pl.BlockSpec((tm, width), lambda i: (i, 0))
    full = lambda a: pl.BlockSpec(a.shape, lambda i: (0, 0))
    widths = (Q_A, KV_A, KV_A, Q_B, Q_B, V_B, V_B, D_MODEL, D_MODEL)
    dtypes = (BF16, F32, F32, BF16, BF16, BF16, F32, F32, F32)
    return pl.pallas_call(
        _proj_body,
        grid=(n // tm,),
        in_specs=[row(D_MODEL), full(g), full(w_bf16), full(b_gate)],
        out_specs=[row(w) for w in widths],
        out_shape=[jax.ShapeDtypeStruct((n, w), d) for w, d in zip(widths, dtypes)],
        compiler_params=_params(("parallel",)),
        name="projection",
    )(x, g, w_bf16, b_gate)


def _swa_body(sink_ref, q_ref, kp_ref, kc_ref, vp_ref, vc_ref, nd_ref, al_ref, o_ref):
    k = jnp.concatenate([kp_ref[...], kc_ref[...]], axis=0)
    v = jnp.concatenate([vp_ref[...], vc_ref[...]], axis=0)
    lane = lax.broadcasted_iota(I32, k.shape, 1)
    low = lane < HEAD_DIM_A
    k_sw = pltpu.roll(k, HEAD_DIM_A, axis=1)
    v_sw = pltpu.roll(v, HEAD_DIM_A, axis=1)
    negdist = nd_ref[...]
    allowed = al_ref[0] > 0.5
    for kv in range(N_KV_A):
        own = low if kv == 0 else jnp.logical_not(low)
        k_rep = jnp.where(own, k, k_sw)
        v_rep = jnp.where(own, v, v_sw)
        zero = jnp.zeros_like(k_rep)
        kk = jnp.concatenate([jnp.where(low, k_rep, zero), jnp.where(low, zero, k_rep)], axis=0).astype(BF16)
        vv = jnp.concatenate([jnp.where(low, v_rep, zero), jnp.where(low, zero, v_rep)], axis=0).astype(BF16)
        for a in range(GROUP_A // 2):
            col = kv * GROUP_A * HEAD_DIM_A + a * LANES
            s2 = _dot_nt(q_ref[:, col:col + LANES], kk)
            ps = []
            for u in range(2):
                head = kv * GROUP_A + 2 * a + u
                slope = 2.0 ** (-8.0 * (head + 1) / N_HEADS_A)
                sink = sink_ref[head]
                s = s2[:, u * ATT_KEYS:(u + 1) * ATT_KEYS] + slope * negdist
                s = jnp.where(allowed, s, NEG_INF)
                m = jnp.maximum(jnp.max(s, axis=-1, keepdims=True), sink)
                p = jnp.exp(s - m)
                denom = jnp.sum(p, axis=-1, keepdims=True) + jnp.exp(sink - m)
                ps.append((p / denom).astype(BF16))
            o = _dot(jnp.concatenate(ps, axis=1), vv)
            o_ref[:, col:col + LANES] = o.astype(BF16)


def _swa(sinks, q, k, v, negdist, allow, groups, q_blocks, tq):
    n = q.shape[0]
    kblocks = k.shape[0] // LANES // groups
    shift = kblocks - q_blocks
    prev = lambda b, j: (b * kblocks + jnp.maximum(j + shift - 1, 0), 0)
    cur = lambda b, j: (b * kblocks + j + shift, 0)
    kv_spec = lambda f: pl.BlockSpec((LANES, KV_A), f)
    return pl.pallas_call(
        _swa_body,
        grid=(groups, q_blocks),
        in_specs=[pl.BlockSpec(memory_space=pltpu.SMEM),
                  pl.BlockSpec((tq, Q_A), lambda b, j: (b * q_blocks + j, 0)),
                  kv_spec(prev), kv_spec(cur), kv_spec(prev), kv_spec(cur),
                  pl.BlockSpec((tq, ATT_KEYS), lambda b, j: (0, 0)),
                  pl.BlockSpec((1, tq, ATT_KEYS), lambda b, j: (jnp.minimum(j + shift, 1), 0, 0))],
        out_specs=pl.BlockSpec((tq, Q_A), lambda b, j: (b * q_blocks + j, 0)),
        out_shape=jax.ShapeDtypeStruct((n, Q_A), BF16),
        compiler_params=_params(("parallel", "parallel")),
        name="attention",
    )(sinks, q, k, k, v, v, negdist, allow)


def _ret_body(q_ref, k_ref, v_ref, rg_ref, s0_ref, intra_ref, rdec_ref, kdec_ref, sdec_ref,
              o_ref, st_ref):
    c = pl.program_id(1)

    @pl.when(c == 0)
    def _():
        st_ref[...] = s0_ref[...]

    for h in range(N_HEADS_B):
        q = q_ref[:, h * DK_B:(h + 1) * DK_B]
        k = k_ref[:, h * DK_B:(h + 1) * DK_B]
        v = v_ref[:, h * DV_B:(h + 1) * DV_B]
        st = st_ref[0, h]
        scores = (_dot_nt(q, k) * intra_ref[h]).astype(BF16)
        o = _dot(scores, v) + _dot(q, st.astype(BF16)) * rdec_ref[h]
        k_dec = (k.astype(F32) * kdec_ref[h]).astype(BF16)
        st_ref[0, h] = sdec_ref[h] * st + _dot_tn(k_dec, v)
        mu = jnp.mean(o, axis=-1, keepdims=True)
        d = o - mu
        var = jnp.mean(d * d, axis=-1, keepdims=True)
        on = d * lax.rsqrt(var + EPS)
        r = rg_ref[:, h * DV_B:(h + 1) * DV_B]
        o_ref[:, h * DV_B:(h + 1) * DV_B] = (on * (r * jax.nn.sigmoid(r))).astype(BF16)


def _retention_consts(length):
    lg = jnp.log1p(-(2.0 ** (-5.0 - jnp.arange(N_HEADS_B, dtype=F32))))
    n = jnp.arange(length, dtype=F32)
    diff = n[:, None] - n[None, :]
    intra = jnp.where(diff >= 0, jnp.exp(jnp.maximum(diff, 0.0) * lg[:, None, None]), 0.0)
    rdec = jnp.exp((n[None, :] + 1.0) * lg[:, None])[..., None]
    kdec = jnp.exp((length - 1.0 - n)[None, :] * lg[:, None])[..., None]
    sdec = jnp.exp(length * lg)
    return intra, rdec, kdec, sdec


def _retention(q, k, v, rg, s0, streams, chunks, length):
    n = q.shape[0]
    intra, rdec, kdec, sdec = _retention_consts(length)
    row = lambda width: pl.BlockSpec((length, width), lambda b, c: (b * chunks + c, 0))
    const = lambda a: pl.BlockSpec(a.shape, lambda b, c: (0,) * a.ndim)
    st_spec = pl.BlockSpec((1, N_HEADS_B, DK_B, DV_B), lambda b, c: (b, 0, 0, 0))
    return pl.pallas_call(
        _ret_body,
        grid=(streams, chunks),
        in_specs=[row(Q_B), row(Q_B), row(V_B), row(V_B), st_spec,
                  const(intra), const(rdec), const(kdec), pl.BlockSpec(memory_space=pltpu.SMEM)],
        out_specs=[row(V_B), st_spec],
        out_shape=[jax.ShapeDtypeStruct((n, V_B), BF16),
                   jax.ShapeDtypeStruct((streams, N_HEADS_B, DK_B, DV_B), F32)],
        compiler_params=_params(("parallel", "arbitrary")),
        name="retention",
    )(q, k, v, rg, s0, intra, rdec, kdec, sdec)


def _merge_body(x_ref, oa_ref, ob_ref, ga_ref, gb_ref, wa_ref, wb_ref, wo_ref, o_ref):
    ya = _dot(oa_ref[...], wa_ref[...])
    yb = _dot(ob_ref[...], wb_ref[...])
    mix = jax.nn.sigmoid(ga_ref[...]) * ya + jax.nn.sigmoid(gb_ref[...]) * yb
    o_ref[...] = x_ref[...] + _dot(mix.astype(BF16), wo_ref[...])


def _merge(x, oa, ob, ga, gb, wa, wb, wo, tm):
    n = x.shape[0]
    row = lambda width: pl.BlockSpec((tm, width), lambda i: (i, 0))
    full = lambda a: pl.BlockSpec(a.shape, lambda i: (0, 0))
    return pl.pallas_call(
        _merge_body,
        grid=(n // tm,),
        in_specs=[row(D_MODEL), row(Q_A), row(V_B), row(D_MODEL), row(D_MODEL), full(wa), full(wb), full(wo)],
        out_specs=row(D_MODEL),
        out_shape=jax.ShapeDtypeStruct((n, D_MODEL), F32),
        compiler_params=_params(("parallel",)),
        name="merge",
    )(x, oa, ob, ga, gb, wa, wb, wo)


def _top16(s, order):
    t = s.shape[1]
    rank = lax.broadcasted_iota(I32, (PEER_TOPK, t), 0)
    vals = jnp.zeros((PEER_TOPK, t), F32)
    ids = jnp.zeros((PEER_TOPK, t), F32)
    for r in range(PEER_TOPK):
        m = jnp.max(s, axis=0, keepdims=True)
        pick = jnp.min(jnp.where(s == m, order, 3e38), axis=0, keepdims=True)
        s = jnp.where(order == pick, -jnp.inf, s)
        vals = jnp.where(rank == r, m, vals)
        ids = jnp.where(rank == r, pick, ids)
    return vals, ids


def _route_body(x_ref, g_ref, wq_ref, k1_ref, k2_ref, idx_ref, gate_ref):
    h = _rmsnorm(x_ref[...], g_ref[...]).astype(BF16)
    t = h.shape[0]
    q_t = _dot_nt(wq_ref[...], h).astype(BF16)
    iota = lambda rows: lax.broadcasted_iota(I32, (rows, t), 0).astype(F32)
    key_id = iota(N_KEYS)
    flat = jnp.concatenate(
        [iota(PEER_TOPK)] + [a * PEER_TOPK + iota(8) for a in range(1, 8)] + [(8 + iota(8)) * PEER_TOPK],
        axis=0)
    rank = lax.broadcasted_iota(I32, (PEER_TOPK, t), 0)
    idx_rows, gate_rows = [], []
    for hd in range(PEER_HEADS):
        base = hd * PEER_QDIM
        s1 = _dot(k1_ref[...], q_t[base:base + PEER_HALF])
        s2 = _dot(k2_ref[...], q_t[base + PEER_HALF:base + PEER_QDIM])
        v1, i1 = _top16(s1, key_id)
        v2, i2 = _top16(s2, key_id)
        cand = jnp.concatenate(
            [v1[0:1] + v2]
            + [v1[a:a + 1] + v2[0:8] for a in range(1, 8)]
            + [v1[8:16] + v2[0:1]], axis=0)
        cidx = jnp.concatenate(
            [i1[0:1] * N_KEYS + i2]
            + [i1[a:a + 1] * N_KEYS + i2[0:8] for a in range(1, 8)]
            + [i1[8:16] * N_KEYS + i2[0:1]], axis=0)
        best, pos = _top16(cand, flat)
        eidx = jnp.zeros((PEER_TOPK, t), F32)
        for r in range(PEER_TOPK):
            picked = jnp.max(jnp.where(flat == pos[r:r + 1], cidx, -1.0), axis=0, keepdims=True)
            eidx = jnp.where(rank == r, picked, eidx)
        e = jnp.exp(best - best[0:1])
        gate_rows.append(e / jnp.sum(e, axis=0, keepdims=True))
        idx_rows.append(eidx.astype(I32))
    idx_ref[...] = jnp.concatenate(idx_rows, axis=0)
    gate_ref[...] = jnp.concatenate(gate_rows, axis=0)


def _route(x, g, wq_t, k1, k2, tm):
    n = x.shape[0]
    full = lambda a: pl.BlockSpec(a.shape, lambda i: (0, 0))
    col = pl.BlockSpec((PICKS, tm), lambda i: (0, i))
    return pl.pallas_call(
        _route_body,
        grid=(n // tm,),
        in_specs=[pl.BlockSpec((tm, D_MODEL), lambda i: (i, 0)), full(g), full(wq_t), full(k1), full(k2)],
        out_specs=[col, col],
        out_shape=[jax.ShapeDtypeStruct((PICKS, n), I32), jax.ShapeDtypeStruct((PICKS, n), F32)],
        compiler_params=_params(("parallel",)),
        name="routing",
    )(x, g, wq_t, k1, k2)


EXP_TOKENS = 128
EXP_SLOTS = 4


def _expert_body(x_ref, g_ref, gf_ref, idx_hbm, gate_ref, tab_hbm, o_ref,
                 idx_smem, buf, h_buf, y_buf, sems, idx_sem):
    i = pl.program_id(0)
    idx_copy = pltpu.make_async_copy(idx_hbm.at[pl.ds(i * (EXP_TOKENS * PICKS), EXP_TOKENS * PICKS)],
                                     idx_smem, idx_sem)
    idx_copy.start()
    h_buf[...] = _rmsnorm(x_ref[...], g_ref[...])
    idx_copy.wait()

    def row_copy(t, k, slot):
        e = idx_smem[t * PICKS + k]
        return pltpu.make_async_copy(tab_hbm.at[pl.ds(e, 1), :], buf.at[slot, pl.ds(k, 1), :], sems.at[slot])

    def issue(t, slot):
        def chunk(c, carry):
            for u in range(8):
                row_copy(t, c * 8 + u, slot).start()
            return carry
        lax.fori_loop(0, PICKS // 8, chunk, 0)

    def wait_slot(slot):
        pltpu.make_async_copy(tab_hbm.at[pl.ds(0, PICKS), :], buf.at[slot], sems.at[slot]).wait()

    for s in range(EXP_SLOTS):
        issue(s, s)

    lane = lax.broadcasted_iota(I32, (PICKS, EXP_TOKENS), 1)

    def token(t, carry):
        slot = t % EXP_SLOTS
        wait_slot(slot)
        h = h_buf[pl.ds(t, 1), :]
        u = buf[slot, :, :D_MODEL]
        s = jnp.sum(u * h, axis=-1, keepdims=True)
        gate = jnp.sum(jnp.where(lane == t, gate_ref[...], 0.0), axis=-1, keepdims=True)
        w = gate * jax.nn.gelu(s)
        v = buf[slot, :, D_MODEL:]
        y_buf[pl.ds(t, 1), :] = jnp.sum(w * v, axis=0, keepdims=True)

        @pl.when(t + EXP_SLOTS < EXP_TOKENS)
        def _():
            issue(t + EXP_SLOTS, slot)
        return carry

    lax.fori_loop(0, EXP_TOKENS, token, 0)
    o_ref[...] = _rmsnorm(x_ref[...] + y_buf[...], gf_ref[...])


def _experts(x, g, g_final, idx_flat, gate_t, table):
    n = x.shape[0]
    full = lambda a: pl.BlockSpec(a.shape, lambda i: (0, 0))
    row = pl.BlockSpec((EXP_TOKENS, D_MODEL), lambda i: (i, 0))
    return pl.pallas_call(
        _expert_body,
        grid=(n // EXP_TOKENS,),
        in_specs=[row, full(g), full(g_final),
                  pl.BlockSpec(memory_space=pl.ANY),
                  pl.BlockSpec((PICKS, EXP_TOKENS), lambda i: (0, i)),
                  pl.BlockSpec(memory_space=pl.ANY)],
        out_specs=row,
        out_shape=jax.ShapeDtypeStruct((n, D_MODEL), F32),
        scratch_shapes=[pltpu.SMEM((EXP_TOKENS * PICKS,), I32),
                        pltpu.VMEM((EXP_SLOTS, PICKS, 2 * D_MODEL), F32),
                        pltpu.VMEM((EXP_TOKENS, D_MODEL), F32),
                        pltpu.VMEM((EXP_TOKENS, D_MODEL), F32),
                        pltpu.SemaphoreType.DMA((EXP_SLOTS,)),
                        pltpu.SemaphoreType.DMA(())],
        compiler_params=_params(("arbitrary",)),
        name="experts",
    )(x, g, g_final, idx_flat, gate_t, table)


def _attention_masks(tq):
    i = jnp.arange(tq, dtype=I32)[:, None]
    r = jnp.arange(ATT_KEYS, dtype=I32)[None, :]
    negdist = -jnp.abs(WINDOW + i - r).astype(F32)
    return negdist, i, r


def _layer(x, n_groups, rows, swa_inputs, ret_state, ret_len, w, tm):
    (norm_mix, w_in, b_gate, sinks, wa, wb, wo, norm_ffn, wq_t, k1, k2, table, norm_final) = w
    qa, ka, va, qb, kb, vb, rg, ga, gb = _project(x, norm_mix, w_in, b_gate, tm)
    oa, k_rows, v_rows = swa_inputs(qa, ka, va, sinks)
    ob, s_fin = _retention(qb, kb, vb, rg, ret_state, n_groups, rows // ret_len, ret_len)
    x1 = _merge(x, oa, ob, ga, gb, wa, wb, wo, tm)
    idx_t, gate_t = _route(x1, norm_ffn, wq_t, k1, k2, tm)
    y = _experts(x1, norm_ffn, norm_final, idx_t.T.reshape(-1), gate_t, table)
    return y, k_rows, v_rows, s_fin


def kernel(x_prompt, x_sample, cache_swa_k, cache_swa_v, state_ret, norm_mix, w_in, b_gate, attn_sinks,
           w_branch_a, w_branch_b, w_out, norm_ffn, peer_w_q, peer_sub_k1, peer_sub_k2, peer_u, peer_v,
           norm_final):
    batch, seq, _ = x_prompt.shape
    dec_batch, dec_seq, _ = x_sample.shape
    assert norm_mix.shape[0] == 1, "single-layer trunk"
    table = jnp.concatenate([peer_u[0], peer_v[0]], axis=1)
    w = (norm_mix[0][None], w_in[0].astype(BF16), b_gate[0][None], attn_sinks[0],
         w_branch_a[0].astype(BF16), w_branch_b[0].astype(BF16), w_out[0].astype(BF16),
         norm_ffn[0][None], peer_w_q[0].T.astype(BF16), peer_sub_k1[0].astype(BF16),
         peer_sub_k2[0].astype(BF16), table, norm_final[None])

    def swa_prompt(qa, ka, va, sinks):
        tq = 2 * CHUNK
        negdist, i, r = _attention_masks(tq)
        lq, lk = i // CHUNK, r // CHUNK
        band = (lk >= lq) & (lk <= lq + 2)
        allow = jnp.stack([band & (r >= WINDOW), band]).astype(F32)
        oa = _swa(sinks, qa, ka, va, negdist, allow, batch, seq // tq, tq)
        tail = lambda a: a.reshape(batch, seq, N_KV_A, HEAD_DIM_A)[:, seq - WINDOW:]
        return oa, tail(ka), tail(va)

    s0 = jnp.zeros((batch, N_HEADS_B, DK_B, DV_B), F32)
    yp, kp, vp, sp = _layer(x_prompt.reshape(batch * seq, D_MODEL), batch, seq, swa_prompt, s0, 256, w, 256)

    def swa_sample(qa, ka, va, sinks):
        negdist, i, r = _attention_masks(dec_seq)
        visible = jnp.broadcast_to(r < WINDOW + dec_seq, (dec_seq, ATT_KEYS))
        allow = jnp.stack([visible, visible]).astype(F32)
        pad = jnp.zeros((dec_batch, ATT_KEYS - WINDOW - dec_seq, KV_A), F32)
        k_all = jnp.concatenate([cache_swa_k[0].reshape(dec_batch, WINDOW, KV_A),
                                 ka.reshape(dec_batch, dec_seq, KV_A), pad], axis=1)
        v_all = jnp.concatenate([cache_swa_v[0].reshape(dec_batch, WINDOW, KV_A),
                                 va.reshape(dec_batch, dec_seq, KV_A), pad], axis=1)
        oa = _swa(sinks, qa, k_all.reshape(-1, KV_A), v_all.reshape(-1, KV_A), negdist, allow,
                  dec_batch, 1, dec_seq)
        tail = lambda a: a[:, dec_seq:WINDOW + dec_seq].reshape(dec_batch, WINDOW, N_KV_A, HEAD_DIM_A)
        return oa, tail(k_all), tail(v_all)

    ys, ks, vs, ss = _layer(x_sample.reshape(dec_batch * dec_seq, D_MODEL), dec_batch, dec_seq, swa_sample,
                            state_ret[0], dec_seq, w, 128)

    return (yp.reshape(batch, seq, D_MODEL), ys.reshape(dec_batch, dec_seq, D_MODEL),
            kp[None], vp[None], sp[None], ks[None], vs[None], ss[None])
```

```python
import functools

import jax
import jax.numpy as jnp
from jax import lax
from jax.experimental import pallas as pl
from jax.experimental.pallas import tpu as pltpu

F32 = jnp.float32
BF16 = jnp.bfloat16
I32 = jnp.int32

D_MODEL = 1024
CHUNK = 64
EPS = 1e-6
NEG_INF = -1e30
PAST_LEN = 2048

N_HEADS_A = 8
N_KV_A = 2
GROUP_A = N_HEADS_A // N_KV_A
HEAD_DIM_A = 64
WINDOW = 128
N_HEADS_B = 4
DK_B = 128
DV_B = 256
Q_A = N_HEADS_A * HEAD_DIM_A
KV_A = N_KV_A * HEAD_DIM_A
Q_B = N_HEADS_B * DK_B
V_B = N_HEADS_B * DV_B
D_IN = Q_A + 2 * KV_A + 2 * Q_B + 2 * V_B + 2 * D_MODEL
OFF_QA = 0
OFF_KA = OFF_QA + Q_A
OFF_VA = OFF_KA + KV_A
OFF_QB = OFF_VA + KV_A
OFF_KB = OFF_QB + Q_B
OFF_VB = OFF_KB + Q_B
OFF_RG = OFF_VB + V_B
OFF_GA = OFF_RG + V_B
OFF_GB = OFF_GA + D_MODEL

N_KEYS = 128
N_EXPERTS = N_KEYS * N_KEYS
PEER_HEADS = 8
PEER_QDIM = 256
PEER_HALF = PEER_QDIM // 2
PEER_TOPK = 16
PICKS = PEER_HEADS * PEER_TOPK

LANES = 128
ATT_KEYS = 2 * WINDOW
VMEM_LIMIT = 56 * 1024 * 1024


def _rmsnorm(x, g):
    return x * lax.rsqrt(jnp.mean(x * x, axis=-1, keepdims=True) + EPS) * g


def _dot(a, b):
    return jnp.dot(a, b, preferred_element_type=F32)


def _dot_nt(a, b):
    return lax.dot_general(a, b, (((1,), (1,)), ((), ())), preferred_element_type=F32)


def _dot_tn(a, b):
    return lax.dot_general(a, b, (((0,), (0,)), ((), ())), preferred_element_type=F32)


def _params(sem):
    return pltpu.CompilerParams(dimension_semantics=sem, vmem_limit_bytes=VMEM_LIMIT)


def _proj_body(x_ref, g_ref, w_ref, bg_ref, qa_ref, ka_ref, va_ref, qb_ref, kb_ref, vb_ref,
               rg_ref, ga_ref, gb_ref):
    h = _rmsnorm(x_ref[...], g_ref[...]).astype(BF16)

    def mm(lo, width):
        return _dot(h, w_ref[:, lo:lo + width])

    qa_ref[...] = (mm(OFF_QA, Q_A) * (HEAD_DIM_A ** -0.5)).astype(BF16)
    ka_ref[...] = mm(OFF_KA, KV_A)
    va_ref[...] = mm(OFF_VA, KV_A)
    qb_ref[...] = mm(OFF_QB, Q_B).astype(BF16)
    kb_ref[...] = (mm(OFF_KB, Q_B) * (DK_B ** -0.5)).astype(BF16)
    vb_ref[...] = mm(OFF_VB, V_B).astype(BF16)
    rg_ref[...] = mm(OFF_RG, V_B)
    ga_ref[...] = mm(OFF_GA, D_MODEL) + bg_ref[:, :D_MODEL]
    gb_ref[...] = mm(OFF_GB, D_MODEL) + bg_ref[:, D_MODEL:]


def _project(x, g, w_bf16, b_gate, tm):
    n = x.shape[0]
    row = lambda width: pl.BlockSpec((tm, width), lambda i: (i, 0))
    full = lambda a: pl.BlockSpec(a.shape, lambda i: (0, 0))
    widths = (Q_A, KV_A, KV_A, Q_B, Q_B, V_B, V_B, D_MODEL, D_MODEL)
    dtypes = (BF16, F32, F32, BF16, BF16, BF16, F32, F32, F32)
    return pl.pallas_call(
        _proj_body,
        grid=(n // tm,),
        in_specs=[row(D_MODEL), full(g), full(w_bf16), full(b_gate)],
        out_specs=[row(w) for w in widths],
        out_shape=[jax.ShapeDtypeStruct((n, w), d) for w, d in zip(widths, dtypes)],
        compiler_params=_params(("parallel",)),
        name="projection",
    )(x, g, w_bf16, b_gate)


def _swa_body(sink_ref, q_ref, kp_ref, kc_ref, vp_ref, vc_ref, nd_ref, al_ref, o_ref):
    k = jnp.concatenate([kp_ref[...], kc_ref[...]], axis=0)
    v = jnp.concatenate([vp_ref[...], vc_ref[...]], axis=0)
    lane = lax.broadcasted_iota(I32, k.shape, 1)
    low = lane < HEAD_DIM_A
    k_sw = pltpu.roll(k, HEAD_DIM_A, axis=1)
    v_sw = pltpu.roll(v, HEAD_DIM_A, axis=1)
    negdist = nd_ref[...]
    allowed = al_ref[0] > 0.5
    for kv in range(N_KV_A):
        own = low if kv == 0 else jnp.logical_not(low)
        k_rep = jnp.where(own, k, k_sw)
        v_rep = jnp.where(own, v, v_sw)
        zero = jnp.zeros_like(k_rep)
        kk = jnp.concatenate([jnp.where(low, k_rep, zero), jnp.where(low, zero, k_rep)], axis=0).astype(BF16)
        vv = jnp.concatenate([jnp.where(low, v_rep, zero), jnp.where(low, zero, v_rep)], axis=0).astype(BF16)
        for a in range(GROUP_A // 2):
            col = kv * GROUP_A * HEAD_DIM_A + a * LANES
            s2 = _dot_nt(q_ref[:, col:col + LANES], kk)
            ps = []
            for u in range(2):
                head = kv * GROUP_A + 2 * a + u
                slope = 2.0 ** (-8.0 * (head + 1) / N_HEADS_A)
                sink = sink_ref[head]
                s = s2[:, u * ATT_KEYS:(u + 1) * ATT_KEYS] + slope * negdist
                s = jnp.where(allowed, s, NEG_INF)
                m = jnp.maximum(jnp.max(s, axis=-1, keepdims=True), sink)
                p = jnp.exp(s - m)
                denom = jnp.sum(p, axis=-1, keepdims=True) + jnp.exp(sink - m)
                ps.append((p / denom).astype(BF16))
            o = _dot(jnp.concatenate(ps, axis=1), vv)
            o_ref[:, col:col + LANES] = o.astype(BF16)


def _swa(sinks, q, k, v, negdist, allow, groups, q_blocks, tq):
    n = q.shape[0]
    kblocks = k.shape[0] // LANES // groups
    shift = kblocks - q_blocks
    prev = lambda b, j: (b * kblocks + jnp.maximum(j + shift - 1, 0), 0)
    cur = lambda b, j: (b * kblocks + j + shift, 0)
    kv_spec = lambda f: pl.BlockSpec((LANES, KV_A), f)
    return pl.pallas_call(
        _swa_body,
        grid=(groups, q_blocks),
        in_specs=[pl.BlockSpec(memory_space=pltpu.SMEM),
                  pl.BlockSpec((tq, Q_A), lambda b, j: (b * q_blocks + j, 0)),
                  kv_spec(prev), kv_spec(cur), kv_spec(prev), kv_spec(cur),
                  pl.BlockSpec((tq, ATT_KEYS), lambda b, j: (0, 0)),
                  pl.BlockSpec((1, tq, ATT_KEYS), lambda b, j: (jnp.minimum(j + shift, 1), 0, 0))],
        out_specs=pl.BlockSpec((tq, Q_A), lambda b, j: (b * q_blocks + j, 0)),
        out_shape=jax.ShapeDtypeStruct((n, Q_A), BF16),
        compiler_params=_params(("parallel", "parallel")),
        name="attention",
    )(sinks, q, k, k, v, v, negdist, allow)


def _ret_body(q_ref, k_ref, v_ref, rg_ref, s0_ref, intra_ref, rdec_ref, kdec_ref, sdec_ref,
              o_ref, st_ref):
    c = pl.program_id(1)

    @pl.when(c == 0)
    def _():
        st_ref[...] = s0_ref[...]

    for h in range(N_HEADS_B):
        q = q_ref[:, h * DK_B:(h + 1) * DK_B]
        k = k_ref[:, h * DK_B:(h + 1) * DK_B]
        v = v_ref[:, h * DV_B:(h + 1) * DV_B]
        st = st_ref[0, h]
        scores = (_dot_nt(q, k) * intra_ref[h]).astype(BF16)
        o = _dot(scores, v) + _dot(q, st.astype(BF16)) * rdec_ref[h]
        k_dec = (k.astype(F32) * kdec_ref[h]).astype(BF16)
        st_ref[0, h] = sdec_ref[h] * st + _dot_tn(k_dec, v)
        mu = jnp.mean(o, axis=-1, keepdims=True)
        d = o - mu
        var = jnp.mean(d * d, axis=-1, keepdims=True)
        on = d * lax.rsqrt(var + EPS)
        r = rg_ref[:, h * DV_B:(h + 1) * DV_B]
        o_ref[:, h * DV_B:(h + 1) * DV_B] = (on * (r * jax.nn.sigmoid(r))).astype(BF16)


def _retention_consts(length):
    lg = jnp.log1p(-(2.0 ** (-5.0 - jnp.arange(N_HEADS_B, dtype=F32))))
    n = jnp.arange(length, dtype=F32)
    diff = n[:, None] - n[None, :]
    intra = jnp.where(diff >= 0, jnp.exp(jnp.maximum(diff, 0.0) * lg[:, None, None]), 0.0)
    rdec = jnp.exp((n[None, :] + 1.0) * lg[:, None])[..., None]
    kdec = jnp.exp((length - 1.0 - n)[None, :] * lg[:, None])[..., None]
    sdec = jnp.exp(length * lg)
    return intra, rdec, kdec, sdec


def _retention(q, k, v, rg, s0, streams, chunks, length):
    n = q.shape[0]
    intra, rdec, kdec, sdec = _retention_consts(length)
    row = lambda width: pl.BlockSpec((length, width), lambda b, c: (b * chunks + c, 0))
    const = lambda a: pl.BlockSpec(a.shape, lambda b, c: (0,) * a.ndim)
    st_spec = pl.BlockSpec((1, N_HEADS_B, DK_B, DV_B), lambda b, c: (b, 0, 0, 0))
    return pl.pallas_call(
        _ret_body,
        grid=(streams, chunks),
        in_specs=[row(Q_B), row(Q_B), row(V_B), row(V_B), st_spec,
                  const(intra), const(rdec), const(kdec), pl.BlockSpec(memory_space=pltpu.SMEM)],
        out_specs=[row(V_B), st_spec],
        out_shape=[jax.ShapeDtypeStruct((n, V_B), BF16),
                   jax.ShapeDtypeStruct((streams, N_HEADS_B, DK_B, DV_B), F32)],
        compiler_params=_params(("parallel", "arbitrary")),
        name="retention",
    )(q, k, v, rg, s0, intra, rdec, kdec, sdec)


def _merge_body(x_ref, oa_ref, ob_ref, ga_ref, gb_ref, wa_ref, wb_ref, wo_ref, o_ref):
    ya = _dot(oa_ref[...], wa_ref[...])
    yb = _dot(ob_ref[...], wb_ref[...])
    mix = jax.nn.sigmoid(ga_ref[...]) * ya + jax.nn.sigmoid(gb_ref[...]) * yb
    o_ref[...] = x_ref[...] + _dot(mix.astype(BF16), wo_ref[...])


def _merge(x, oa, ob, ga, gb, wa, wb, wo, tm):
    n = x.shape[0]
    row = lambda width: pl.BlockSpec((tm, width), lambda i: (i, 0))
    full = lambda a: pl.BlockSpec(a.shape, lambda i: (0, 0))
    return pl.pallas_call(
        _merge_body,
        grid=(n // tm,),
        in_specs=[row(D_MODEL), row(Q_A), row(V_B), row(D_MODEL), row(D_MODEL), full(wa), full(wb), full(wo)],
        out_specs=row(D_MODEL),
        out_shape=jax.ShapeDtypeStruct((n, D_MODEL), F32),
        compiler_params=_params(("parallel",)),
        name="merge",
    )(x, oa, ob, ga, gb, wa, wb, wo)


def _top16(s, order):
    t = s.shape[1]
    rank = lax.broadcasted_iota(I32, (PEER_TOPK, t), 0)
    vals = jnp.zeros((PEER_TOPK, t), F32)
    ids = jnp.zeros((PEER_TOPK, t), F32)
    for r in range(PEER_TOPK):
        m = jnp.max(s, axis=0, keepdims=True)
        pick = jnp.min(jnp.where(s == m, order, 3e38), axis=0, keepdims=True)
        s = jnp.where(order == pick, -jnp.inf, s)
        vals = jnp.where(rank == r, m, vals)
        ids = jnp.where(rank == r, pick, ids)
    return vals, ids


def _route_body(x_ref, g_ref, wq_ref, k1_ref, k2_ref, idx_ref, gate_ref):
    h = _rmsnorm(x_ref[...], g_ref[...]).astype(BF16)
    t = h.shape[0]
    q_t = _dot_nt(wq_ref[...], h).astype(BF16)
    iota = lambda rows: lax.broadcasted_iota(I32, (rows, t), 0).astype(F32)
    key_id = iota(N_KEYS)
    flat = jnp.concatenate(
        [iota(PEER_TOPK)] + [a * PEER_TOPK + iota(8) for a in range(1, 8)] + [(8 + iota(8)) * PEER_TOPK],
        axis=0)
    rank = lax.broadcasted_iota(I32, (PEER_TOPK, t), 0)
    idx_rows, gate_rows = [], []
    for hd in range(PEER_HEADS):
        base = hd * PEER_QDIM
        s1 = _dot(k1_ref[...], q_t[base:base + PEER_HALF])
        s2 = _dot(k2_ref[...], q_t[base + PEER_HALF:base + PEER_QDIM])
        v1, i1 = _top16(s1, key_id)
        v2, i2 = _top16(s2, key_id)
        cand = jnp.concatenate(
            [v1[0:1] + v2]
            + [v1[a:a + 1] + v2[0:8] for a in range(1, 8)]
            + [v1[8:16] + v2[0:1]], axis=0)
        cidx = jnp.concatenate(
            [i1[0:1] * N_KEYS + i2]
            + [i1[a:a + 1] * N_KEYS + i2[0:8] for a in range(1, 8)]
            + [i1[8:16] * N_KEYS + i2[0:1]], axis=0)
        best, pos = _top16(cand, flat)
        eidx = jnp.zeros((PEER_TOPK, t), F32)
        for r in range(PEER_TOPK):
            picked = jnp.max(jnp.where(flat == pos[r:r + 1], cidx, -1.0), axis=0, keepdims=True)
            eidx = jnp.where(rank == r, picked, eidx)
        e = jnp.exp(best - best[0:1])
        gate_rows.append(e / jnp.sum(e, axis=0, keepdims=True))
        idx_rows.append(eidx.astype(I32))
    idx_ref[...] = jnp.concatenate(idx_rows, axis=0)
    gate_ref[...] = jnp.concatenate(gate_rows, axis=0)


def _route(x, g, wq_t, k1, k2, tm):
    n = x.shape[0]
    full = lambda a: pl.BlockSpec(a.shape, lambda i: (0, 0))
    col = pl.BlockSpec((PICKS, tm), lambda i: (0, i))
    return pl.pallas_call(
        _route_body,
        grid=(n // tm,),
        in_specs=[pl.BlockSpec((tm, D_MODEL), lambda i: (i, 0)), full(g), full(wq_t), full(k1), full(k2)],
        out_specs=[col, col],
        out_shape=[jax.ShapeDtypeStruct((PICKS, n), I32), jax.ShapeDtypeStruct((PICKS, n), F32)],
        compiler_params=_params(("parallel",)),
        name="routing",
    )(x, g, wq_t, k1, k2)


EXP_TOKENS = 256
EXP_SLOTS = 4


def _expert_body(x_ref, g_ref, gf_ref, idx_hbm, gate_ref, tab_hbm, o_ref,
                 idx_smem, buf0, buf1, buf2, buf3, h_buf, y_buf, sems, idx_sem):
    bufs = (buf0, buf1, buf2, buf3)
    ahead = EXP_SLOTS - 1
    i = pl.program_id(0)
    idx_copy = pltpu.make_async_copy(idx_hbm.at[pl.ds(i * (EXP_TOKENS * PICKS), EXP_TOKENS * PICKS)],
                                     idx_smem, idx_sem)
    idx_copy.start()
    h_buf[...] = _rmsnorm(x_ref[...], g_ref[...])
    idx_copy.wait()

    def issue(t, slot):
        for k in range(PICKS):
            e = idx_smem[t * PICKS + k]
            pltpu.make_async_copy(tab_hbm.at[e], bufs[slot].at[pl.ds(k, 1), :], sems.at[slot]).start()

    def wait_slot(slot):
        pltpu.make_async_copy(bufs[(slot + 1) % EXP_SLOTS], bufs[slot], sems.at[slot]).wait()

    lane = lax.broadcasted_iota(I32, (PICKS, EXP_TOKENS), 1)

    def compute(t, slot):
        h = h_buf[pl.ds(t, 1), :]
        u = bufs[slot][:, :D_MODEL]
        s = jnp.sum(u * h, axis=-1, keepdims=True)
        gate = jnp.sum(jnp.where(lane == t, gate_ref[...], 0.0), axis=-1, keepdims=True)
        w = gate * jax.nn.gelu(s)
        v = bufs[slot][:, D_MODEL:]
        y_buf[pl.ds(t, 1), :] = jnp.sum(w * v, axis=0, keepdims=True)

    def group(t0, issue_upto):
        for s in range(EXP_SLOTS):
            wait_slot(s)
            if s < issue_upto:
                issue(t0 + s + ahead, (s + ahead) % EXP_SLOTS)
            compute(t0 + s, s)

    for s in range(ahead):
        issue(s, s)

    def steady(q, carry):
        group(q * EXP_SLOTS, EXP_SLOTS)
        return carry

    n_groups = EXP_TOKENS // EXP_SLOTS
    lax.fori_loop(0, n_groups - 1, steady, 0)
    group((n_groups - 1) * EXP_SLOTS, EXP_SLOTS - ahead)
    o_ref[...] = _rmsnorm(x_ref[...] + y_buf[...], gf_ref[...])


def _experts(x, g, g_final, idx_flat, gate_t, table):
    n = x.shape[0]
    full = lambda a: pl.BlockSpec(a.shape, lambda i: (0, 0))
    row = pl.BlockSpec((EXP_TOKENS, D_MODEL), lambda i: (i, 0))
    return pl.pallas_call(
        _expert_body,
        grid=(n // EXP_TOKENS,),
        in_specs=[row, full(g), full(g_final),
                  pl.BlockSpec(memory_space=pl.ANY),
                  pl.BlockSpec((PICKS, EXP_TOKENS), lambda i: (0, i)),
                  pl.BlockSpec(memory_space=pl.ANY)],
        out_specs=row,
        out_shape=jax.ShapeDtypeStruct((n, D_MODEL), F32),
        scratch_shapes=[pltpu.SMEM((EXP_TOKENS * PICKS,), I32)]
        + [pltpu.VMEM((PICKS, 2 * D_MODEL), F32) for _ in range(EXP_SLOTS)]
        + [pltpu.VMEM((EXP_TOKENS, D_MODEL), F32),
           pltpu.VMEM((EXP_TOKENS, D_MODEL), F32),
           pltpu.SemaphoreType.DMA((EXP_SLOTS,)),
           pltpu.SemaphoreType.DMA(())],
        compiler_params=pltpu.CompilerParams(dimension_semantics=("arbitrary",), vmem_limit_bytes=VMEM_LIMIT,
                                             disable_bounds_checks=True),
        name="experts",
    )(x, g, g_final, idx_flat, gate_t, table)


def _attention_masks(tq):
    i = jnp.arange(tq, dtype=I32)[:, None]
    r = jnp.arange(ATT_KEYS, dtype=I32)[None, :]
    negdist = -jnp.abs(WINDOW + i - r).astype(F32)
    return negdist, i, r


def _layer(x, n_groups, rows, swa_inputs, ret_state, ret_len, w, tm):
    (norm_mix, w_in, b_gate, sinks, wa, wb, wo, norm_ffn, wq_t, k1, k2, table, norm_final) = w
    qa, ka, va, qb, kb, vb, rg, ga, gb = _project(x, norm_mix, w_in, b_gate, tm)
    oa, k_rows, v_rows = swa_inputs(qa, ka, va, sinks)
    ob, s_fin = _retention(qb, kb, vb, rg, ret_state, n_groups, rows // ret_len, ret_len)
    x1 = _merge(x, oa, ob, ga, gb, wa, wb, wo, tm)
    idx_t, gate_t = _route(x1, norm_ffn, wq_t, k1, k2, tm)
    y = _experts(x1, norm_ffn, norm_final, idx_t.T.reshape(-1), gate_t, table)
    return y, k_rows, v_rows, s_fin


def kernel(x_prompt, x_sample, cache_swa_k, cache_swa_v, state_ret, norm_mix, w_in, b_gate, attn_sinks,
           w_branch_a, w_branch_b, w_out, norm_ffn, peer_w_q, peer_sub_k1, peer_sub_k2, peer_u, peer_v,
           norm_final):
    batch, seq, _ = x_prompt.shape
    dec_batch, dec_seq, _ = x_sample.shape
    assert norm_mix.shape[0] == 1, "single-layer trunk"
    table = jnp.concatenate([peer_u[0], peer_v[0]], axis=1)[:, None, :]
    w = (norm_mix[0][None], w_in[0].astype(BF16), b_gate[0][None], attn_sinks[0],
         w_branch_a[0].astype(BF16), w_branch_b[0].astype(BF16), w_out[0].astype(BF16),
         norm_ffn[0][None], peer_w_q[0].T.astype(BF16), peer_sub_k1[0].astype(BF16),
         peer_sub_k2[0].astype(BF16), table, norm_final[None])

    def swa_prompt(qa, ka, va, sinks):
        tq = 2 * CHUNK
        negdist, i, r = _attention_masks(tq)
        lq, lk = i // CHUNK, r // CHUNK
        band = (lk >= lq) & (lk <= lq + 2)
        allow = jnp.stack([band & (r >= WINDOW), band]).astype(F32)
        oa = _swa(sinks, qa, ka, va, negdist, allow, batch, seq // tq, tq)
        tail = lambda a: a.reshape(batch, seq, N_KV_A, HEAD_DIM_A)[:, seq - WINDOW:]
        return oa, tail(ka), tail(va)

    def swa_sample(qa, ka, va, sinks):
        negdist, i, r = _attention_masks(dec_seq)
        visible = jnp.broadcast_to(r < WINDOW + dec_seq, (dec_seq, ATT_KEYS))
        allow = jnp.stack([visible, visible]).astype(F32)
        pad = jnp.zeros((dec_batch, ATT_KEYS - WINDOW - dec_seq, KV_A), F32)
        k_all = jnp.concatenate([cache_swa_k[0].reshape(dec_batch, WINDOW, KV_A),
                                 ka.reshape(dec_batch, dec_seq, KV_A), pad], axis=1)
        v_all = jnp.concatenate([cache_swa_v[0].reshape(dec_batch, WINDOW, KV_A),
                                 va.reshape(dec_batch, dec_seq, KV_A), pad], axis=1)
        oa = _swa(sinks, qa, k_all.reshape(-1, KV_A), v_all.reshape(-1, KV_A), negdist, allow,
                  dec_batch, 1, dec_seq)
        tail = lambda a: a[:, dec_seq:WINDOW + dec_seq].reshape(dec_batch, WINDOW, N_KV_A, HEAD_DIM_A)
        return oa, tail(k_all), tail(v_all)

    ys, ks, vs, ss = _layer(x_sample.reshape(dec_batch * dec_seq, D_MODEL), dec_batch, dec_seq, swa_sample,
                            state_ret[0], dec_seq, w, 128)

    s0 = jnp.zeros((batch, N_HEADS_B, DK_B, DV_B), F32)
    yp, kp, vp, sp = _layer(x_prompt.reshape(batch * seq, D_MODEL), batch, seq, swa_prompt, s0, 256, w, 256)

    return (yp.reshape(batch, seq, D_MODEL), ys.reshape(dec_batch, dec_seq, D_MODEL),
            kp[None], vp[None], sp[None], ks[None], vs[None], ss[None])
```

```python
import functools

import jax
import jax.numpy as jnp
from jax import lax
from jax.experimental import pallas as pl
from jax.experimental.pallas import tpu as pltpu
from jax.experimental.pallas import tpu_sc as plsc

F32 = jnp.float32
BF16 = jnp.bfloat16
I32 = jnp.int32

D_MODEL = 1024
CHUNK = 64
EPS = 1e-6
NEG_INF = -1e30
PAST_LEN = 2048

N_HEADS_A = 8
N_KV_A = 2
GROUP_A = N_HEADS_A // N_KV_A
HEAD_DIM_A = 64
WINDOW = 128
N_HEADS_B = 4
DK_B = 128
DV_B = 256
Q_A = N_HEADS_A * HEAD_DIM_A
KV_A = N_KV_A * HEAD_DIM_A
Q_B = N_HEADS_B * DK_B
V_B = N_HEADS_B * DV_B
D_IN = Q_A + 2 * KV_A + 2 * Q_B + 2 * V_B + 2 * D_MODEL
OFF_QA = 0
OFF_KA = OFF_QA + Q_A
OFF_VA = OFF_KA + KV_A
OFF_QB = OFF_VA + KV_A
OFF_KB = OFF_QB + Q_B
OFF_VB = OFF_KB + Q_B
OFF_RG = OFF_VB + V_B
OFF_GA = OFF_RG + V_B
OFF_GB = OFF_GA + D_MODEL

N_KEYS = 128
N_EXPERTS = N_KEYS * N_KEYS
PEER_HEADS = 8
PEER_QDIM = 256
PEER_HALF = PEER_QDIM // 2
PEER_TOPK = 16
PICKS = PEER_HEADS * PEER_TOPK

LANES = 128
ATT_KEYS = 2 * WINDOW
VMEM_LIMIT = 56 * 1024 * 1024


def _rmsnorm(x, g):
    return x * lax.rsqrt(jnp.mean(x * x, axis=-1, keepdims=True) + EPS) * g


def _dot(a, b):
    return jnp.dot(a, b, preferred_element_type=F32)


def _dot_nt(a, b):
    return lax.dot_general(a, b, (((1,), (1,)), ((), ())), preferred_element_type=F32)


def _dot_tn(a, b):
    return lax.dot_general(a, b, (((0,), (0,)), ((), ())), preferred_element_type=F32)


def _params(sem):
    return pltpu.CompilerParams(dimension_semantics=sem, vmem_limit_bytes=VMEM_LIMIT)


def _proj_body(x_ref, g_ref, w_ref, bg_ref, qa_ref, ka_ref, va_ref, qb_ref, kb_ref, vb_ref,
               rg_ref, ga_ref, gb_ref):
    h = _rmsnorm(x_ref[...], g_ref[...]).astype(BF16)

    def mm(lo, width):
        return _dot(h, w_ref[:, lo:lo + width])

    qa_ref[...] = (mm(OFF_QA, Q_A) * (HEAD_DIM_A ** -0.5)).astype(BF16)
    ka_ref[...] = mm(OFF_KA, KV_A)
    va_ref[...] = mm(OFF_VA, KV_A)
    qb_ref[...] = mm(OFF_QB, Q_B).astype(BF16)
    kb_ref[...] = (mm(OFF_KB, Q_B) * (DK_B ** -0.5)).astype(BF16)
    vb_ref[...] = mm(OFF_VB, V_B).astype(BF16)
    rg_ref[...] = mm(OFF_RG, V_B)
    ga_ref[...] = mm(OFF_GA, D_MODEL) + bg_ref[:, :D_MODEL]
    gb_ref[...] = mm(OFF_GB, D_MODEL) + bg_ref[:, D_MODEL:]


def _project(x, g, w_bf16, b_gate, tm):
    n = x.shape[0]
    row = lambda width: pl.BlockSpec((tm, width), lambda i: (i, 0))
    full = lambda a: pl.BlockSpec(a.shape, lambda i: (0, 0))
    widths = (Q_A, KV_A, KV_A, Q_B, Q_B, V_B, V_B, D_MODEL, D_MODEL)
    dtypes = (BF16, F32, F32, BF16, BF16, BF16, F32, F32, F32)
    return pl.pallas_call(
        _proj_body,
        grid=(n // tm,),
        in_specs=[row(D_MODEL), full(g), full(w_bf16), full(b_gate)],
        out_specs=[row(w) for w in widths],
        out_shape=[jax.ShapeDtypeStruct((n, w), d) for w, d in zip(widths, dtypes)],
        compiler_params=_params(("parallel",)),
        name="projection",
    )(x, g, w_bf16, b_gate)


def _swa_body(sink_ref, q_ref, kp_ref, kc_ref, vp_ref, vc_ref, nd_ref, al_ref, o_ref):
    k = jnp.concatenate([kp_ref[...], kc_ref[...]], axis=0)
    v = jnp.concatenate([vp_ref[...], vc_ref[...]], axis=0)
    lane = lax.broadcasted_iota(I32, k.shape, 1)
    low = lane < HEAD_DIM_A
    k_sw = pltpu.roll(k, HEAD_DIM_A, axis=1)
    v_sw = pltpu.roll(v, HEAD_DIM_A, axis=1)
    negdist = nd_ref[...]
    allowed = al_ref[0] > 0.5
    for kv in range(N_KV_A):
        own = low if kv == 0 else jnp.logical_not(low)
        k_rep = jnp.where(own, k, k_sw)
        v_rep = jnp.where(own, v, v_sw)
        zero = jnp.zeros_like(k_rep)
        kk = jnp.concatenate([jnp.where(low, k_rep, zero), jnp.where(low, zero, k_rep)], axis=0).astype(BF16)
        vv = jnp.concatenate([jnp.where(low, v_rep, zero), jnp.where(low, zero, v_rep)], axis=0).astype(BF16)
        for a in range(GROUP_A // 2):
            col = kv * GROUP_A * HEAD_DIM_A + a * LANES
            s2 = _dot_nt(q_ref[:, col:col + LANES], kk)
            ps = []
            for u in range(2):
                head = kv * GROUP_A + 2 * a + u
                slope = 2.0 ** (-8.0 * (head + 1) / N_HEADS_A)
                sink = sink_ref[head]
                s = s2[:, u * ATT_KEYS:(u + 1) * ATT_KEYS] + slope * negdist
                s = jnp.where(allowed, s, NEG_INF)
                m = jnp.maximum(jnp.max(s, axis=-1, keepdims=True), sink)
                p = jnp.exp(s - m)
                denom = jnp.sum(p, axis=-1, keepdims=True) + jnp.exp(sink - m)
                ps.append((p / denom).astype(BF16))
            o = _dot(jnp.concatenate(ps, axis=1), vv)
            o_ref[:, col:col + LANES] = o.astype(BF16)


def _swa(sinks, q, k, v, negdist, allow, groups, q_blocks, tq):
    n = q.shape[0]
    kblocks = k.shape[0] // LANES // groups
    shift = kblocks - q_blocks
    prev = lambda b, j: (b * kblocks + jnp.maximum(j + shift - 1, 0), 0)
    cur = lambda b, j: (b * kblocks + j + shift, 0)
    kv_spec = lambda f: pl.BlockSpec((LANES, KV_A), f)
    return pl.pallas_call(
        _swa_body,
        grid=(groups, q_blocks),
        in_specs=[pl.BlockSpec(memory_space=pltpu.SMEM),
                  pl.BlockSpec((tq, Q_A), lambda b, j: (b * q_blocks + j, 0)),
                  kv_spec(prev), kv_spec(cur), kv_spec(prev), kv_spec(cur),
                  pl.BlockSpec((tq, ATT_KEYS), lambda b, j: (0, 0)),
                  pl.BlockSpec((1, tq, ATT_KEYS), lambda b, j: (jnp.minimum(j + shift, 1), 0, 0))],
        out_specs=pl.BlockSpec((tq, Q_A), lambda b, j: (b * q_blocks + j, 0)),
        out_shape=jax.ShapeDtypeStruct((n, Q_A), BF16),
        compiler_params=_params(("parallel", "parallel")),
        name="attention",
    )(sinks, q, k, k, v, v, negdist, allow)


def _ret_body(q_ref, k_ref, v_ref, rg_ref, s0_ref, intra_ref, rdec_ref, kdec_ref, sdec_ref,
              o_ref, st_ref):
    c = pl.program_id(1)

    @pl.when(c == 0)
    def _():
        st_ref[...] = s0_ref[...]

    for h in range(N_HEADS_B):
        q = q_ref[:, h * DK_B:(h + 1) * DK_B]
        k = k_ref[:, h * DK_B:(h + 1) * DK_B]
        v = v_ref[:, h * DV_B:(h + 1) * DV_B]
        st = st_ref[0, h]
        scores = (_dot_nt(q, k) * intra_ref[h]).astype(BF16)
        o = _dot(scores, v) + _dot(q, st.astype(BF16)) * rdec_ref[h]
        k_dec = (k.astype(F32) * kdec_ref[h]).astype(BF16)
        st_ref[0, h] = sdec_ref[h] * st + _dot_tn(k_dec, v)
        mu = jnp.mean(o, axis=-1, keepdims=True)
        d = o - mu
        var = jnp.mean(d * d, axis=-1, keepdims=True)
        on = d * lax.rsqrt(var + EPS)
        r = rg_ref[:, h * DV_B:(h + 1) * DV_B]
        o_ref[:, h * DV_B:(h + 1) * DV_B] = (on * (r * jax.nn.sigmoid(r))).astype(BF16)


def _retention_consts(length):
    lg = jnp.log1p(-(2.0 ** (-5.0 - jnp.arange(N_HEADS_B, dtype=F32))))
    n = jnp.arange(length, dtype=F32)
    diff = n[:, None] - n[None, :]
    intra = jnp.where(diff >= 0, jnp.exp(jnp.maximum(diff, 0.0) * lg[:, None, None]), 0.0)
    rdec = jnp.exp((n[None, :] + 1.0) * lg[:, None])[..., None]
    kdec = jnp.exp((length - 1.0 - n)[None, :] * lg[:, None])[..., None]
    sdec = jnp.exp(length * lg)
    return intra, rdec, kdec, sdec


def _retention(q, k, v, rg, s0, streams, chunks, length):
    n = q.shape[0]
    intra, rdec, kdec, sdec = _retention_consts(length)
    row = lambda width: pl.BlockSpec((length, width), lambda b, c: (b * chunks + c, 0))
    const = lambda a: pl.BlockSpec(a.shape, lambda b, c: (0,) * a.ndim)
    st_spec = pl.BlockSpec((1, N_HEADS_B, DK_B, DV_B), lambda b, c: (b, 0, 0, 0))
    return pl.pallas_call(
        _ret_body,
        grid=(streams, chunks),
        in_specs=[row(Q_B), row(Q_B), row(V_B), row(V_B), st_spec,
                  const(intra), const(rdec), const(kdec), pl.BlockSpec(memory_space=pltpu.SMEM)],
        out_specs=[row(V_B), st_spec],
        out_shape=[jax.ShapeDtypeStruct((n, V_B), BF16),
                   jax.ShapeDtypeStruct((streams, N_HEADS_B, DK_B, DV_B), F32)],
        compiler_params=_params(("parallel", "arbitrary")),
        name="retention",
    )(q, k, v, rg, s0, intra, rdec, kdec, sdec)


def _merge_body(x_ref, oa_ref, ob_ref, ga_ref, gb_ref, wa_ref, wb_ref, wo_ref, o_ref):
    ya = _dot(oa_ref[...], wa_ref[...])
    yb = _dot(ob_ref[...], wb_ref[...])
    mix = jax.nn.sigmoid(ga_ref[...]) * ya + jax.nn.sigmoid(gb_ref[...]) * yb
    o_ref[...] = x_ref[...] + _dot(mix.astype(BF16), wo_ref[...])


def _merge(x, oa, ob, ga, gb, wa, wb, wo, tm):
    n = x.shape[0]
    row = lambda width: pl.BlockSpec((tm, width), lambda i: (i, 0))
    full = lambda a: pl.BlockSpec(a.shape, lambda i: (0, 0))
    return pl.pallas_call(
        _merge_body,
        grid=(n // tm,),
        in_specs=[row(D_MODEL), row(Q_A), row(V_B), row(D_MODEL), row(D_MODEL), full(wa), full(wb), full(wo)],
        out_specs=row(D_MODEL),
        out_shape=jax.ShapeDtypeStruct((n, D_MODEL), F32),
        compiler_params=_params(("parallel",)),
        name="merge",
    )(x, oa, ob, ga, gb, wa, wb, wo)


def _top16(s, order):
    t = s.shape[1]
    rank = lax.broadcasted_iota(I32, (PEER_TOPK, t), 0)
    vals = jnp.zeros((PEER_TOPK, t), F32)
    ids = jnp.zeros((PEER_TOPK, t), F32)
    for r in range(PEER_TOPK):
        m = jnp.max(s, axis=0, keepdims=True)
        pick = jnp.min(jnp.where(s == m, order, 3e38), axis=0, keepdims=True)
        s = jnp.where(order == pick, -jnp.inf, s)
        vals = jnp.where(rank == r, m, vals)
        ids = jnp.where(rank == r, pick, ids)
    return vals, ids


def _route_body(x_ref, g_ref, wq_ref, k1_ref, k2_ref, idx_ref, gate_ref, h_ref):
    h_ref[...] = _rmsnorm(x_ref[...], g_ref[...])
    h = h_ref[...].astype(BF16)
    t = h.shape[0]
    q_t = _dot_nt(wq_ref[...], h).astype(BF16)
    iota = lambda rows: lax.broadcasted_iota(I32, (rows, t), 0).astype(F32)
    key_id = iota(N_KEYS)
    flat = jnp.concatenate(
        [iota(PEER_TOPK)] + [a * PEER_TOPK + iota(8) for a in range(1, 8)] + [(8 + iota(8)) * PEER_TOPK],
        axis=0)
    rank = lax.broadcasted_iota(I32, (PEER_TOPK, t), 0)
    idx_rows, gate_rows = [], []
    for hd in range(PEER_HEADS):
        base = hd * PEER_QDIM
        s1 = _dot(k1_ref[...], q_t[base:base + PEER_HALF])
        s2 = _dot(k2_ref[...], q_t[base + PEER_HALF:base + PEER_QDIM])
        v1, i1 = _top16(s1, key_id)
        v2, i2 = _top16(s2, key_id)
        cand = jnp.concatenate(
            [v1[0:1] + v2]
            + [v1[a:a + 1] + v2[0:8] for a in range(1, 8)]
            + [v1[8:16] + v2[0:1]], axis=0)
        cidx = jnp.concatenate(
            [i1[0:1] * N_KEYS + i2]
            + [i1[a:a + 1] * N_KEYS + i2[0:8] for a in range(1, 8)]
            + [i1[8:16] * N_KEYS + i2[0:1]], axis=0)
        best, pos = _top16(cand, flat)
        eidx = jnp.zeros((PEER_TOPK, t), F32)
        for r in range(PEER_TOPK):
            picked = jnp.max(jnp.where(flat == pos[r:r + 1], cidx, -1.0), axis=0, keepdims=True)
            eidx = jnp.where(rank == r, picked, eidx)
        e = jnp.exp(best - best[0:1])
        gate_rows.append(e / jnp.sum(e, axis=0, keepdims=True))
        idx_rows.append(eidx.astype(I32))
    idx_ref[...] = jnp.concatenate(idx_rows, axis=0)
    gate_ref[...] = jnp.concatenate(gate_rows, axis=0)


def _route(x, g, wq_t, k1, k2, tm):
    n = x.shape[0]
    full = lambda a: pl.BlockSpec(a.shape, lambda i: (0, 0))
    col = pl.BlockSpec((PICKS, tm), lambda i: (0, i))
    return pl.pallas_call(
        _route_body,
        grid=(n // tm,),
        in_specs=[pl.BlockSpec((tm, D_MODEL), lambda i: (i, 0)), full(g), full(wq_t), full(k1), full(k2)],
        out_specs=[col, col, pl.BlockSpec((tm, D_MODEL), lambda i: (i, 0))],
        out_shape=[jax.ShapeDtypeStruct((PICKS, n), I32), jax.ShapeDtypeStruct((PICKS, n), F32),
                   jax.ShapeDtypeStruct((n, D_MODEL), F32)],
        compiler_params=_params(("parallel",)),
        name="routing",
    )(x, g, wq_t, k1, k2)


SC_CORES = 2
SC_SUBCORES = 16
SC_LANES = 16
SC_CHUNK = 16
SC_PROMPT_TOKENS = 8192


def _gelu_via_exp(x):
    z = 0.7978845608028654 * (x + 0.044715 * x * x * x)
    return 0.5 * x * (2.0 - 2.0 / (1.0 + jnp.exp(2.0 * z)))


def _sc_experts(table, idx, gate, h):
    n = idx.shape[0]
    workers = SC_CORES * SC_SUBCORES
    per_worker = n // workers
    n_chunks = PICKS // SC_CHUNK
    steps = D_MODEL // SC_LANES
    mesh = plsc.VectorSubcoreMesh(core_axis_name="c", subcore_axis_name="s",
                                  num_cores=SC_CORES, num_subcores=SC_SUBCORES)

    def body(tab_hbm, idx_hbm, gate_hbm, h_hbm, y_hbm, idx_v, gate_v, h_v, y_v, rows_a, rows_b, sem_a, sem_b):
        base = (lax.axis_index("s") * SC_CORES + lax.axis_index("c")) * per_worker
        bufs = ((rows_a, sem_a), (rows_b, sem_b))

        def gather(c, b):
            rows, sem = bufs[b]
            return pltpu.make_async_copy(tab_hbm.at[idx_v.at[pl.ds(c * SC_CHUNK, SC_CHUNK)]], rows, sem)

        def compute(c, b):
            rows, _ = bufs[b]

            def dot_step(j, accs):
                hj = h_v[pl.ds(j * SC_LANES, SC_LANES)]
                return tuple(accs[k] + rows[k, pl.ds(j * SC_LANES, SC_LANES)] * hj for k in range(SC_CHUNK))

            accs = lax.fori_loop(0, steps, dot_step,
                                 tuple(jnp.zeros((SC_LANES,), F32) for _ in range(SC_CHUNK)))
            ws = []
            for k in range(SC_CHUNK):
                s = jnp.full((SC_LANES,), jnp.sum(accs[k]), F32)
                g = plsc.load_gather(gate_v, [jnp.full((SC_LANES,), c * SC_CHUNK + k, I32)])
                ws.append(_gelu_via_exp(s) * g)

            def sum_step(j, carry):
                acc = y_v[pl.ds(j * SC_LANES, SC_LANES)]
                for k in range(SC_CHUNK):
                    acc = acc + ws[k] * rows[k, pl.ds(D_MODEL + j * SC_LANES, SC_LANES)]
                y_v[pl.ds(j * SC_LANES, SC_LANES)] = acc
                return carry

            lax.fori_loop(0, steps, sum_step, 0)

        @pl.loop(0, per_worker)
        def _(i):
            t = base + i
            pltpu.sync_copy(idx_hbm.at[t], idx_v)
            pltpu.sync_copy(gate_hbm.at[t], gate_v)
            pltpu.sync_copy(h_hbm.at[t], h_v)

            @pl.loop(0, steps)
            def _(j):
                y_v[pl.ds(j * SC_LANES, SC_LANES)] = jnp.zeros((SC_LANES,), F32)

            gather(0, 0).start()

            @pl.loop(0, n_chunks // 2)
            def _(p):
                c0 = p * 2
                gather(c0 + 1, 1).start()
                gather(c0, 0).wait()
                compute(c0, 0)

                @pl.when(p + 1 < n_chunks // 2)
                def _():
                    gather(c0 + 2, 0).start()

                gather(c0 + 1, 1).wait()
                compute(c0 + 1, 1)

            pltpu.sync_copy(y_v, y_hbm.at[t])

    return pl.kernel(
        body, out_type=jax.ShapeDtypeStruct((n, D_MODEL), F32), mesh=mesh,
        scratch_types=[pltpu.VMEM((PICKS,), I32), pltpu.VMEM((PICKS,), F32),
                       pltpu.VMEM((D_MODEL,), F32), pltpu.VMEM((D_MODEL,), F32),
                       pltpu.VMEM((SC_CHUNK, 2 * D_MODEL), F32), pltpu.VMEM((SC_CHUNK, 2 * D_MODEL), F32),
                       pltpu.SemaphoreType.DMA, pltpu.SemaphoreType.DMA],
        compiler_params=pltpu.CompilerParams(needs_layout_passes=False),
        name="experts_sc",
    )(table, idx, gate, h)


def _finish_body(x_ref, y_ref, gf_ref, rest_ref, o_ref):
    del rest_ref
    o_ref[...] = _rmsnorm(x_ref[...] + y_ref[...], gf_ref[...])


def _finish(x, y, g_final, rest, tm):
    row = pl.BlockSpec((tm, D_MODEL), lambda i: (i, 0))
    return pl.pallas_call(
        _finish_body,
        grid=(y.shape[0] // tm,),
        in_specs=[row, row, pl.BlockSpec(g_final.shape, lambda i: (0, 0)), pl.BlockSpec(memory_space=pl.ANY)],
        out_specs=row,
        out_shape=jax.ShapeDtypeStruct(rest.shape, F32),
        input_output_aliases={3: 0},
        compiler_params=_params(("parallel",)),
        name="finish",
    )(x, y, g_final, rest)


EXP_TOKENS = 256
EXP_SLOTS = 4


def _expert_body(x_ref, g_ref, gf_ref, idx_hbm, gate_ref, tab_hbm, o_ref,
                 idx_smem, buf0, buf1, buf2, buf3, h_buf, y_buf, sems, idx_sem, *, first_block):
    bufs = (buf0, buf1, buf2, buf3)
    ahead = EXP_SLOTS - 1
    i = pl.program_id(0) + first_block
    idx_copy = pltpu.make_async_copy(idx_hbm.at[pl.ds(i * (EXP_TOKENS * PICKS), EXP_TOKENS * PICKS)],
                                     idx_smem, idx_sem)
    idx_copy.start()
    h_buf[...] = _rmsnorm(x_ref[...], g_ref[...])
    idx_copy.wait()

    def issue(t, slot):
        for k in range(PICKS):
            e = idx_smem[t * PICKS + k]
            pltpu.make_async_copy(tab_hbm.at[e], bufs[slot].at[pl.ds(k, 1), :], sems.at[slot]).start()

    def wait_slot(slot):
        pltpu.make_async_copy(bufs[(slot + 1) % EXP_SLOTS], bufs[slot], sems.at[slot]).wait()

    lane = lax.broadcasted_iota(I32, (PICKS, EXP_TOKENS), 1)

    def compute(t, slot):
        h = h_buf[pl.ds(t, 1), :]
        u = bufs[slot][:, :D_MODEL]
        s = jnp.sum(u * h, axis=-1, keepdims=True)
        gate = jnp.sum(jnp.where(lane == t, gate_ref[...], 0.0), axis=-1, keepdims=True)
        w = gate * jax.nn.gelu(s)
        v = bufs[slot][:, D_MODEL:]
        y_buf[pl.ds(t, 1), :] = jnp.sum(w * v, axis=0, keepdims=True)

    def group(t0, issue_upto):
        for s in range(EXP_SLOTS):
            wait_slot(s)
            if s < issue_upto:
                issue(t0 + s + ahead, (s + ahead) % EXP_SLOTS)
            compute(t0 + s, s)

    for s in range(ahead):
        issue(s, s)

    def steady(q, carry):
        group(q * EXP_SLOTS, EXP_SLOTS)
        return carry

    n_groups = EXP_TOKENS // EXP_SLOTS
    lax.fori_loop(0, n_groups - 1, steady, 0)
    group((n_groups - 1) * EXP_SLOTS, EXP_SLOTS - ahead)
    o_ref[...] = _rmsnorm(x_ref[...] + y_buf[...], gf_ref[...])


def _experts(x, g, g_final, idx_flat, gate_t, table, first_token):
    n = x.shape[0]
    first_block = first_token // EXP_TOKENS
    full = lambda a: pl.BlockSpec(a.shape, lambda i: (0, 0))
    row = pl.BlockSpec((EXP_TOKENS, D_MODEL), lambda i: (i + first_block, 0))
    return pl.pallas_call(
        functools.partial(_expert_body, first_block=first_block),
        grid=(n // EXP_TOKENS - first_block,),
        in_specs=[row, full(g), full(g_final),
                  pl.BlockSpec(memory_space=pl.ANY),
                  pl.BlockSpec((PICKS, EXP_TOKENS), lambda i: (0, i + first_block)),
                  pl.BlockSpec(memory_space=pl.ANY)],
        out_specs=row,
        out_shape=jax.ShapeDtypeStruct((n, D_MODEL), F32),
        scratch_shapes=[pltpu.SMEM((EXP_TOKENS * PICKS,), I32)]
        + [pltpu.VMEM((PICKS, 2 * D_MODEL), F32) for _ in range(EXP_SLOTS)]
        + [pltpu.VMEM((EXP_TOKENS, D_MODEL), F32),
           pltpu.VMEM((EXP_TOKENS, D_MODEL), F32),
           pltpu.SemaphoreType.DMA((EXP_SLOTS,)),
           pltpu.SemaphoreType.DMA(())],
        compiler_params=pltpu.CompilerParams(dimension_semantics=("arbitrary",), vmem_limit_bytes=VMEM_LIMIT,
                                             disable_bounds_checks=True),
        name="experts",
    )(x, g, g_final, idx_flat, gate_t, table)


def _attention_masks(tq):
    i = jnp.arange(tq, dtype=I32)[:, None]
    r = jnp.arange(ATT_KEYS, dtype=I32)[None, :]
    negdist = -jnp.abs(WINDOW + i - r).astype(F32)
    return negdist, i, r


def _layer(x, n_groups, rows, swa_inputs, ret_state, ret_len, w, tm, n_sc):
    (norm_mix, w_in, b_gate, sinks, wa, wb, wo, norm_ffn, wq_t, k1, k2, table, norm_final) = w
    qa, ka, va, qb, kb, vb, rg, ga, gb = _project(x, norm_mix, w_in, b_gate, tm)
    oa, k_rows, v_rows = swa_inputs(qa, ka, va, sinks)
    ob, s_fin = _retention(qb, kb, vb, rg, ret_state, n_groups, rows // ret_len, ret_len)
    x1 = _merge(x, oa, ob, ga, gb, wa, wb, wo, tm)
    idx_t, gate_t, h2 = _route(x1, norm_ffn, wq_t, k1, k2, tm)
    y = _experts(x1, norm_ffn, norm_final, idx_t.T.reshape(-1), gate_t, table[:, None, :], n_sc)
    if n_sc:
        y_sc = _sc_experts(table, idx_t[:, :n_sc].T, gate_t[:, :n_sc].T, h2)
        y = _finish(x1, y_sc, norm_final, y, tm)
    return y, k_rows, v_rows, s_fin


def kernel(x_prompt, x_sample, cache_swa_k, cache_swa_v, state_ret, norm_mix, w_in, b_gate, attn_sinks,
           w_branch_a, w_branch_b, w_out, norm_ffn, peer_w_q, peer_sub_k1, peer_sub_k2, peer_u, peer_v,
           norm_final):
    batch, seq, _ = x_prompt.shape
    dec_batch, dec_seq, _ = x_sample.shape
    assert norm_mix.shape[0] == 1, "single-layer trunk"
    table = jnp.concatenate([peer_u[0], peer_v[0]], axis=1)
    w = (norm_mix[0][None], w_in[0].astype(BF16), b_gate[0][None], attn_sinks[0],
         w_branch_a[0].astype(BF16), w_branch_b[0].astype(BF16), w_out[0].astype(BF16),
         norm_ffn[0][None], peer_w_q[0].T.astype(BF16), peer_sub_k1[0].astype(BF16),
         peer_sub_k2[0].astype(BF16), table, norm_final[None])

    def swa_prompt(qa, ka, va, sinks):
        tq = 2 * CHUNK
        negdist, i, r = _attention_masks(tq)
        lq, lk = i // CHUNK, r // CHUNK
        band = (lk >= lq) & (lk <= lq + 2)
        allow = jnp.stack([band & (r >= WINDOW), band]).astype(F32)
        oa = _swa(sinks, qa, ka, va, negdist, allow, batch, seq // tq, tq)
        tail = lambda a: a.reshape(batch, seq, N_KV_A, HEAD_DIM_A)[:, seq - WINDOW:]
        return oa, tail(ka), tail(va)

    def swa_sample(qa, ka, va, sinks):
        negdist, i, r = _attention_masks(dec_seq)
        visible = jnp.broadcast_to(r < WINDOW + dec_seq, (dec_seq, ATT_KEYS))
        allow = jnp.stack([visible, visible]).astype(F32)
        pad = jnp.zeros((dec_batch, ATT_KEYS - WINDOW - dec_seq, KV_A), F32)
        k_all = jnp.concatenate([cache_swa_k[0].reshape(dec_batch, WINDOW, KV_A),
                                 ka.reshape(dec_batch, dec_seq, KV_A), pad], axis=1)
        v_all = jnp.concatenate([cache_swa_v[0].reshape(dec_batch, WINDOW, KV_A),
                                 va.reshape(dec_batch, dec_seq, KV_A), pad], axis=1)
        oa = _swa(sinks, qa, k_all.reshape(-1, KV_A), v_all.reshape(-1, KV_A), negdist, allow,
                  dec_batch, 1, dec_seq)
        tail = lambda a: a[:, dec_seq:WINDOW + dec_seq].reshape(dec_batch, WINDOW, N_KV_A, HEAD_DIM_A)
        return oa, tail(k_all), tail(v_all)

    ys, ks, vs, ss = _layer(x_sample.reshape(dec_batch * dec_seq, D_MODEL), dec_batch, dec_seq, swa_sample,
                            state_ret[0], dec_seq, w, 128, 0)

    s0 = jnp.zeros((batch, N_HEADS_B, DK_B, DV_B), F32)
    yp, kp, vp, sp = _layer(x_prompt.reshape(batch * seq, D_MODEL), batch, seq, swa_prompt, s0, 256, w, 256,
                            SC_PROMPT_TOKENS)

    return (yp.reshape(batch, seq, D_MODEL), ys.reshape(dec_batch, dec_seq, D_MODEL),
            kp[None], vp[None], sp[None], ks[None], vs[None], ss[None])
```

```python
import functools

import jax
import jax.numpy as jnp
from jax import lax
from jax.experimental import pallas as pl
from jax.experimental.pallas import tpu as pltpu
from jax.experimental.pallas import tpu_sc as plsc

F32 = jnp.float32
BF16 = jnp.bfloat16
I32 = jnp.int32

D_MODEL = 1024
CHUNK = 64
EPS = 1e-6
NEG_INF = -1e30
PAST_LEN = 2048

N_HEADS_A = 8
N_KV_A = 2
GROUP_A = N_HEADS_A // N_KV_A
HEAD_DIM_A = 64
WINDOW = 128
N_HEADS_B = 4
DK_B = 128
DV_B = 256
Q_A = N_HEADS_A * HEAD_DIM_A
KV_A = N_KV_A * HEAD_DIM_A
Q_B = N_HEADS_B * DK_B
V_B = N_HEADS_B * DV_B
D_IN = Q_A + 2 * KV_A + 2 * Q_B + 2 * V_B + 2 * D_MODEL
OFF_QA = 0
OFF_KA = OFF_QA + Q_A
OFF_VA = OFF_KA + KV_A
OFF_QB = OFF_VA + KV_A
OFF_KB = OFF_QB + Q_B
OFF_VB = OFF_KB + Q_B
OFF_RG = OFF_VB + V_B
OFF_GA = OFF_RG + V_B
OFF_GB = OFF_GA + D_MODEL

N_KEYS = 128
N_EXPERTS = N_KEYS * N_KEYS
PEER_HEADS = 8
PEER_QDIM = 256
PEER_HALF = PEER_QDIM // 2
PEER_TOPK = 16
PICKS = PEER_HEADS * PEER_TOPK

LANES = 128
ATT_KEYS = 2 * WINDOW
VMEM_LIMIT = 56 * 1024 * 1024


def _rmsnorm(x, g):
    return x * lax.rsqrt(jnp.mean(x * x, axis=-1, keepdims=True) + EPS) * g


def _dot(a, b):
    return jnp.dot(a, b, preferred_element_type=F32)


def _dot_nt(a, b):
    return lax.dot_general(a, b, (((1,), (1,)), ((), ())), preferred_element_type=F32)


def _dot_tn(a, b):
    return lax.dot_general(a, b, (((0,), (0,)), ((), ())), preferred_element_type=F32)


def _params(sem):
    return pltpu.CompilerParams(dimension_semantics=sem, vmem_limit_bytes=VMEM_LIMIT)


def _proj_body(x_ref, g_ref, w_ref, bg_ref, qa_ref, ka_ref, va_ref, qb_ref, kb_ref, vb_ref,
               rg_ref, ga_ref, gb_ref):
    h = _rmsnorm(x_ref[...], g_ref[...]).astype(BF16)

    def mm(lo, width):
        return _dot(h, w_ref[:, lo:lo + width])

    qa_ref[...] = (mm(OFF_QA, Q_A) * (HEAD_DIM_A ** -0.5)).astype(BF16)
    ka_ref[...] = mm(OFF_KA, KV_A)
    va_ref[...] = mm(OFF_VA, KV_A)
    qb_ref[...] = mm(OFF_QB, Q_B).astype(BF16)
    kb_ref[...] = (mm(OFF_KB, Q_B) * (DK_B ** -0.5)).astype(BF16)
    vb_ref[...] = mm(OFF_VB, V_B).astype(BF16)
    rg_ref[...] = mm(OFF_RG, V_B)
    ga_ref[...] = mm(OFF_GA, D_MODEL) + bg_ref[:, :D_MODEL]
    gb_ref[...] = mm(OFF_GB, D_MODEL) + bg_ref[:, D_MODEL:]


def _project(x, g, w_bf16, b_gate, tm):
    n = x.shape[0]
    row = lambda width: pl.BlockSpec((tm, width), lambda i: (i, 0))
    full = lambda a: pl.BlockSpec(a.shape, lambda i: (0, 0))
    widths = (Q_A, KV_A, KV_A, Q_B, Q_B, V_B, V_B, D_MODEL, D_MODEL)
    dtypes = (BF16, F32, F32, BF16, BF16, BF16, F32, F32, F32)
    return pl.pallas_call(
        _proj_body,
        grid=(n // tm,),
        in_specs=[row(D_MODEL), full(g), full(w_bf16), full(b_gate)],
        out_specs=[row(w) for w in widths],
        out_shape=[jax.ShapeDtypeStruct((n, w), d) for w, d in zip(widths, dtypes)],
        compiler_params=_params(("parallel",)),
        name="projection",
    )(x, g, w_bf16, b_gate)


def _swa_body(sink_ref, q_ref, kp_ref, kc_ref, vp_ref, vc_ref, nd_ref, al_ref, o_ref):
    k = jnp.concatenate([kp_ref[...], kc_ref[...]], axis=0)
    v = jnp.concatenate([vp_ref[...], vc_ref[...]], axis=0)
    lane = lax.broadcasted_iota(I32, k.shape, 1)
    low = lane < HEAD_DIM_A
    k_sw = pltpu.roll(k, HEAD_DIM_A, axis=1)
    v_sw = pltpu.roll(v, HEAD_DIM_A, axis=1)
    negdist = nd_ref[...]
    allowed = al_ref[0] > 0.5
    for kv in range(N_KV_A):
        own = low if kv == 0 else jnp.logical_not(low)
        k_rep = jnp.where(own, k, k_sw)
        v_rep = jnp.where(own, v, v_sw)
        zero = jnp.zeros_like(k_rep)
        kk = jnp.concatenate([jnp.where(low, k_rep, zero), jnp.where(low, zero, k_rep)], axis=0).astype(BF16)
        vv = jnp.concatenate([jnp.where(low, v_rep, zero), jnp.where(low, zero, v_rep)], axis=0).astype(BF16)
        for a in range(GROUP_A // 2):
            col = kv * GROUP_A * HEAD_DIM_A + a * LANES
            s2 = _dot_nt(q_ref[:, col:col + LANES], kk)
            ps = []
            for u in range(2):
                head = kv * GROUP_A + 2 * a + u
                slope = 2.0 ** (-8.0 * (head + 1) / N_HEADS_A)
                sink = sink_ref[head]
                s = s2[:, u * ATT_KEYS:(u + 1) * ATT_KEYS] + slope * negdist
                s = jnp.where(allowed, s, NEG_INF)
                m = jnp.maximum(jnp.max(s, axis=-1, keepdims=True), sink)
                p = jnp.exp(s - m)
                denom = jnp.sum(p, axis=-1, keepdims=True) + jnp.exp(sink - m)
                ps.append((p / denom).astype(BF16))
            o = _dot(jnp.concatenate(ps, axis=1), vv)
            o_ref[:, col:col + LANES] = o.astype(BF16)


def _swa(sinks, q, k, v, negdist, allow, groups, q_blocks, tq):
    n = q.shape[0]
    kblocks = k.shape[0] // LANES // groups
    shift = kblocks - q_blocks
    prev = lambda b, j: (b * kblocks + jnp.maximum(j + shift - 1, 0), 0)
    cur = lambda b, j: (b * kblocks + j + shift, 0)
    kv_spec = lambda f: pl.BlockSpec((LANES, KV_A), f)
    return pl.pallas_call(
        _swa_body,
        grid=(groups, q_blocks),
        in_specs=[pl.BlockSpec(memory_space=pltpu.SMEM),
                  pl.BlockSpec((tq, Q_A), lambda b, j: (b * q_blocks + j, 0)),
                  kv_spec(prev), kv_spec(cur), kv_spec(prev), kv_spec(cur),
                  pl.BlockSpec((tq, ATT_KEYS), lambda b, j: (0, 0)),
                  pl.BlockSpec((1, tq, ATT_KEYS), lambda b, j: (jnp.minimum(j + shift, 1), 0, 0))],
        out_specs=pl.BlockSpec((tq, Q_A), lambda b, j: (b * q_blocks + j, 0)),
        out_shape=jax.ShapeDtypeStruct((n, Q_A), BF16),
        compiler_params=_params(("parallel", "parallel")),
        name="attention",
    )(sinks, q, k, k, v, v, negdist, allow)


def _ret_body(q_ref, k_ref, v_ref, rg_ref, s0_ref, intra_ref, rdec_ref, kdec_ref, sdec_ref,
              o_ref, st_ref):
    c = pl.program_id(1)

    @pl.when(c == 0)
    def _():
        st_ref[...] = s0_ref[...]

    for h in range(N_HEADS_B):
        q = q_ref[:, h * DK_B:(h + 1) * DK_B]
        k = k_ref[:, h * DK_B:(h + 1) * DK_B]
        v = v_ref[:, h * DV_B:(h + 1) * DV_B]
        st = st_ref[0, h]
        scores = (_dot_nt(q, k) * intra_ref[h]).astype(BF16)
        o = _dot(scores, v) + _dot(q, st.astype(BF16)) * rdec_ref[h]
        k_dec = (k.astype(F32) * kdec_ref[h]).astype(BF16)
        st_ref[0, h] = sdec_ref[h] * st + _dot_tn(k_dec, v)
        mu = jnp.mean(o, axis=-1, keepdims=True)
        d = o - mu
        var = jnp.mean(d * d, axis=-1, keepdims=True)
        on = d * lax.rsqrt(var + EPS)
        r = rg_ref[:, h * DV_B:(h + 1) * DV_B]
        o_ref[:, h * DV_B:(h + 1) * DV_B] = (on * (r * jax.nn.sigmoid(r))).astype(BF16)


def _retention_consts(length):
    lg = jnp.log1p(-(2.0 ** (-5.0 - jnp.arange(N_HEADS_B, dtype=F32))))
    n = jnp.arange(length, dtype=F32)
    diff = n[:, None] - n[None, :]
    intra = jnp.where(diff >= 0, jnp.exp(jnp.maximum(diff, 0.0) * lg[:, None, None]), 0.0)
    rdec = jnp.exp((n[None, :] + 1.0) * lg[:, None])[..., None]
    kdec = jnp.exp((length - 1.0 - n)[None, :] * lg[:, None])[..., None]
    sdec = jnp.exp(length * lg)
    return intra, rdec, kdec, sdec


def _retention(q, k, v, rg, s0, streams, chunks, length):
    n = q.shape[0]
    intra, rdec, kdec, sdec = _retention_consts(length)
    row = lambda width: pl.BlockSpec((length, width), lambda b, c: (b * chunks + c, 0))
    const = lambda a: pl.BlockSpec(a.shape, lambda b, c: (0,) * a.ndim)
    st_spec = pl.BlockSpec((1, N_HEADS_B, DK_B, DV_B), lambda b, c: (b, 0, 0, 0))
    return pl.pallas_call(
        _ret_body,
        grid=(streams, chunks),
        in_specs=[row(Q_B), row(Q_B), row(V_B), row(V_B), st_spec,
                  const(intra), const(rdec), const(kdec), pl.BlockSpec(memory_space=pltpu.SMEM)],
        out_specs=[row(V_B), st_spec],
        out_shape=[jax.ShapeDtypeStruct((n, V_B), BF16),
                   jax.ShapeDtypeStruct((streams, N_HEADS_B, DK_B, DV_B), F32)],
        compiler_params=_params(("parallel", "arbitrary")),
        name="retention",
    )(q, k, v, rg, s0, intra, rdec, kdec, sdec)


def _merge_body(x_ref, oa_ref, ob_ref, ga_ref, gb_ref, wa_ref, wb_ref, wo_ref, o_ref):
    ya = _dot(oa_ref[...], wa_ref[...])
    yb = _dot(ob_ref[...], wb_ref[...])
    mix = jax.nn.sigmoid(ga_ref[...]) * ya + jax.nn.sigmoid(gb_ref[...]) * yb
    o_ref[...] = x_ref[...] + _dot(mix.astype(BF16), wo_ref[...])


def _merge(x, oa, ob, ga, gb, wa, wb, wo, tm):
    n = x.shape[0]
    row = lambda width: pl.BlockSpec((tm, width), lambda i: (i, 0))
    full = lambda a: pl.BlockSpec(a.shape, lambda i: (0, 0))
    return pl.pallas_call(
        _merge_body,
        grid=(n // tm,),
        in_specs=[row(D_MODEL), row(Q_A), row(V_B), row(D_MODEL), row(D_MODEL), full(wa), full(wb), full(wo)],
        out_specs=row(D_MODEL),
        out_shape=jax.ShapeDtypeStruct((n, D_MODEL), F32),
        compiler_params=_params(("parallel",)),
        name="merge",
    )(x, oa, ob, ga, gb, wa, wb, wo)


def _top16(s, order):
    t = s.shape[1]
    rank = lax.broadcasted_iota(I32, (PEER_TOPK, t), 0)
    vals = jnp.zeros((PEER_TOPK, t), F32)
    ids = jnp.zeros((PEER_TOPK, t), F32)
    for r in range(PEER_TOPK):
        m = jnp.max(s, axis=0, keepdims=True)
        pick = jnp.min(jnp.where(s == m, order, 3e38), axis=0, keepdims=True)
        s = jnp.where(order == pick, -jnp.inf, s)
        vals = jnp.where(rank == r, m, vals)
        ids = jnp.where(rank == r, pick, ids)
    return vals, ids


def _route_body(x_ref, g_ref, wq_ref, k1_ref, k2_ref, idx_ref, gate_ref, h_ref):
    h_ref[...] = _rmsnorm(x_ref[...], g_ref[...])
    h = h_ref[...].astype(BF16)
    t = h.shape[0]
    q_t = _dot_nt(wq_ref[...], h).astype(BF16)
    iota = lambda rows: lax.broadcasted_iota(I32, (rows, t), 0).astype(F32)
    key_id = iota(N_KEYS)
    flat = jnp.concatenate(
        [iota(PEER_TOPK)] + [a * PEER_TOPK + iota(8) for a in range(1, 8)] + [(8 + iota(8)) * PEER_TOPK],
        axis=0)
    rank = lax.broadcasted_iota(I32, (PEER_TOPK, t), 0)
    idx_rows, gate_rows = [], []
    for hd in range(PEER_HEADS):
        base = hd * PEER_QDIM
        s1 = _dot(k1_ref[...], q_t[base:base + PEER_HALF])
        s2 = _dot(k2_ref[...], q_t[base + PEER_HALF:base + PEER_QDIM])
        v1, i1 = _top16(s1, key_id)
        v2, i2 = _top16(s2, key_id)
        cand = jnp.concatenate(
            [v1[0:1] + v2]
            + [v1[a:a + 1] + v2[0:8] for a in range(1, 8)]
            + [v1[8:16] + v2[0:1]], axis=0)
        cidx = jnp.concatenate(
            [i1[0:1] * N_KEYS + i2]
            + [i1[a:a + 1] * N_KEYS + i2[0:8] for a in range(1, 8)]
            + [i1[8:16] * N_KEYS + i2[0:1]], axis=0)
        best, pos = _top16(cand, flat)
        eidx = jnp.zeros((PEER_TOPK, t), F32)
        for r in range(PEER_TOPK):
            picked = jnp.max(jnp.where(flat == pos[r:r + 1], cidx, -1.0), axis=0, keepdims=True)
            eidx = jnp.where(rank == r, picked, eidx)
        e = jnp.exp(best - best[0:1])
        gate_rows.append(e / jnp.sum(e, axis=0, keepdims=True))
        idx_rows.append(eidx.astype(I32))
    idx_ref[...] = jnp.concatenate(idx_rows, axis=0)
    gate_ref[...] = jnp.concatenate(gate_rows, axis=0)


def _route(x, g, wq_t, k1, k2, tm):
    n = x.shape[0]
    full = lambda a: pl.BlockSpec(a.shape, lambda i: (0, 0))
    col = pl.BlockSpec((PICKS, tm), lambda i: (0, i))
    return pl.pallas_call(
        _route_body,
        grid=(n // tm,),
        in_specs=[pl.BlockSpec((tm, D_MODEL), lambda i: (i, 0)), full(g), full(wq_t), full(k1), full(k2)],
        out_specs=[col, col, pl.BlockSpec((tm, D_MODEL), lambda i: (i, 0))],
        out_shape=[jax.ShapeDtypeStruct((PICKS, n), I32), jax.ShapeDtypeStruct((PICKS, n), F32),
                   jax.ShapeDtypeStruct((n, D_MODEL), F32)],
        compiler_params=_params(("parallel",)),
        name="routing",
    )(x, g, wq_t, k1, k2)


SC_CORES = 2
SC_SUBCORES = 16
SC_LANES = 16
SC_CHUNK = 16
SC_PROMPT_TOKENS = 13824


def _gelu_via_exp(x):
    z = 0.7978845608028654 * (x + 0.044715 * x * x * x)
    return 0.5 * x * (2.0 - 2.0 / (1.0 + jnp.exp(2.0 * z)))


def _sc_experts(table, idx, gate, h):
    n = idx.shape[0]
    workers = SC_CORES * SC_SUBCORES
    per_worker = n // workers
    n_chunks = PICKS // SC_CHUNK
    steps = D_MODEL // SC_LANES
    mesh = plsc.VectorSubcoreMesh(core_axis_name="c", subcore_axis_name="s",
                                  num_cores=SC_CORES, num_subcores=SC_SUBCORES)

    def body(tab_hbm, idx_hbm, gate_hbm, h_hbm, y_hbm, idx_v, gate_v, h_v, y_v, rows_a, rows_b, sem_a, sem_b):
        base = (lax.axis_index("s") * SC_CORES + lax.axis_index("c")) * per_worker
        bufs = ((rows_a, sem_a), (rows_b, sem_b))

        def gather(c, b):
            rows, sem = bufs[b]
            return pltpu.make_async_copy(tab_hbm.at[idx_v.at[pl.ds(c * SC_CHUNK, SC_CHUNK)]], rows, sem)

        def compute(c, b):
            rows, _ = bufs[b]

            def dot_step(j, accs):
                hj = h_v[pl.ds(j * SC_LANES, SC_LANES)]
                return tuple(accs[k] + rows[k, pl.ds(j * SC_LANES, SC_LANES)] * hj for k in range(SC_CHUNK))

            accs = lax.fori_loop(0, steps, dot_step,
                                 tuple(jnp.zeros((SC_LANES,), F32) for _ in range(SC_CHUNK)))
            ws = []
            for k in range(SC_CHUNK):
                s = jnp.full((SC_LANES,), jnp.sum(accs[k]), F32)
                g = plsc.load_gather(gate_v, [jnp.full((SC_LANES,), c * SC_CHUNK + k, I32)])
                ws.append(_gelu_via_exp(s) * g)

            def sum_step(j, carry):
                acc = y_v[pl.ds(j * SC_LANES, SC_LANES)]
                for k in range(SC_CHUNK):
                    acc = acc + ws[k] * rows[k, pl.ds(D_MODEL + j * SC_LANES, SC_LANES)]
                y_v[pl.ds(j * SC_LANES, SC_LANES)] = acc
                return carry

            lax.fori_loop(0, steps, sum_step, 0)

        @pl.loop(0, per_worker)
        def _(i):
            t = base + i
            pltpu.sync_copy(idx_hbm.at[t], idx_v)
            pltpu.sync_copy(gate_hbm.at[t], gate_v)
            pltpu.sync_copy(h_hbm.at[t], h_v)

            @pl.loop(0, steps)
            def _(j):
                y_v[pl.ds(j * SC_LANES, SC_LANES)] = jnp.zeros((SC_LANES,), F32)

            gather(0, 0).start()

            @pl.loop(0, n_chunks // 2)
            def _(p):
                c0 = p * 2
                gather(c0 + 1, 1).start()
                gather(c0, 0).wait()
                compute(c0, 0)

                @pl.when(p + 1 < n_chunks // 2)
                def _():
                    gather(c0 + 2, 0).start()

                gather(c0 + 1, 1).wait()
                compute(c0 + 1, 1)

            pltpu.sync_copy(y_v, y_hbm.at[t])

    return pl.kernel(
        body, out_type=jax.ShapeDtypeStruct((n, D_MODEL), F32), mesh=mesh,
        scratch_types=[pltpu.VMEM((PICKS,), I32), pltpu.VMEM((PICKS,), F32),
                       pltpu.VMEM((D_MODEL,), F32), pltpu.VMEM((D_MODEL,), F32),
                       pltpu.VMEM((SC_CHUNK, 2 * D_MODEL), F32), pltpu.VMEM((SC_CHUNK, 2 * D_MODEL), F32),
                       pltpu.SemaphoreType.DMA, pltpu.SemaphoreType.DMA],
        compiler_params=pltpu.CompilerParams(needs_layout_passes=False),
        name="experts_sc",
    )(table, idx, gate, h)


def _finish_body(x_ref, y_ref, gf_ref, rest_ref, o_ref):
    del rest_ref
    o_ref[...] = _rmsnorm(x_ref[...] + y_ref[...], gf_ref[...])


def _finish(x, y, g_final, rest, tm):
    row = pl.BlockSpec((tm, D_MODEL), lambda i: (i, 0))
    return pl.pallas_call(
        _finish_body,
        grid=(y.shape[0] // tm,),
        in_specs=[row, row, pl.BlockSpec(g_final.shape, lambda i: (0, 0)), pl.BlockSpec(memory_space=pl.ANY)],
        out_specs=row,
        out_shape=jax.ShapeDtypeStruct(rest.shape, F32),
        input_output_aliases={3: 0},
        compiler_params=_params(("parallel",)),
        name="finish",
    )(x, y, g_final, rest)


EXP_TOKENS = 256
EXP_SLOTS = 4


def _expert_body(x_ref, g_ref, gf_ref, idx_hbm, gate_ref, tab_hbm, o_ref,
                 idx_smem, buf0, buf1, buf2, buf3, h_buf, y_buf, sems, idx_sem, *, first_block):
    bufs = (buf0, buf1, buf2, buf3)
    ahead = EXP_SLOTS - 1
    i = pl.program_id(0) + first_block
    idx_copy = pltpu.make_async_copy(idx_hbm.at[pl.ds(i * (EXP_TOKENS * PICKS), EXP_TOKENS * PICKS)],
                                     idx_smem, idx_sem)
    idx_copy.start()
    h_buf[...] = _rmsnorm(x_ref[...], g_ref[...])
    idx_copy.wait()

    def issue(t, slot):
        for k in range(PICKS):
            e = idx_smem[t * PICKS + k]
            pltpu.async_copy(tab_hbm.at[e], bufs[slot].at[pl.ds(k, 1), :], sems.at[slot], priority=k % 2)

    def wait_slot(slot):
        pltpu.make_async_copy(bufs[(slot + 1) % EXP_SLOTS], bufs[slot], sems.at[slot]).wait()

    lane = lax.broadcasted_iota(I32, (PICKS, EXP_TOKENS), 1)

    def compute(t, slot):
        h = h_buf[pl.ds(t, 1), :]
        u = bufs[slot][:, :D_MODEL]
        s = jnp.sum(u * h, axis=-1, keepdims=True)
        gate = jnp.sum(jnp.where(lane == t, gate_ref[...], 0.0), axis=-1, keepdims=True)
        w = gate * jax.nn.gelu(s)
        v = bufs[slot][:, D_MODEL:]
        y_buf[pl.ds(t, 1), :] = jnp.sum(w * v, axis=0, keepdims=True)

    def group(t0, issue_upto):
        for s in range(EXP_SLOTS):
            wait_slot(s)
            if s < issue_upto:
                issue(t0 + s + ahead, (s + ahead) % EXP_SLOTS)
            compute(t0 + s, s)

    for s in range(ahead):
        issue(s, s)

    def steady(q, carry):
        group(q * EXP_SLOTS, EXP_SLOTS)
        return carry

    n_groups = EXP_TOKENS // EXP_SLOTS
    lax.fori_loop(0, n_groups - 1, steady, 0)
    group((n_groups - 1) * EXP_SLOTS, EXP_SLOTS - ahead)
    o_ref[...] = _rmsnorm(x_ref[...] + y_buf[...], gf_ref[...])


def _experts(x, g, g_final, idx_flat, gate_t, table, first_token):
    n = x.shape[0]
    first_block = first_token // EXP_TOKENS
    full = lambda a: pl.BlockSpec(a.shape, lambda i: (0, 0))
    row = pl.BlockSpec((EXP_TOKENS, D_MODEL), lambda i: (i + first_block, 0))
    return pl.pallas_call(
        functools.partial(_expert_body, first_block=first_block),
        grid=(n // EXP_TOKENS - first_block,),
        in_specs=[row, full(g), full(g_final),
                  pl.BlockSpec(memory_space=pl.ANY),
                  pl.BlockSpec((PICKS, EXP_TOKENS), lambda i: (0, i + first_block)),
                  pl.BlockSpec(memory_space=pl.ANY)],
        out_specs=row,
        out_shape=jax.ShapeDtypeStruct((n, D_MODEL), F32),
        scratch_shapes=[pltpu.SMEM((EXP_TOKENS * PICKS,), I32)]
        + [pltpu.VMEM((PICKS, 2 * D_MODEL), F32) for _ in range(EXP_SLOTS)]
        + [pltpu.VMEM((EXP_TOKENS, D_MODEL), F32),
           pltpu.VMEM((EXP_TOKENS, D_MODEL), F32),
           pltpu.SemaphoreType.DMA((EXP_SLOTS,)),
           pltpu.SemaphoreType.DMA(())],
        compiler_params=pltpu.CompilerParams(dimension_semantics=("arbitrary",), vmem_limit_bytes=VMEM_LIMIT,
                                             disable_bounds_checks=True),
        name="experts",
    )(x, g, g_final, idx_flat, gate_t, table)


def _attention_masks(tq):
    i = jnp.arange(tq, dtype=I32)[:, None]
    r = jnp.arange(ATT_KEYS, dtype=I32)[None, :]
    negdist = -jnp.abs(WINDOW + i - r).astype(F32)
    return negdist, i, r


def _layer(x, n_groups, rows, swa_inputs, ret_state, ret_len, w, tm, n_sc):
    (norm_mix, w_in, b_gate, sinks, wa, wb, wo, norm_ffn, wq_t, k1, k2, table, norm_final) = w
    qa, ka, va, qb, kb, vb, rg, ga, gb = _project(x, norm_mix, w_in, b_gate, tm)
    oa, k_rows, v_rows = swa_inputs(qa, ka, va, sinks)
    ob, s_fin = _retention(qb, kb, vb, rg, ret_state, n_groups, rows // ret_len, ret_len)
    x1 = _merge(x, oa, ob, ga, gb, wa, wb, wo, tm)
    idx_t, gate_t, h2 = _route(x1, norm_ffn, wq_t, k1, k2, tm)
    y = _experts(x1, norm_ffn, norm_final, idx_t.T.reshape(-1), gate_t, table[:, None, :], n_sc)
    if n_sc:
        y_sc = _sc_experts(table, idx_t[:, :n_sc].T, gate_t[:, :n_sc].T, h2)
        y = _finish(x1, y_sc, norm_final, y, tm)
    return y, k_rows, v_rows, s_fin


def kernel(x_prompt, x_sample, cache_swa_k, cache_swa_v, state_ret, norm_mix, w_in, b_gate, attn_sinks,
           w_branch_a, w_branch_b, w_out, norm_ffn, peer_w_q, peer_sub_k1, peer_sub_k2, peer_u, peer_v,
           norm_final):
    batch, seq, _ = x_prompt.shape
    dec_batch, dec_seq, _ = x_sample.shape
    assert norm_mix.shape[0] == 1, "single-layer trunk"
    table = jnp.concatenate([peer_u[0], peer_v[0]], axis=1)
    w = (norm_mix[0][None], w_in[0].astype(BF16), b_gate[0][None], attn_sinks[0],
         w_branch_a[0].astype(BF16), w_branch_b[0].astype(BF16), w_out[0].astype(BF16),
         norm_ffn[0][None], peer_w_q[0].T.astype(BF16), peer_sub_k1[0].astype(BF16),
         peer_sub_k2[0].astype(BF16), table, norm_final[None])

    def swa_prompt(qa, ka, va, sinks):
        tq = 2 * CHUNK
        negdist, i, r = _attention_masks(tq)
        lq, lk = i // CHUNK, r // CHUNK
        band = (lk >= lq) & (lk <= lq + 2)
        allow = jnp.stack([band & (r >= WINDOW), band]).astype(F32)
        oa = _swa(sinks, qa, ka, va, negdist, allow, batch, seq // tq, tq)
        tail = lambda a: a.reshape(batch, seq, N_KV_A, HEAD_DIM_A)[:, seq - WINDOW:]
        return oa, tail(ka), tail(va)

    def swa_sample(qa, ka, va, sinks):
        negdist, i, r = _attention_masks(dec_seq)
        visible = jnp.broadcast_to(r < WINDOW + dec_seq, (dec_seq, ATT_KEYS))
        allow = jnp.stack([visible, visible]).astype(F32)
        pad = jnp.zeros((dec_batch, ATT_KEYS - WINDOW - dec_seq, KV_A), F32)
        k_all = jnp.concatenate([cache_swa_k[0].reshape(dec_batch, WINDOW, KV_A),
                                 ka.reshape(dec_batch, dec_seq, KV_A), pad], axis=1)
        v_all = jnp.concatenate([cache_swa_v[0].reshape(dec_batch, WINDOW, KV_A),
                                 va.reshape(dec_batch, dec_seq, KV_A), pad], axis=1)
        oa = _swa(sinks, qa, k_all.reshape(-1, KV_A), v_all.reshape(-1, KV_A), negdist, allow,
                  dec_batch, 1, dec_seq)
        tail = lambda a: a[:, dec_seq:WINDOW + dec_seq].reshape(dec_batch, WINDOW, N_KV_A, HEAD_DIM_A)
        return oa, tail(k_all), tail(v_all)

    ys, ks, vs, ss = _layer(x_sample.reshape(dec_batch * dec_seq, D_MODEL), dec_batch, dec_seq, swa_sample,
                            state_ret[0], dec_seq, w, 128, 0)

    s0 = jnp.zeros((batch, N_HEADS_B, DK_B, DV_B), F32)
    yp, kp, vp, sp = _layer(x_prompt.reshape(batch * seq, D_MODEL), batch, seq, swa_prompt, s0, 256, w, 256,
                            SC_PROMPT_TOKENS)

    return (yp.reshape(batch, seq, D_MODEL), ys.reshape(dec_batch, dec_seq, D_MODEL),
            kp[None], vp[None], sp[None], ks[None], vs[None], ss[None])
```

```python
import functools

import jax
import jax.numpy as jnp
from jax import lax
from jax.experimental import pallas as pl
from jax.experimental.pallas import tpu as pltpu
from jax.experimental.pallas import tpu_sc as plsc

F32 = jnp.float32
BF16 = jnp.bfloat16
I32 = jnp.int32

D_MODEL = 1024
CHUNK = 64
EPS = 1e-6
NEG_INF = -1e30
PAST_LEN = 2048

N_HEADS_A = 8
N_KV_A = 2
GROUP_A = N_HEADS_A // N_KV_A
HEAD_DIM_A = 64
WINDOW = 128
N_HEADS_B = 4
DK_B = 128
DV_B = 256
Q_A = N_HEADS_A * HEAD_DIM_A
KV_A = N_KV_A * HEAD_DIM_A
Q_B = N_HEADS_B * DK_B
V_B = N_HEADS_B * DV_B
D_IN = Q_A + 2 * KV_A + 2 * Q_B + 2 * V_B + 2 * D_MODEL
OFF_QA = 0
OFF_KA = OFF_QA + Q_A
OFF_VA = OFF_KA + KV_A
OFF_QB = OFF_VA + KV_A
OFF_KB = OFF_QB + Q_B
OFF_VB = OFF_KB + Q_B
OFF_RG = OFF_VB + V_B
OFF_GA = OFF_RG + V_B
OFF_GB = OFF_GA + D_MODEL

N_KEYS = 128
N_EXPERTS = N_KEYS * N_KEYS
PEER_HEADS = 8
PEER_QDIM = 256
PEER_HALF = PEER_QDIM // 2
PEER_TOPK = 16
PICKS = PEER_HEADS * PEER_TOPK

LANES = 128
ATT_KEYS = 2 * WINDOW
VMEM_LIMIT = 56 * 1024 * 1024


def _rmsnorm(x, g):
    return x * lax.rsqrt(jnp.mean(x * x, axis=-1, keepdims=True) + EPS) * g


def _dot(a, b):
    return jnp.dot(a, b, preferred_element_type=F32)


def _dot_nt(a, b):
    return lax.dot_general(a, b, (((1,), (1,)), ((), ())), preferred_element_type=F32)


def _dot_tn(a, b):
    return lax.dot_general(a, b, (((0,), (0,)), ((), ())), preferred_element_type=F32)


def _params(sem):
    return pltpu.CompilerParams(dimension_semantics=sem, vmem_limit_bytes=VMEM_LIMIT)


def _proj_body(x_ref, g_ref, w_ref, bg_ref, qa_ref, ka_ref, va_ref, qb_ref, kb_ref, vb_ref,
               rg_ref, ga_ref, gb_ref):
    h = _rmsnorm(x_ref[...], g_ref[...]).astype(BF16)

    def mm(lo, width):
        return _dot(h, w_ref[:, lo:lo + width])

    qa_ref[...] = (mm(OFF_QA, Q_A) * (HEAD_DIM_A ** -0.5)).astype(BF16)
    ka_ref[...] = mm(OFF_KA, KV_A)
    va_ref[...] = mm(OFF_VA, KV_A)
    qb_ref[...] = mm(OFF_QB, Q_B).astype(BF16)
    kb_ref[...] = (mm(OFF_KB, Q_B) * (DK_B ** -0.5)).astype(BF16)
    vb_ref[...] = mm(OFF_VB, V_B).astype(BF16)
    rg_ref[...] = mm(OFF_RG, V_B)
    ga_ref[...] = mm(OFF_GA, D_MODEL) + bg_ref[:, :D_MODEL]
    gb_ref[...] = mm(OFF_GB, D_MODEL) + bg_ref[:, D_MODEL:]


def _project(x, g, w_bf16, b_gate, tm):
    n = x.shape[0]
    row = lambda width: pl.BlockSpec((tm, width), lambda i: (i, 0))
    full = lambda a: pl.BlockSpec(a.shape, lambda i: (0, 0))
    widths = (Q_A, KV_A, KV_A, Q_B, Q_B, V_B, V_B, D_MODEL, D_MODEL)
    dtypes = (BF16, F32, F32, BF16, BF16, BF16, F32, F32, F32)
    return pl.pallas_call(
        _proj_body,
        grid=(n // tm,),
        in_specs=[row(D_MODEL), full(g), full(w_bf16), full(b_gate)],
        out_specs=[row(w) for w in widths],
        out_shape=[jax.ShapeDtypeStruct((n, w), d) for w, d in zip(widths, dtypes)],
        compiler_params=_params(("parallel",)),
        name="projection",
    )(x, g, w_bf16, b_gate)


def _swa_body(sink_ref, q_ref, kp_ref, kc_ref, vp_ref, vc_ref, nd_ref, al_ref, o_ref):
    k = jnp.concatenate([kp_ref[...], kc_ref[...]], axis=0)
    v = jnp.concatenate([vp_ref[...], vc_ref[...]], axis=0)
    lane = lax.broadcasted_iota(I32, k.shape, 1)
    low = lane < HEAD_DIM_A
    k_sw = pltpu.roll(k, HEAD_DIM_A, axis=1)
    v_sw = pltpu.roll(v, HEAD_DIM_A, axis=1)
    negdist = nd_ref[...]
    allowed = al_ref[0] > 0.5
    for kv in range(N_KV_A):
        own = low if kv == 0 else jnp.logical_not(low)
        k_rep = jnp.where(own, k, k_sw)
        v_rep = jnp.where(own, v, v_sw)
        zero = jnp.zeros_like(k_rep)
        kk = jnp.concatenate([jnp.where(low, k_rep, zero), jnp.where(low, zero, k_rep)], axis=0).astype(BF16)
        vv = jnp.concatenate([jnp.where(low, v_rep, zero), jnp.where(low, zero, v_rep)], axis=0).astype(BF16)
        for a in range(GROUP_A // 2):
            col = kv * GROUP_A * HEAD_DIM_A + a * LANES
            s2 = _dot_nt(q_ref[:, col:col + LANES], kk)
            ps = []
            for u in range(2):
                head = kv * GROUP_A + 2 * a + u
                slope = 2.0 ** (-8.0 * (head + 1) / N_HEADS_A)
                sink = sink_ref[head]
                s = s2[:, u * ATT_KEYS:(u + 1) * ATT_KEYS] + slope * negdist
                s = jnp.where(allowed, s, NEG_INF)
                m = jnp.maximum(jnp.max(s, axis=-1, keepdims=True), sink)
                p = jnp.exp(s - m)
                denom = jnp.sum(p, axis=-1, keepdims=True) + jnp.exp(sink - m)
                ps.append((p / denom).astype(BF16))
            o = _dot(jnp.concatenate(ps, axis=1), vv)
            o_ref[:, col:col + LANES] = o.astype(BF16)


def _swa(sinks, q, k, v, negdist, allow, groups, q_blocks, tq):
    n = q.shape[0]
    kblocks = k.shape[0] // LANES // groups
    shift = kblocks - q_blocks
    prev = lambda b, j: (b * kblocks + jnp.maximum(j + shift - 1, 0), 0)
    cur = lambda b, j: (b * kblocks + j + shift, 0)
    kv_spec = lambda f: pl.BlockSpec((LANES, KV_A), f)
    return pl.pallas_call(
        _swa_body,
        grid=(groups, q_blocks),
        in_specs=[pl.BlockSpec(memory_space=pltpu.SMEM),
                  pl.BlockSpec((tq, Q_A), lambda b, j: (b * q_blocks + j, 0)),
                  kv_spec(prev), kv_spec(cur), kv_spec(prev), kv_spec(cur),
                  pl.BlockSpec((tq, ATT_KEYS), lambda b, j: (0, 0)),
                  pl.BlockSpec((1, tq, ATT_KEYS), lambda b, j: (jnp.minimum(j + shift, 1), 0, 0))],
        out_specs=pl.BlockSpec((tq, Q_A), lambda b, j: (b * q_blocks + j, 0)),
        out_shape=jax.ShapeDtypeStruct((n, Q_A), BF16),
        compiler_params=_params(("parallel", "parallel")),
        name="attention",
    )(sinks, q, k, k, v, v, negdist, allow)


def _ret_body(q_ref, k_ref, v_ref, rg_ref, s0_ref, intra_ref, rdec_ref, kdec_ref, sdec_ref,
              o_ref, st_ref):
    c = pl.program_id(1)

    @pl.when(c == 0)
    def _():
        st_ref[...] = s0_ref[...]

    for h in range(N_HEADS_B):
        q = q_ref[:, h * DK_B:(h + 1) * DK_B]
        k = k_ref[:, h * DK_B:(h + 1) * DK_B]
        v = v_ref[:, h * DV_B:(h + 1) * DV_B]
        st = st_ref[0, h]
        scores = (_dot_nt(q, k) * intra_ref[h]).astype(BF16)
        o = _dot(scores, v) + _dot(q, st.astype(BF16)) * rdec_ref[h]
        k_dec = (k.astype(F32) * kdec_ref[h]).astype(BF16)
        st_ref[0, h] = sdec_ref[h] * st + _dot_tn(k_dec, v)
        mu = jnp.mean(o, axis=-1, keepdims=True)
        d = o - mu
        var = jnp.mean(d * d, axis=-1, keepdims=True)
        on = d * lax.rsqrt(var + EPS)
        r = rg_ref[:, h * DV_B:(h + 1) * DV_B]
        o_ref[:, h * DV_B:(h + 1) * DV_B] = (on * (r * jax.nn.sigmoid(r))).astype(BF16)


def _retention_consts(length):
    lg = jnp.log1p(-(2.0 ** (-5.0 - jnp.arange(N_HEADS_B, dtype=F32))))
    n = jnp.arange(length, dtype=F32)
    diff = n[:, None] - n[None, :]
    intra = jnp.where(diff >= 0, jnp.exp(jnp.maximum(diff, 0.0) * lg[:, None, None]), 0.0)
    rdec = jnp.exp((n[None, :] + 1.0) * lg[:, None])[..., None]
    kdec = jnp.exp((length - 1.0 - n)[None, :] * lg[:, None])[..., None]
    sdec = jnp.exp(length * lg)
    return intra, rdec, kdec, sdec


def _retention(q, k, v, rg, s0, streams, chunks, length):
    n = q.shape[0]
    intra, rdec, kdec, sdec = _retention_consts(length)
    row = lambda width: pl.BlockSpec((length, width), lambda b, c: (b * chunks + c, 0))
    const = lambda a: pl.BlockSpec(a.shape, lambda b, c: (0,) * a.ndim)
    st_spec = pl.BlockSpec((1, N_HEADS_B, DK_B, DV_B), lambda b, c: (b, 0, 0, 0))
    return pl.pallas_call(
        _ret_body,
        grid=(streams, chunks),
        in_specs=[row(Q_B), row(Q_B), row(V_B), row(V_B), st_spec,
                  const(intra), const(rdec), const(kdec), pl.BlockSpec(memory_space=pltpu.SMEM)],
        out_specs=[row(V_B), st_spec],
        out_shape=[jax.ShapeDtypeStruct((n, V_B), BF16),
                   jax.ShapeDtypeStruct((streams, N_HEADS_B, DK_B, DV_B), F32)],
        compiler_params=_params(("parallel", "arbitrary")),
        name="retention",
    )(q, k, v, rg, s0, intra, rdec, kdec, sdec)


def _merge_body(x_ref, oa_ref, ob_ref, ga_ref, gb_ref, wa_ref, wb_ref, wo_ref, o_ref):
    ya = _dot(oa_ref[...], wa_ref[...])
    yb = _dot(ob_ref[...], wb_ref[...])
    mix = jax.nn.sigmoid(ga_ref[...]) * ya + jax.nn.sigmoid(gb_ref[...]) * yb
    o_ref[...] = x_ref[...] + _dot(mix.astype(BF16), wo_ref[...])


def _merge(x, oa, ob, ga, gb, wa, wb, wo, tm):
    n = x.shape[0]
    row = lambda width: pl.BlockSpec((tm, width), lambda i: (i, 0))
    full = lambda a: pl.BlockSpec(a.shape, lambda i: (0, 0))
    return pl.pallas_call(
        _merge_body,
        grid=(n // tm,),
        in_specs=[row(D_MODEL), row(Q_A), row(V_B), row(D_MODEL), row(D_MODEL), full(wa), full(wb), full(wo)],
        out_specs=row(D_MODEL),
        out_shape=jax.ShapeDtypeStruct((n, D_MODEL), F32),
        compiler_params=_params(("parallel",)),
        name="merge",
    )(x, oa, ob, ga, gb, wa, wb, wo)


def _top16(s, order):
    t = s.shape[1]
    rank = lax.broadcasted_iota(I32, (PEER_TOPK, t), 0)
    vals = jnp.zeros((PEER_TOPK, t), F32)
    ids = jnp.zeros((PEER_TOPK, t), F32)
    for r in range(PEER_TOPK):
        m = jnp.max(s, axis=0, keepdims=True)
        pick = jnp.min(jnp.where(s == m, order, 3e38), axis=0, keepdims=True)
        s = jnp.where(order == pick, -jnp.inf, s)
        vals = jnp.where(rank == r, m, vals)
        ids = jnp.where(rank == r, pick, ids)
    return vals, ids


def _route_body(x_ref, g_ref, wq_ref, k1_ref, k2_ref, idx_ref, gate_ref, h_ref):
    h_ref[...] = _rmsnorm(x_ref[...], g_ref[...])
    h = h_ref[...].astype(BF16)
    t = h.shape[0]
    q_t = _dot_nt(wq_ref[...], h).astype(BF16)
    iota = lambda rows: lax.broadcasted_iota(I32, (rows, t), 0).astype(F32)
    key_id = iota(N_KEYS)
    flat = jnp.concatenate(
        [iota(PEER_TOPK)] + [a * PEER_TOPK + iota(8) for a in range(1, 8)] + [(8 + iota(8)) * PEER_TOPK],
        axis=0)
    rank = lax.broadcasted_iota(I32, (PEER_TOPK, t), 0)
    idx_rows, gate_rows = [], []
    for hd in range(PEER_HEADS):
        base = hd * PEER_QDIM
        s1 = _dot(k1_ref[...], q_t[base:base + PEER_HALF])
        s2 = _dot(k2_ref[...], q_t[base + PEER_HALF:base + PEER_QDIM])
        v1, i1 = _top16(s1, key_id)
        v2, i2 = _top16(s2, key_id)
        cand = jnp.concatenate(
            [v1[0:1] + v2]
            + [v1[a:a + 1] + v2[0:8] for a in range(1, 8)]
            + [v1[8:16] + v2[0:1]], axis=0)
        cidx = jnp.concatenate(
            [i1[0:1] * N_KEYS + i2]
            + [i1[a:a + 1] * N_KEYS + i2[0:8] for a in range(1, 8)]
            + [i1[8:16] * N_KEYS + i2[0:1]], axis=0)
        best, pos = _top16(cand, flat)
        eidx = jnp.zeros((PEER_TOPK, t), F32)
        for r in range(PEER_TOPK):
            picked = jnp.max(jnp.where(flat == pos[r:r + 1], cidx, -1.0), axis=0, keepdims=True)
            eidx = jnp.where(rank == r, picked, eidx)
        e = jnp.exp(best - best[0:1])
        gate_rows.append(e / jnp.sum(e, axis=0, keepdims=True))
        idx_rows.append(eidx.astype(I32))
    idx_ref[...] = jnp.concatenate(idx_rows, axis=0)
    gate_ref[...] = jnp.concatenate(gate_rows, axis=0)


def _route(x, g, wq_t, k1, k2, tm):
    n = x.shape[0]
    full = lambda a: pl.BlockSpec(a.shape, lambda i: (0, 0))
    col = pl.BlockSpec((PICKS, tm), lambda i: (0, i))
    return pl.pallas_call(
        _route_body,
        grid=(n // tm,),
        in_specs=[pl.BlockSpec((tm, D_MODEL), lambda i: (i, 0)), full(g), full(wq_t), full(k1), full(k2)],
        out_specs=[col, col, pl.BlockSpec((tm, D_MODEL), lambda i: (i, 0))],
        out_shape=[jax.ShapeDtypeStruct((PICKS, n), I32), jax.ShapeDtypeStruct((PICKS, n), F32),
                   jax.ShapeDtypeStruct((n, D_MODEL), F32)],
        compiler_params=_params(("parallel",)),
        name="routing",
    )(x, g, wq_t, k1, k2)


SC_CORES = 2
SC_SUBCORES = 16
SC_LANES = 16
SC_CHUNK = 16
SC_PROMPT_TOKENS = 15872


def _gelu_via_exp(x):
    z = 0.7978845608028654 * (x + 0.044715 * x * x * x)
    return 0.5 * x * (2.0 - 2.0 / (1.0 + jnp.exp(2.0 * z)))


def _sc_experts(table, idx, gate, h):
    n = idx.shape[0]
    workers = SC_CORES * SC_SUBCORES
    per_worker = n // workers
    n_chunks = PICKS // SC_CHUNK
    steps = D_MODEL // SC_LANES
    mesh = plsc.VectorSubcoreMesh(core_axis_name="c", subcore_axis_name="s",
                                  num_cores=SC_CORES, num_subcores=SC_SUBCORES)

    def body(tab_hbm, idx_hbm, gate_hbm, h_hbm, y_hbm,
             idx_0, idx_1, gate_0, gate_1, h_0, h_1, y_0, y_1, rows_a, rows_b,
             in_sem_0, in_sem_1, y_sem_0, y_sem_1, sem_a, sem_b):
        base = (lax.axis_index("s") * SC_CORES + lax.axis_index("c")) * per_worker
        last = base + per_worker - 1
        sets = ((idx_0, gate_0, h_0, y_0, in_sem_0, y_sem_0), (idx_1, gate_1, h_1, y_1, in_sem_1, y_sem_1))
        bufs = ((rows_a, sem_a), (rows_b, sem_b))

        def in_copies(t, q):
            idx_v, gate_v, h_v, _, sem, _ = sets[q]
            return (pltpu.make_async_copy(idx_hbm.at[t], idx_v, sem),
                    pltpu.make_async_copy(gate_hbm.at[t], gate_v, sem),
                    pltpu.make_async_copy(h_hbm.at[t], h_v, sem))

        def y_copy(t, q):
            return pltpu.make_async_copy(sets[q][3], y_hbm.at[t], sets[q][5])

        def gather(c, b, q):
            rows, sem = bufs[b]
            return pltpu.make_async_copy(tab_hbm.at[sets[q][0].at[pl.ds(c * SC_CHUNK, SC_CHUNK)]], rows, sem)

        def compute(c, b, q):
            rows, _ = bufs[b]
            _, gate_v, h_v, y_v, _, _ = sets[q]

            def dot_step(j, accs):
                hj = h_v[pl.ds(j * SC_LANES, SC_LANES)]
                return tuple(accs[k] + rows[k, 0, pl.ds(j * SC_LANES, SC_LANES)] * hj for k in range(SC_CHUNK))

            accs = lax.fori_loop(0, steps, dot_step,
                                 tuple(jnp.zeros((SC_LANES,), F32) for _ in range(SC_CHUNK)))
            ws = []
            for k in range(SC_CHUNK):
                s = jnp.full((SC_LANES,), jnp.sum(accs[k]), F32)
                g = plsc.load_gather(gate_v, [jnp.full((SC_LANES,), c * SC_CHUNK + k, I32)])
                ws.append(_gelu_via_exp(s) * g)

            def sum_step(j, carry):
                acc = y_v[pl.ds(j * SC_LANES, SC_LANES)]
                for k in range(SC_CHUNK):
                    acc = acc + ws[k] * rows[k, 0, pl.ds(D_MODEL + j * SC_LANES, SC_LANES)]
                y_v[pl.ds(j * SC_LANES, SC_LANES)] = acc
                return carry

            lax.fori_loop(0, steps, sum_step, 0)

        def token(t, q, first):
            nxt = jnp.minimum(t + 1, last)
            for cp in in_copies(nxt, 1 - q):
                cp.start()
            y_v = sets[q][3]

            @pl.when(jnp.logical_not(first))
            def _():
                y_copy(t, q).wait()

            @pl.loop(0, steps)
            def _(j):
                y_v[pl.ds(j * SC_LANES, SC_LANES)] = jnp.zeros((SC_LANES,), F32)

            @pl.loop(0, n_chunks // 2)
            def _(p):
                c0 = p * 2
                gather(c0 + 1, 1, q).start()
                gather(c0, 0, q).wait()
                compute(c0, 0, q)

                @pl.when(p + 1 < n_chunks // 2)
                def _():
                    gather(c0 + 2, 0, q).start()

                gather(c0 + 1, 1, q).wait()
                compute(c0 + 1, 1, q)

            for cp in in_copies(nxt, 1 - q):
                cp.wait()
            gather(0, 0, 1 - q).start()
            y_copy(t, q).start()

        for cp in in_copies(base, 0):
            cp.start()
        for cp in in_copies(base, 0):
            cp.wait()
        gather(0, 0, 0).start()

        @pl.loop(0, per_worker // 2)
        def _(i):
            token(base + 2 * i, 0, i == 0)
            token(base + 2 * i + 1, 1, i == 0)

        gather(0, 0, 0).wait()
        y_copy(last - 1, 0).wait()
        y_copy(last, 1).wait()

    vec = lambda n_el, dt: [pltpu.VMEM((n_el,), dt), pltpu.VMEM((n_el,), dt)]
    return pl.kernel(
        body, out_type=jax.ShapeDtypeStruct((n, D_MODEL), F32), mesh=mesh,
        scratch_types=vec(PICKS, I32) + vec(PICKS, F32) + vec(D_MODEL, F32) + vec(D_MODEL, F32)
        + [pltpu.VMEM((SC_CHUNK, 1, 2 * D_MODEL), F32), pltpu.VMEM((SC_CHUNK, 1, 2 * D_MODEL), F32)]
        + [pltpu.SemaphoreType.DMA] * 6,
        compiler_params=pltpu.CompilerParams(needs_layout_passes=False),
        name="experts_sc",
    )(table, idx, gate, h)


def _finish_body(x_ref, y_ref, gf_ref, rest_ref, o_ref):
    del rest_ref
    o_ref[...] = _rmsnorm(x_ref[...] + y_ref[...], gf_ref[...])


def _finish(x, y, g_final, rest, tm):
    row = pl.BlockSpec((tm, D_MODEL), lambda i: (i, 0))
    return pl.pallas_call(
        _finish_body,
        grid=(y.shape[0] // tm,),
        in_specs=[row, row, pl.BlockSpec(g_final.shape, lambda i: (0, 0)), pl.BlockSpec(memory_space=pl.ANY)],
        out_specs=row,
        out_shape=jax.ShapeDtypeStruct(rest.shape, F32),
        input_output_aliases={3: 0},
        compiler_params=_params(("parallel",)),
        name="finish",
    )(x, y, g_final, rest)


EXP_TOKENS = 256
EXP_SLOTS = 4


def _expert_body(x_ref, g_ref, gf_ref, idx_hbm, gate_ref, tab_hbm, o_ref,
                 idx_smem, buf0, buf1, buf2, buf3, h_buf, y_buf, sems, idx_sem, *, first_block):
    bufs = (buf0, buf1, buf2, buf3)
    ahead = EXP_SLOTS - 1
    i = pl.program_id(0) + first_block
    idx_copy = pltpu.make_async_copy(idx_hbm.at[pl.ds(i * (EXP_TOKENS * PICKS), EXP_TOKENS * PICKS)],
                                     idx_smem, idx_sem)
    idx_copy.start()
    h_buf[...] = _rmsnorm(x_ref[...], g_ref[...])
    idx_copy.wait()

    def issue(t, slot):
        for k in range(PICKS):
            e = idx_smem[t * PICKS + k]
            pltpu.async_copy(tab_hbm.at[e], bufs[slot].at[pl.ds(k, 1), :], sems.at[slot], priority=k % 2)

    def wait_slot(slot):
        pltpu.make_async_copy(bufs[(slot + 1) % EXP_SLOTS], bufs[slot], sems.at[slot]).wait()

    lane = lax.broadcasted_iota(I32, (PICKS, EXP_TOKENS), 1)

    def compute(t, slot):
        h = h_buf[pl.ds(t, 1), :]
        u = bufs[slot][:, :D_MODEL]
        s = jnp.sum(u * h, axis=-1, keepdims=True)
        gate = jnp.sum(jnp.where(lane == t, gate_ref[...], 0.0), axis=-1, keepdims=True)
        w = gate * jax.nn.gelu(s)
        v = bufs[slot][:, D_MODEL:]
        y_buf[pl.ds(t, 1), :] = jnp.sum(w * v, axis=0, keepdims=True)

    def group(t0, issue_upto):
        for s in range(EXP_SLOTS):
            wait_slot(s)
            if s < issue_upto:
                issue(t0 + s + ahead, (s + ahead) % EXP_SLOTS)
            compute(t0 + s, s)

    for s in range(ahead):
        issue(s, s)

    def steady(q, carry):
        group(q * EXP_SLOTS, EXP_SLOTS)
        return carry

    n_groups = EXP_TOKENS // EXP_SLOTS
    lax.fori_loop(0, n_groups - 1, steady, 0)
    group((n_groups - 1) * EXP_SLOTS, EXP_SLOTS - ahead)
    o_ref[...] = _rmsnorm(x_ref[...] + y_buf[...], gf_ref[...])


def _experts(x, g, g_final, idx_flat, gate_t, table, first_token):
    n = x.shape[0]
    first_block = first_token // EXP_TOKENS
    full = lambda a: pl.BlockSpec(a.shape, lambda i: (0, 0))
    row = pl.BlockSpec((EXP_TOKENS, D_MODEL), lambda i: (i + first_block, 0))
    return pl.pallas_call(
        functools.partial(_expert_body, first_block=first_block),
        grid=(n // EXP_TOKENS - first_block,),
        in_specs=[row, full(g), full(g_final),
                  pl.BlockSpec(memory_space=pl.ANY),
                  pl.BlockSpec((PICKS, EXP_TOKENS), lambda i: (0, i + first_block)),
                  pl.BlockSpec(memory_space=pl.ANY)],
        out_specs=row,
        out_shape=jax.ShapeDtypeStruct((n, D_MODEL), F32),
        scratch_shapes=[pltpu.SMEM((EXP_TOKENS * PICKS,), I32)]
        + [pltpu.VMEM((PICKS, 2 * D_MODEL), F32) for _ in range(EXP_SLOTS)]
        + [pltpu.VMEM((EXP_TOKENS, D_MODEL), F32),
           pltpu.VMEM((EXP_TOKENS, D_MODEL), F32),
           pltpu.SemaphoreType.DMA((EXP_SLOTS,)),
           pltpu.SemaphoreType.DMA(())],
        compiler_params=pltpu.CompilerParams(dimension_semantics=("arbitrary",), vmem_limit_bytes=VMEM_LIMIT,
                                             disable_bounds_checks=True),
        name="experts",
    )(x, g, g_final, idx_flat, gate_t, table)


def _attention_masks(tq):
    i = jnp.arange(tq, dtype=I32)[:, None]
    r = jnp.arange(ATT_KEYS, dtype=I32)[None, :]
    negdist = -jnp.abs(WINDOW + i - r).astype(F32)
    return negdist, i, r


def _layer(x, n_groups, rows, swa_inputs, ret_state, ret_len, w, tm, n_sc):
    (norm_mix, w_in, b_gate, sinks, wa, wb, wo, norm_ffn, wq_t, k1, k2, table, norm_final) = w
    qa, ka, va, qb, kb, vb, rg, ga, gb = _project(x, norm_mix, w_in, b_gate, tm)
    oa, k_rows, v_rows = swa_inputs(qa, ka, va, sinks)
    ob, s_fin = _retention(qb, kb, vb, rg, ret_state, n_groups, rows // ret_len, ret_len)
    x1 = _merge(x, oa, ob, ga, gb, wa, wb, wo, tm)
    idx_t, gate_t, h2 = _route(x1, norm_ffn, wq_t, k1, k2, tm)
    y = _experts(x1, norm_ffn, norm_final, idx_t.T.reshape(-1), gate_t, table, n_sc)
    if n_sc:
        y_sc = _sc_experts(table, idx_t[:, :n_sc].T, gate_t[:, :n_sc].T, h2)
        y = _finish(x1, y_sc, norm_final, y, tm)
    return y, k_rows, v_rows, s_fin


def kernel(x_prompt, x_sample, cache_swa_k, cache_swa_v, state_ret, norm_mix, w_in, b_gate, attn_sinks,
           w_branch_a, w_branch_b, w_out, norm_ffn, peer_w_q, peer_sub_k1, peer_sub_k2, peer_u, peer_v,
           norm_final):
    batch, seq, _ = x_prompt.shape
    dec_batch, dec_seq, _ = x_sample.shape
    assert norm_mix.shape[0] == 1, "single-layer trunk"
    table = jnp.concatenate([peer_u[0], peer_v[0]], axis=1)[:, None, :]
    w = (norm_mix[0][None], w_in[0].astype(BF16), b_gate[0][None], attn_sinks[0],
         w_branch_a[0].astype(BF16), w_branch_b[0].astype(BF16), w_out[0].astype(BF16),
         norm_ffn[0][None], peer_w_q[0].T.astype(BF16), peer_sub_k1[0].astype(BF16),
         peer_sub_k2[0].astype(BF16), table, norm_final[None])

    def swa_prompt(qa, ka, va, sinks):
        tq = 2 * CHUNK
        negdist, i, r = _attention_masks(tq)
        lq, lk = i // CHUNK, r // CHUNK
        band = (lk >= lq) & (lk <= lq + 2)
        allow = jnp.stack([band & (r >= WINDOW), band]).astype(F32)
        oa = _swa(sinks, qa, ka, va, negdist, allow, batch, seq // tq, tq)
        tail = lambda a: a.reshape(batch, seq, N_KV_A, HEAD_DIM_A)[:, seq - WINDOW:]
        return oa, tail(ka), tail(va)

    def swa_sample(qa, ka, va, sinks):
        negdist, i, r = _attention_masks(dec_seq)
        visible = jnp.broadcast_to(r < WINDOW + dec_seq, (dec_seq, ATT_KEYS))
        allow = jnp.stack([visible, visible]).astype(F32)
        pad = jnp.zeros((dec_batch, ATT_KEYS - WINDOW - dec_seq, KV_A), F32)
        k_all = jnp.concatenate([cache_swa_k[0].reshape(dec_batch, WINDOW, KV_A),
                                 ka.reshape(dec_batch, dec_seq, KV_A), pad], axis=1)
        v_all = jnp.concatenate([cache_swa_v[0].reshape(dec_batch, WINDOW, KV_A),
                                 va.reshape(dec_batch, dec_seq, KV_A), pad], axis=1)
        oa = _swa(sinks, qa, k_all.reshape(-1, KV_A), v_all.reshape(-1, KV_A), negdist, allow,
                  dec_batch, 1, dec_seq)
        tail = lambda a: a[:, dec_seq:WINDOW + dec_seq].reshape(dec_batch, WINDOW, N_KV_A, HEAD_DIM_A)
        return oa, tail(k_all), tail(v_all)

    ys, ks, vs, ss = _layer(x_sample.reshape(dec_batch * dec_seq, D_MODEL), dec_batch, dec_seq, swa_sample,
                            state_ret[0], dec_seq, w, 128, 0)

    s0 = jnp.zeros((batch, N_HEADS_B, DK_B, DV_B), F32)
    yp, kp, vp, sp = _layer(x_prompt.reshape(batch * seq, D_MODEL), batch, seq, swa_prompt, s0, 256, w, 256,
                            SC_PROMPT_TOKENS)

    return (yp.reshape(batch, seq, D_MODEL), ys.reshape(dec_batch, dec_seq, D_MODEL),
            kp[None], vp[None], sp[None], ks[None], vs[None], ss[None])
```

```python
import functools

import jax
import jax.numpy as jnp
from jax import lax
from jax.experimental import pallas as pl
from jax.experimental.pallas import tpu as pltpu
from jax.experimental.pallas import tpu_sc as plsc

F32 = jnp.float32
BF16 = jnp.bfloat16
I32 = jnp.int32

D_MODEL = 1024
CHUNK = 64
EPS = 1e-6
NEG_INF = -1e30
PAST_LEN = 2048

N_HEADS_A = 8
N_KV_A = 2
GROUP_A = N_HEADS_A // N_KV_A
HEAD_DIM_A = 64
WINDOW = 128
N_HEADS_B = 4
DK_B = 128
DV_B = 256
Q_A = N_HEADS_A * HEAD_DIM_A
KV_A = N_KV_A * HEAD_DIM_A
Q_B = N_HEADS_B * DK_B
V_B = N_HEADS_B * DV_B
D_IN = Q_A + 2 * KV_A + 2 * Q_B + 2 * V_B + 2 * D_MODEL
OFF_QA = 0
OFF_KA = OFF_QA + Q_A
OFF_VA = OFF_KA + KV_A
OFF_QB = OFF_VA + KV_A
OFF_KB = OFF_QB + Q_B
OFF_VB = OFF_KB + Q_B
OFF_RG = OFF_VB + V_B
OFF_GA = OFF_RG + V_B
OFF_GB = OFF_GA + D_MODEL

N_KEYS = 128
N_EXPERTS = N_KEYS * N_KEYS
PEER_HEADS = 8
PEER_QDIM = 256
PEER_HALF = PEER_QDIM // 2
PEER_TOPK = 16
PICKS = PEER_HEADS * PEER_TOPK

LANES = 128
ATT_KEYS = 2 * WINDOW
VMEM_LIMIT = 56 * 1024 * 1024


def _rmsnorm(x, g):
    return x * lax.rsqrt(jnp.mean(x * x, axis=-1, keepdims=True) + EPS) * g


def _dot(a, b):
    return jnp.dot(a, b, preferred_element_type=F32)


def _dot_nt(a, b):
    return lax.dot_general(a, b, (((1,), (1,)), ((), ())), preferred_element_type=F32)


def _dot_tn(a, b):
    return lax.dot_general(a, b, (((0,), (0,)), ((), ())), preferred_element_type=F32)


def _params(sem):
    return pltpu.CompilerParams(dimension_semantics=sem, vmem_limit_bytes=VMEM_LIMIT)


def _proj_body(x_ref, g_ref, w_ref, bg_ref, qa_ref, ka_ref, va_ref, qb_ref, kb_ref, vb_ref,
               rg_ref, ga_ref, gb_ref):
    h = _rmsnorm(x_ref[...], g_ref[...]).astype(BF16)

    def mm(lo, width):
        return _dot(h, w_ref[:, lo:lo + width])

    qa_ref[...] = (mm(OFF_QA, Q_A) * (HEAD_DIM_A ** -0.5)).astype(BF16)
    ka_ref[...] = mm(OFF_KA, KV_A)
    va_ref[...] = mm(OFF_VA, KV_A)
    qb_ref[...] = mm(OFF_QB, Q_B).astype(BF16)
    kb_ref[...] = (mm(OFF_KB, Q_B) * (DK_B ** -0.5)).astype(BF16)
    vb_ref[...] = mm(OFF_VB, V_B).astype(BF16)
    rg_ref[...] = mm(OFF_RG, V_B)
    ga_ref[...] = mm(OFF_GA, D_MODEL) + bg_ref[:, :D_MODEL]
    gb_ref[...] = mm(OFF_GB, D_MODEL) + bg_ref[:, D_MODEL:]


def _project(x, g, w_bf16, b_gate, tm):
    n = x.shape[0]
    row = lambda width: pl.BlockSpec((tm, width), lambda i: (i, 0))
    full = lambda a: pl.BlockSpec(a.shape, lambda i: (0, 0))
    widths = (Q_A, KV_A, KV_A, Q_B, Q_B, V_B, V_B, D_MODEL, D_MODEL)
    dtypes = (BF16, F32, F32, BF16, BF16, BF16, F32, F32, F32)
    return pl.pallas_call(
        _proj_body,
        grid=(n // tm,),
        in_specs=[row(D_MODEL), full(g), full(w_bf16), full(b_gate)],
        out_specs=[row(w) for w in widths],
        out_shape=[jax.ShapeDtypeStruct((n, w), d) for w, d in zip(widths, dtypes)],
        compiler_params=_params(("parallel",)),
        name="projection",
    )(x, g, w_bf16, b_gate)


def _swa_body(sink_ref, q_ref, kp_ref, kc_ref, vp_ref, vc_ref, nd_ref, al_ref, o_ref):
    k = jnp.concatenate([kp_ref[...], kc_ref[...]], axis=0)
    v = jnp.concatenate([vp_ref[...], vc_ref[...]], axis=0)
    lane = lax.broadcasted_iota(I32, k.shape, 1)
    low = lane < HEAD_DIM_A
    k_sw = pltpu.roll(k, HEAD_DIM_A, axis=1)
    v_sw = pltpu.roll(v, HEAD_DIM_A, axis=1)
    negdist = nd_ref[...]
    allowed = al_ref[0] > 0.5
    for kv in range(N_KV_A):
        own = low if kv == 0 else jnp.logical_not(low)
        k_rep = jnp.where(own, k, k_sw)
        v_rep = jnp.where(own, v, v_sw)
        zero = jnp.zeros_like(k_rep)
        kk = jnp.concatenate([jnp.where(low, k_rep, zero), jnp.where(low, zero, k_rep)], axis=0).astype(BF16)
        vv = jnp.concatenate([jnp.where(low, v_rep, zero), jnp.where(low, zero, v_rep)], axis=0).astype(BF16)
        for a in range(GROUP_A // 2):
            col = kv * GROUP_A * HEAD_DIM_A + a * LANES
            s2 = _dot_nt(q_ref[:, col:col + LANES], kk)
            ps = []
            for u in range(2):
                head = kv * GROUP_A + 2 * a + u
                slope = 2.0 ** (-8.0 * (head + 1) / N_HEADS_A)
                sink = sink_ref[head]
                s = s2[:, u * ATT_KEYS:(u + 1) * ATT_KEYS] + slope * negdist
                s = jnp.where(allowed, s, NEG_INF)
                m = jnp.maximum(jnp.max(s, axis=-1, keepdims=True), sink)
                p = jnp.exp(s - m)
                denom = jnp.sum(p, axis=-1, keepdims=True) + jnp.exp(sink - m)
                ps.append((p / denom).astype(BF16))
            o = _dot(jnp.concatenate(ps, axis=1), vv)
            o_ref[:, col:col + LANES] = o.astype(BF16)


def _swa(sinks, q, k, v, negdist, allow, groups, q_blocks, tq):
    n = q.shape[0]
    kblocks = k.shape[0] // LANES // groups
    shift = kblocks - q_blocks
    prev = lambda b, j: (b * kblocks + jnp.maximum(j + shift - 1, 0), 0)
    cur = lambda b, j: (b * kblocks + j + shift, 0)
    kv_spec = lambda f: pl.BlockSpec((LANES, KV_A), f)
    return pl.pallas_call(
        _swa_body,
        grid=(groups, q_blocks),
        in_specs=[pl.BlockSpec(memory_space=pltpu.SMEM),
                  pl.BlockSpec((tq, Q_A), lambda b, j: (b * q_blocks + j, 0)),
                  kv_spec(prev), kv_spec(cur), kv_spec(prev), kv_spec(cur),
                  pl.BlockSpec((tq, ATT_KEYS), lambda b, j: (0, 0)),
                  pl.BlockSpec((1, tq, ATT_KEYS), lambda b, j: (jnp.minimum(j + shift, 1), 0, 0))],
        out_specs=pl.BlockSpec((tq, Q_A), lambda b, j: (b * q_blocks + j, 0)),
        out_shape=jax.ShapeDtypeStruct((n, Q_A), BF16),
        compiler_params=_params(("parallel", "parallel")),
        name="attention",
    )(sinks, q, k, k, v, v, negdist, allow)


def _ret_body(q_ref, k_ref, v_ref, rg_ref, s0_ref, intra_ref, rdec_ref, kdec_ref, sdec_ref,
              o_ref, st_ref):
    c = pl.program_id(1)

    @pl.when(c == 0)
    def _():
        st_ref[...] = s0_ref[...]

    for h in range(N_HEADS_B):
        q = q_ref[:, h * DK_B:(h + 1) * DK_B]
        k = k_ref[:, h * DK_B:(h + 1) * DK_B]
        v = v_ref[:, h * DV_B:(h + 1) * DV_B]
        st = st_ref[0, h]
        scores = (_dot_nt(q, k) * intra_ref[h]).astype(BF16)
        o = _dot(scores, v) + _dot(q, st.astype(BF16)) * rdec_ref[h]
        k_dec = (k.astype(F32) * kdec_ref[h]).astype(BF16)
        st_ref[0, h] = sdec_ref[h] * st + _dot_tn(k_dec, v)
        mu = jnp.mean(o, axis=-1, keepdims=True)
        d = o - mu
        var = jnp.mean(d * d, axis=-1, keepdims=True)
        on = d * lax.rsqrt(var + EPS)
        r = rg_ref[:, h * DV_B:(h + 1) * DV_B]
        o_ref[:, h * DV_B:(h + 1) * DV_B] = (on * (r * jax.nn.sigmoid(r))).astype(BF16)


def _retention_consts(length):
    lg = jnp.log1p(-(2.0 ** (-5.0 - jnp.arange(N_HEADS_B, dtype=F32))))
    n = jnp.arange(length, dtype=F32)
    diff = n[:, None] - n[None, :]
    intra = jnp.where(diff >= 0, jnp.exp(jnp.maximum(diff, 0.0) * lg[:, None, None]), 0.0)
    rdec = jnp.exp((n[None, :] + 1.0) * lg[:, None])[..., None]
    kdec = jnp.exp((length - 1.0 - n)[None, :] * lg[:, None])[..., None]
    sdec = jnp.exp(length * lg)
    return intra, rdec, kdec, sdec


def _retention(q, k, v, rg, s0, streams, chunks, length):
    n = q.shape[0]
    intra, rdec, kdec, sdec = _retention_consts(length)
    row = lambda width: pl.BlockSpec((length, width), lambda b, c: (b * chunks + c, 0))
    const = lambda a: pl.BlockSpec(a.shape, lambda b, c: (0,) * a.ndim)
    st_spec = pl.BlockSpec((1, N_HEADS_B, DK_B, DV_B), lambda b, c: (b, 0, 0, 0))
    return pl.pallas_call(
        _ret_body,
        grid=(streams, chunks),
        in_specs=[row(Q_B), row(Q_B), row(V_B), row(V_B), st_spec,
                  const(intra), const(rdec), const(kdec), pl.BlockSpec(memory_space=pltpu.SMEM)],
        out_specs=[row(V_B), st_spec],
        out_shape=[jax.ShapeDtypeStruct((n, V_B), BF16),
                   jax.ShapeDtypeStruct((streams, N_HEADS_B, DK_B, DV_B), F32)],
        compiler_params=_params(("parallel", "arbitrary")),
        name="retention",
    )(q, k, v, rg, s0, intra, rdec, kdec, sdec)


def _merge_body(x_ref, oa_ref, ob_ref, ga_ref, gb_ref, wa_ref, wb_ref, wo_ref, o_ref):
    ya = _dot(oa_ref[...], wa_ref[...])
    yb = _dot(ob_ref[...], wb_ref[...])
    mix = jax.nn.sigmoid(ga_ref[...]) * ya + jax.nn.sigmoid(gb_ref[...]) * yb
    o_ref[...] = x_ref[...] + _dot(mix.astype(BF16), wo_ref[...])


def _merge(x, oa, ob, ga, gb, wa, wb, wo, tm):
    n = x.shape[0]
    row = lambda width: pl.BlockSpec((tm, width), lambda i: (i, 0))
    full = lambda a: pl.BlockSpec(a.shape, lambda i: (0, 0))
    return pl.pallas_call(
        _merge_body,
        grid=(n // tm,),
        in_specs=[row(D_MODEL), row(Q_A), row(V_B), row(D_MODEL), row(D_MODEL), full(wa), full(wb), full(wo)],
        out_specs=row(D_MODEL),
        out_shape=jax.ShapeDtypeStruct((n, D_MODEL), F32),
        compiler_params=_params(("parallel",)),
        name="merge",
    )(x, oa, ob, ga, gb, wa, wb, wo)


def _top16(s, order):
    t = s.shape[1]
    rank = lax.broadcasted_iota(I32, (PEER_TOPK, t), 0)
    vals = jnp.zeros((PEER_TOPK, t), F32)
    ids = jnp.zeros((PEER_TOPK, t), F32)
    for r in range(PEER_TOPK):
        m = jnp.max(s, axis=0, keepdims=True)
        pick = jnp.min(jnp.where(s == m, order, 3e38), axis=0, keepdims=True)
        s = jnp.where(order == pick, -jnp.inf, s)
        vals = jnp.where(rank == r, m, vals)
        ids = jnp.where(rank == r, pick, ids)
    return vals, ids


def _route_body(x_ref, g_ref, wq_ref, k1_ref, k2_ref, idx_ref, gate_ref, h_ref):
    h_ref[...] = _rmsnorm(x_ref[...], g_ref[...])
    h = h_ref[...].astype(BF16)
    t = h.shape[0]
    q_t = _dot_nt(wq_ref[...], h).astype(BF16)
    iota = lambda rows: lax.broadcasted_iota(I32, (rows, t), 0).astype(F32)
    key_id = iota(N_KEYS)
    flat = jnp.concatenate(
        [iota(PEER_TOPK)] + [a * PEER_TOPK + iota(8) for a in range(1, 8)] + [(8 + iota(8)) * PEER_TOPK],
        axis=0)
    rank = lax.broadcasted_iota(I32, (PEER_TOPK, t), 0)
    idx_rows, gate_rows = [], []
    for hd in range(PEER_HEADS):
        base = hd * PEER_QDIM
        s1 = _dot(k1_ref[...], q_t[base:base + PEER_HALF])
        s2 = _dot(k2_ref[...], q_t[base + PEER_HALF:base + PEER_QDIM])
        v1, i1 = _top16(s1, key_id)
        v2, i2 = _top16(s2, key_id)
        cand = jnp.concatenate(
            [v1[0:1] + v2]
            + [v1[a:a + 1] + v2[0:8] for a in range(1, 8)]
            + [v1[8:16] + v2[0:1]], axis=0)
        cidx = jnp.concatenate(
            [i1[0:1] * N_KEYS + i2]
            + [i1[a:a + 1] * N_KEYS + i2[0:8] for a in range(1, 8)]
            + [i1[8:16] * N_KEYS + i2[0:1]], axis=0)
        best, pos = _top16(cand, flat)
        eidx = jnp.zeros((PEER_TOPK, t), F32)
        for r in range(PEER_TOPK):
            picked = jnp.max(jnp.where(flat == pos[r:r + 1], cidx, -1.0), axis=0, keepdims=True)
            eidx = jnp.where(rank == r, picked, eidx)
        e = jnp.exp(best - best[0:1])
        gate_rows.append(e / jnp.sum(e, axis=0, keepdims=True))
        idx_rows.append(eidx.astype(I32))
    idx_ref[...] = jnp.concatenate(idx_rows, axis=0)
    gate_ref[...] = jnp.concatenate(gate_rows, axis=0)


def _route(x, g, wq_t, k1, k2, tm):
    n = x.shape[0]
    full = lambda a: pl.BlockSpec(a.shape, lambda i: (0, 0))
    col = pl.BlockSpec((PICKS, tm), lambda i: (0, i))
    return pl.pallas_call(
        _route_body,
        grid=(n // tm,),
        in_specs=[pl.BlockSpec((tm, D_MODEL), lambda i: (i, 0)), full(g), full(wq_t), full(k1), full(k2)],
        out_specs=[col, col, pl.BlockSpec((tm, D_MODEL), lambda i: (i, 0))],
        out_shape=[jax.ShapeDtypeStruct((PICKS, n), I32), jax.ShapeDtypeStruct((PICKS, n), F32),
                   jax.ShapeDtypeStruct((n, D_MODEL), F32)],
        compiler_params=_params(("parallel",)),
        name="routing",
    )(x, g, wq_t, k1, k2)


SC_CORES = 2
SC_SUBCORES = 16
SC_LANES = 16
SC_CHUNK = 16
SC_GROUP_TOKENS = (16384, 768)


def _gelu_via_exp(x):
    z = 0.7978845608028654 * (x + 0.044715 * x * x * x)
    return 0.5 * x * (2.0 - 2.0 / (1.0 + jnp.exp(2.0 * z)))


def _sc_experts(table, idx, gate, h):
    n = idx.shape[0]
    workers = SC_CORES * SC_SUBCORES
    per_worker = n // workers
    n_chunks = PICKS // SC_CHUNK
    steps = D_MODEL // SC_LANES
    mesh = plsc.VectorSubcoreMesh(core_axis_name="c", subcore_axis_name="s",
                                  num_cores=SC_CORES, num_subcores=SC_SUBCORES)

    def body(tab_hbm, idx_hbm, gate_hbm, h_hbm, y_hbm,
             idx_0, idx_1, gate_0, gate_1, h_0, h_1, y_0, y_1, rows_a, rows_b,
             in_sem_0, in_sem_1, y_sem_0, y_sem_1, sem_a, sem_b):
        base = (lax.axis_index("s") * SC_CORES + lax.axis_index("c")) * per_worker
        last = base + per_worker - 1
        sets = ((idx_0, gate_0, h_0, y_0, in_sem_0, y_sem_0), (idx_1, gate_1, h_1, y_1, in_sem_1, y_sem_1))
        bufs = ((rows_a, sem_a), (rows_b, sem_b))

        def in_copies(t, q):
            idx_v, gate_v, h_v, _, sem, _ = sets[q]
            return (pltpu.make_async_copy(idx_hbm.at[t], idx_v, sem),
                    pltpu.make_async_copy(gate_hbm.at[t], gate_v, sem),
                    pltpu.make_async_copy(h_hbm.at[t], h_v, sem))

        def y_copy(t, q):
            return pltpu.make_async_copy(sets[q][3], y_hbm.at[t], sets[q][5])

        def gather(c, b, q):
            rows, sem = bufs[b]
            return pltpu.make_async_copy(tab_hbm.at[sets[q][0].at[pl.ds(c * SC_CHUNK, SC_CHUNK)]], rows, sem)

        def compute(c, b, q):
            rows, _ = bufs[b]
            _, gate_v, h_v, y_v, _, _ = sets[q]

            def dot_step(j, accs):
                hj = h_v[pl.ds(j * SC_LANES, SC_LANES)]
                return tuple(accs[k] + rows[k, 0, pl.ds(j * SC_LANES, SC_LANES)] * hj for k in range(SC_CHUNK))

            accs = lax.fori_loop(0, steps, dot_step,
                                 tuple(jnp.zeros((SC_LANES,), F32) for _ in range(SC_CHUNK)))
            ws = []
            for k in range(SC_CHUNK):
                s = jnp.full((SC_LANES,), jnp.sum(accs[k]), F32)
                g = plsc.load_gather(gate_v, [jnp.full((SC_LANES,), c * SC_CHUNK + k, I32)])
                ws.append(_gelu_via_exp(s) * g)

            def sum_step(j, carry):
                acc = y_v[pl.ds(j * SC_LANES, SC_LANES)]
                for k in range(SC_CHUNK):
                    acc = acc + ws[k] * rows[k, 0, pl.ds(D_MODEL + j * SC_LANES, SC_LANES)]
                y_v[pl.ds(j * SC_LANES, SC_LANES)] = acc
                return carry

            lax.fori_loop(0, steps, sum_step, 0)

        def token(t, q, first):
            nxt = jnp.minimum(t + 1, last)
            for cp in in_copies(nxt, 1 - q):
                cp.start()
            y_v = sets[q][3]

            @pl.when(jnp.logical_not(first))
            def _():
                y_copy(t, q).wait()

            @pl.loop(0, steps)
            def _(j):
                y_v[pl.ds(j * SC_LANES, SC_LANES)] = jnp.zeros((SC_LANES,), F32)

            @pl.loop(0, n_chunks // 2)
            def _(p):
                c0 = p * 2
                gather(c0 + 1, 1, q).start()
                gather(c0, 0, q).wait()
                compute(c0, 0, q)

                @pl.when(p + 1 < n_chunks // 2)
                def _():
                    gather(c0 + 2, 0, q).start()

                gather(c0 + 1, 1, q).wait()
                compute(c0 + 1, 1, q)

            for cp in in_copies(nxt, 1 - q):
                cp.wait()
            gather(0, 0, 1 - q).start()
            y_copy(t, q).start()

        for cp in in_copies(base, 0):
            cp.start()
        for cp in in_copies(base, 0):
            cp.wait()
        gather(0, 0, 0).start()

        @pl.loop(0, per_worker // 2)
        def _(i):
            token(base + 2 * i, 0, i == 0)
            token(base + 2 * i + 1, 1, i == 0)

        gather(0, 0, 0).wait()
        y_copy(last - 1, 0).wait()
        y_copy(last, 1).wait()

    vec = lambda n_el, dt: [pltpu.VMEM((n_el,), dt), pltpu.VMEM((n_el,), dt)]
    return pl.kernel(
        body, out_type=jax.ShapeDtypeStruct((n, D_MODEL), F32), mesh=mesh,
        scratch_types=vec(PICKS, I32) + vec(PICKS, F32) + vec(D_MODEL, F32) + vec(D_MODEL, F32)
        + [pltpu.VMEM((SC_CHUNK, 1, 2 * D_MODEL), F32), pltpu.VMEM((SC_CHUNK, 1, 2 * D_MODEL), F32)]
        + [pltpu.SemaphoreType.DMA] * 6,
        compiler_params=pltpu.CompilerParams(needs_layout_passes=False),
        name="experts_sc",
    )(table, idx, gate, h)


def _finish_body(x_ref, y_ref, gf_ref, *rest_and_out):
    rest_and_out[-1][...] = _rmsnorm(x_ref[...] + y_ref[...], gf_ref[...])


def _finish(x, y, g_final, rest, tm):
    row = pl.BlockSpec((tm, D_MODEL), lambda i: (i, 0))
    extra = () if rest is None else (rest,)
    return pl.pallas_call(
        _finish_body,
        grid=(y.shape[0] // tm,),
        in_specs=[row, row, pl.BlockSpec(g_final.shape, lambda i: (0, 0))]
        + [pl.BlockSpec(memory_space=pl.ANY) for _ in extra],
        out_specs=row,
        out_shape=jax.ShapeDtypeStruct(x.shape if rest is not None else y.shape, F32),
        input_output_aliases={3: 0} if rest is not None else {},
        compiler_params=_params(("parallel",)),
        name="finish",
    )(x, y, g_final, *extra)


EXP_TOKENS = 256
EXP_SLOTS = 4


def _expert_body(x_ref, g_ref, gf_ref, idx_hbm, gate_ref, tab_hbm, o_ref,
                 idx_smem, buf0, buf1, buf2, buf3, h_buf, y_buf, sems, idx_sem, *, first_block):
    bufs = (buf0, buf1, buf2, buf3)
    ahead = EXP_SLOTS - 1
    i = pl.program_id(0) + first_block
    idx_copy = pltpu.make_async_copy(idx_hbm.at[pl.ds(i * (EXP_TOKENS * PICKS), EXP_TOKENS * PICKS)],
                                     idx_smem, idx_sem)
    idx_copy.start()
    h_buf[...] = _rmsnorm(x_ref[...], g_ref[...])
    idx_copy.wait()

    def issue(t, slot):
        for k in range(PICKS):
            e = idx_smem[t * PICKS + k]
            pltpu.async_copy(tab_hbm.at[e], bufs[slot].at[pl.ds(k, 1), :], sems.at[slot], priority=k % 2)

    def wait_slot(slot):
        pltpu.make_async_copy(bufs[(slot + 1) % EXP_SLOTS], bufs[slot], sems.at[slot]).wait()

    lane = lax.broadcasted_iota(I32, (PICKS, EXP_TOKENS), 1)

    def compute(t, slot):
        h = h_buf[pl.ds(t, 1), :]
        u = bufs[slot][:, :D_MODEL]
        s = jnp.sum(u * h, axis=-1, keepdims=True)
        gate = jnp.sum(jnp.where(lane == t, gate_ref[...], 0.0), axis=-1, keepdims=True)
        w = gate * jax.nn.gelu(s)
        v = bufs[slot][:, D_MODEL:]
        y_buf[pl.ds(t, 1), :] = jnp.sum(w * v, axis=0, keepdims=True)

    def group(t0, issue_upto):
        for s in range(EXP_SLOTS):
            wait_slot(s)
            if s < issue_upto:
                issue(t0 + s + ahead, (s + ahead) % EXP_SLOTS)
            compute(t0 + s, s)

    for s in range(ahead):
        issue(s, s)

    def steady(q, carry):
        group(q * EXP_SLOTS, EXP_SLOTS)
        return carry

    n_groups = EXP_TOKENS // EXP_SLOTS
    lax.fori_loop(0, n_groups - 1, steady, 0)
    group((n_groups - 1) * EXP_SLOTS, EXP_SLOTS - ahead)
    o_ref[...] = _rmsnorm(x_ref[...] + y_buf[...], gf_ref[...])


def _experts(x, g, g_final, idx_flat, gate_t, table, first_token):
    n = x.shape[0]
    first_block = first_token // EXP_TOKENS
    full = lambda a: pl.BlockSpec(a.shape, lambda i: (0, 0))
    row = pl.BlockSpec((EXP_TOKENS, D_MODEL), lambda i: (i + first_block, 0))
    return pl.pallas_call(
        functools.partial(_expert_body, first_block=first_block),
        grid=(n // EXP_TOKENS - first_block,),
        in_specs=[row, full(g), full(g_final),
                  pl.BlockSpec(memory_space=pl.ANY),
                  pl.BlockSpec((PICKS, EXP_TOKENS), lambda i: (0, i + first_block)),
                  pl.BlockSpec(memory_space=pl.ANY)],
        out_specs=row,
        out_shape=jax.ShapeDtypeStruct((n, D_MODEL), F32),
        scratch_shapes=[pltpu.SMEM((EXP_TOKENS * PICKS,), I32)]
        + [pltpu.VMEM((PICKS, 2 * D_MODEL), F32) for _ in range(EXP_SLOTS)]
        + [pltpu.VMEM((EXP_TOKENS, D_MODEL), F32),
           pltpu.VMEM((EXP_TOKENS, D_MODEL), F32),
           pltpu.SemaphoreType.DMA((EXP_SLOTS,)),
           pltpu.SemaphoreType.DMA(())],
        compiler_params=pltpu.CompilerParams(dimension_semantics=("arbitrary",), vmem_limit_bytes=VMEM_LIMIT,
                                             disable_bounds_checks=True),
        name="experts",
    )(x, g, g_final, idx_flat, gate_t, table)


def _attention_masks(tq):
    i = jnp.arange(tq, dtype=I32)[:, None]
    r = jnp.arange(ATT_KEYS, dtype=I32)[None, :]
    negdist = -jnp.abs(WINDOW + i - r).astype(F32)
    return negdist, i, r


def _layer(x, n_groups, rows, swa_inputs, ret_state, ret_len, w, tm, n_sc):
    (norm_mix, w_in, b_gate, sinks, wa, wb, wo, norm_ffn, wq_t, k1, k2, table, norm_final) = w
    qa, ka, va, qb, kb, vb, rg, ga, gb = _project(x, norm_mix, w_in, b_gate, tm)
    oa, k_rows, v_rows = swa_inputs(qa, ka, va, sinks)
    ob, s_fin = _retention(qb, kb, vb, rg, ret_state, n_groups, rows // ret_len, ret_len)
    x1 = _merge(x, oa, ob, ga, gb, wa, wb, wo, tm)
    idx_t, gate_t, h2 = _route(x1, norm_ffn, wq_t, k1, k2, tm)
    y = None
    if n_sc < x.shape[0]:
        y = _experts(x1, norm_ffn, norm_final, idx_t.T.reshape(-1), gate_t, table, n_sc)
    if n_sc:
        y_sc = _sc_experts(table, idx_t[:, :n_sc].T, gate_t[:, :n_sc].T, h2)
        y = _finish(x1, y_sc, norm_final, y, tm)
    return y, k_rows, v_rows, s_fin


def kernel(x_prompt, x_sample, cache_swa_k, cache_swa_v, state_ret, norm_mix, w_in, b_gate, attn_sinks,
           w_branch_a, w_branch_b, w_out, norm_ffn, peer_w_q, peer_sub_k1, peer_sub_k2, peer_u, peer_v,
           norm_final):
    batch, seq, _ = x_prompt.shape
    dec_batch, dec_seq, _ = x_sample.shape
    assert norm_mix.shape[0] == 1, "single-layer trunk"
    table = jnp.concatenate([peer_u[0], peer_v[0]], axis=1)[:, None, :]
    w = (norm_mix[0][None], w_in[0].astype(BF16), b_gate[0][None], attn_sinks[0],
         w_branch_a[0].astype(BF16), w_branch_b[0].astype(BF16), w_out[0].astype(BF16),
         norm_ffn[0][None], peer_w_q[0].T.astype(BF16), peer_sub_k1[0].astype(BF16),
         peer_sub_k2[0].astype(BF16), table, norm_final[None])

    def swa_prompt(qa, ka, va, sinks):
        streams = qa.shape[0] // seq
        tq = 2 * CHUNK
        negdist, i, r = _attention_masks(tq)
        lq, lk = i // CHUNK, r // CHUNK
        band = (lk >= lq) & (lk <= lq + 2)
        allow = jnp.stack([band & (r >= WINDOW), band]).astype(F32)
        oa = _swa(sinks, qa, ka, va, negdist, allow, streams, seq // tq, tq)
        tail = lambda a: a.reshape(streams, seq, N_KV_A, HEAD_DIM_A)[:, seq - WINDOW:]
        return oa, tail(ka), tail(va)

    def swa_sample(qa, ka, va, sinks):
        negdist, i, r = _attention_masks(dec_seq)
        visible = jnp.broadcast_to(r < WINDOW + dec_seq, (dec_seq, ATT_KEYS))
        allow = jnp.stack([visible, visible]).astype(F32)
        pad = jnp.zeros((dec_batch, ATT_KEYS - WINDOW - dec_seq, KV_A), F32)
        k_all = jnp.concatenate([cache_swa_k[0].reshape(dec_batch, WINDOW, KV_A),
                                 ka.reshape(dec_batch, dec_seq, KV_A), pad], axis=1)
        v_all = jnp.concatenate([cache_swa_v[0].reshape(dec_batch, WINDOW, KV_A),
                                 va.reshape(dec_batch, dec_seq, KV_A), pad], axis=1)
        oa = _swa(sinks, qa, k_all.reshape(-1, KV_A), v_all.reshape(-1, KV_A), negdist, allow,
                  dec_batch, 1, dec_seq)
        tail = lambda a: a[:, dec_seq:WINDOW + dec_seq].reshape(dec_batch, WINDOW, N_KV_A, HEAD_DIM_A)
        return oa, tail(k_all), tail(v_all)

    ys, ks, vs, ss = _layer(x_sample.reshape(dec_batch * dec_seq, D_MODEL), dec_batch, dec_seq, swa_sample,
                            state_ret[0], dec_seq, w, 128, 0)

    per_group = batch // len(SC_GROUP_TOKENS)
    outs = []
    for g, n_sc in enumerate(SC_GROUP_TOKENS):
        xg = x_prompt[g * per_group:(g + 1) * per_group].reshape(per_group * seq, D_MODEL)
        s0 = jnp.zeros((per_group, N_HEADS_B, DK_B, DV_B), F32)
        outs.append(_layer(xg, per_group, seq, swa_prompt, s0, 256, w, 256, n_sc))
    yp, kp, vp, sp = (jnp.concatenate(parts, axis=0) for parts in zip(*outs))

    return (yp.reshape(batch, seq, D_MODEL), ys.reshape(dec_batch, dec_seq, D_MODEL),
            kp[None], vp[None], sp[None], ks[None], vs[None], ss[None])
```

```python
import functools

import jax
import jax.numpy as jnp
from jax import lax
from jax.experimental import pallas as pl
from jax.experimental.pallas import tpu as pltpu
from jax.experimental.pallas import tpu_sc as plsc

F32 = jnp.float32
BF16 = jnp.bfloat16
I32 = jnp.int32

D_MODEL = 1024
CHUNK = 64
EPS = 1e-6
NEG_INF = -1e30
PAST_LEN = 2048

N_HEADS_A = 8
N_KV_A = 2
GROUP_A = N_HEADS_A // N_KV_A
HEAD_DIM_A = 64
WINDOW = 128
N_HEADS_B = 4
DK_B = 128
DV_B = 256
Q_A = N_HEADS_A * HEAD_DIM_A
KV_A = N_KV_A * HEAD_DIM_A
Q_B = N_HEADS_B * DK_B
V_B = N_HEADS_B * DV_B
D_IN = Q_A + 2 * KV_A + 2 * Q_B + 2 * V_B + 2 * D_MODEL
OFF_QA = 0
OFF_KA = OFF_QA + Q_A
OFF_VA = OFF_KA + KV_A
OFF_QB = OFF_VA + KV_A
OFF_KB = OFF_QB + Q_B
OFF_VB = OFF_KB + Q_B
OFF_RG = OFF_VB + V_B
OFF_GA = OFF_RG + V_B
OFF_GB = OFF_GA + D_MODEL

N_KEYS = 128
N_EXPERTS = N_KEYS * N_KEYS
PEER_HEADS = 8
PEER_QDIM = 256
PEER_HALF = PEER_QDIM // 2
PEER_TOPK = 16
PICKS = PEER_HEADS * PEER_TOPK

LANES = 128
ATT_KEYS = 2 * WINDOW
VMEM_LIMIT = 56 * 1024 * 1024


def _rmsnorm(x, g):
    return x * lax.rsqrt(jnp.mean(x * x, axis=-1, keepdims=True) + EPS) * g


def _dot(a, b):
    return jnp.dot(a, b, preferred_element_type=F32)


def _dot_nt(a, b):
    return lax.dot_general(a, b, (((1,), (1,)), ((), ())), preferred_element_type=F32)


def _dot_tn(a, b):
    return lax.dot_general(a, b, (((0,), (0,)), ((), ())), preferred_element_type=F32)


def _params(sem):
    return pltpu.CompilerParams(dimension_semantics=sem, vmem_limit_bytes=VMEM_LIMIT)


def _proj_body(x_ref, g_ref, w_ref, bg_ref, qa_ref, ka_ref, va_ref, qb_ref, kb_ref, vb_ref,
               rg_ref, ga_ref, gb_ref):
    h = _rmsnorm(x_ref[...], g_ref[...]).astype(BF16)

    def mm(lo, width):
        return _dot(h, w_ref[:, lo:lo + width])

    qa_ref[...] = (mm(OFF_QA, Q_A) * (HEAD_DIM_A ** -0.5)).astype(BF16)
    ka_ref[...] = mm(OFF_KA, KV_A)
    va_ref[...] = mm(OFF_VA, KV_A)
    qb_ref[...] = mm(OFF_QB, Q_B).astype(BF16)
    kb_ref[...] = (mm(OFF_KB, Q_B) * (DK_B ** -0.5)).astype(BF16)
    vb_ref[...] = mm(OFF_VB, V_B).astype(BF16)
    rg_ref[...] = mm(OFF_RG, V_B)
    ga_ref[...] = mm(OFF_GA, D_MODEL) + bg_ref[:, :D_MODEL]
    gb_ref[...] = mm(OFF_GB, D_MODEL) + bg_ref[:, D_MODEL:]


def _project(x, g, w_bf16, b_gate, tm):
    n = x.shape[0]
    row = lambda width: pl.BlockSpec((tm, width), lambda i: (i, 0))
    full = lambda a: pl.BlockSpec(a.shape, lambda i: (0, 0))
    widths = (Q_A, KV_A, KV_A, Q_B, Q_B, V_B, V_B, D_MODEL, D_MODEL)
    dtypes = (BF16, F32, F32, BF16, BF16, BF16, F32, F32, F32)
    return pl.pallas_call(
        _proj_body,
        grid=(n // tm,),
        in_specs=[row(D_MODEL), full(g), full(w_bf16), full(b_gate)],
        out_specs=[row(w) for w in widths],
        out_shape=[jax.ShapeDtypeStruct((n, w), d) for w, d in zip(widths, dtypes)],
        compiler_params=_params(("parallel",)),
        name="projection",
    )(x, g, w_bf16, b_gate)


def _swa_body(sink_ref, q_ref, kp_ref, kc_ref, vp_ref, vc_ref, nd_ref, al_ref, o_ref):
    k = jnp.concatenate([kp_ref[...], kc_ref[...]], axis=0)
    v = jnp.concatenate([vp_ref[...], vc_ref[...]], axis=0)
    lane = lax.broadcasted_iota(I32, k.shape, 1)
    low = lane < HEAD_DIM_A
    k_sw = pltpu.roll(k, HEAD_DIM_A, axis=1)
    v_sw = pltpu.roll(v, HEAD_DIM_A, axis=1)
    negdist = nd_ref[...]
    allowed = al_ref[0] > 0.5
    for kv in range(N_KV_A):
        own = low if kv == 0 else jnp.logical_not(low)
        k_rep = jnp.where(own, k, k_sw)
        v_rep = jnp.where(own, v, v_sw)
        zero = jnp.zeros_like(k_rep)
        kk = jnp.concatenate([jnp.where(low, k_rep, zero), jnp.where(low, zero, k_rep)], axis=0).astype(BF16)
        vv = jnp.concatenate([jnp.where(low, v_rep, zero), jnp.where(low, zero, v_rep)], axis=0).astype(BF16)
        for a in range(GROUP_A // 2):
            col = kv * GROUP_A * HEAD_DIM_A + a * LANES
            s2 = _dot_nt(q_ref[:, col:col + LANES], kk)
            ps = []
            for u in range(2):
                head = kv * GROUP_A + 2 * a + u
                slope = 2.0 ** (-8.0 * (head + 1) / N_HEADS_A)
                sink = sink_ref[head]
                s = s2[:, u * ATT_KEYS:(u + 1) * ATT_KEYS] + slope * negdist
                s = jnp.where(allowed, s, NEG_INF)
                m = jnp.maximum(jnp.max(s, axis=-1, keepdims=True), sink)
                p = jnp.exp(s - m)
                denom = jnp.sum(p, axis=-1, keepdims=True) + jnp.exp(sink - m)
                ps.append((p / denom).astype(BF16))
            o = _dot(jnp.concatenate(ps, axis=1), vv)
            o_ref[:, col:col + LANES] = o.astype(BF16)


def _swa(sinks, q, k, v, negdist, allow, groups, q_blocks, tq):
    n = q.shape[0]
    kblocks = k.shape[0] // LANES // groups
    shift = kblocks - q_blocks
    prev = lambda b, j: (b * kblocks + jnp.maximum(j + shift - 1, 0), 0)
    cur = lambda b, j: (b * kblocks + j + shift, 0)
    kv_spec = lambda f: pl.BlockSpec((LANES, KV_A), f)
    return pl.pallas_call(
        _swa_body,
        grid=(groups, q_blocks),
        in_specs=[pl.BlockSpec(memory_space=pltpu.SMEM),
                  pl.BlockSpec((tq, Q_A), lambda b, j: (b * q_blocks + j, 0)),
                  kv_spec(prev), kv_spec(cur), kv_spec(prev), kv_spec(cur),
                  pl.BlockSpec((tq, ATT_KEYS), lambda b, j: (0, 0)),
                  pl.BlockSpec((1, tq, ATT_KEYS), lambda b, j: (jnp.minimum(j + shift, 1), 0, 0))],
        out_specs=pl.BlockSpec((tq, Q_A), lambda b, j: (b * q_blocks + j, 0)),
        out_shape=jax.ShapeDtypeStruct((n, Q_A), BF16),
        compiler_params=_params(("parallel", "parallel")),
        name="attention",
    )(sinks, q, k, k, v, v, negdist, allow)


def _ret_body(q_ref, k_ref, v_ref, rg_ref, s0_ref, intra_ref, rdec_ref, kdec_ref, sdec_ref,
              o_ref, st_ref):
    c = pl.program_id(1)

    @pl.when(c == 0)
    def _():
        st_ref[...] = s0_ref[...]

    for h in range(N_HEADS_B):
        q = q_ref[:, h * DK_B:(h + 1) * DK_B]
        k = k_ref[:, h * DK_B:(h + 1) * DK_B]
        v = v_ref[:, h * DV_B:(h + 1) * DV_B]
        st = st_ref[0, h]
        scores = (_dot_nt(q, k) * intra_ref[h]).astype(BF16)
        o = _dot(scores, v) + _dot(q, st.astype(BF16)) * rdec_ref[h]
        k_dec = (k.astype(F32) * kdec_ref[h]).astype(BF16)
        st_ref[0, h] = sdec_ref[h] * st + _dot_tn(k_dec, v)
        mu = jnp.mean(o, axis=-1, keepdims=True)
        d = o - mu
        var = jnp.mean(d * d, axis=-1, keepdims=True)
        on = d * lax.rsqrt(var + EPS)
        r = rg_ref[:, h * DV_B:(h + 1) * DV_B]
        o_ref[:, h * DV_B:(h + 1) * DV_B] = (on * (r * jax.nn.sigmoid(r))).astype(BF16)


def _retention_consts(length):
    lg = jnp.log1p(-(2.0 ** (-5.0 - jnp.arange(N_HEADS_B, dtype=F32))))
    n = jnp.arange(length, dtype=F32)
    diff = n[:, None] - n[None, :]
    intra = jnp.where(diff >= 0, jnp.exp(jnp.maximum(diff, 0.0) * lg[:, None, None]), 0.0)
    rdec = jnp.exp((n[None, :] + 1.0) * lg[:, None])[..., None]
    kdec = jnp.exp((length - 1.0 - n)[None, :] * lg[:, None])[..., None]
    sdec = jnp.exp(length * lg)
    return intra, rdec, kdec, sdec


def _retention(q, k, v, rg, s0, streams, chunks, length):
    n = q.shape[0]
    intra, rdec, kdec, sdec = _retention_consts(length)
    row = lambda width: pl.BlockSpec((length, width), lambda b, c: (b * chunks + c, 0))
    const = lambda a: pl.BlockSpec(a.shape, lambda b, c: (0,) * a.ndim)
    st_spec = pl.BlockSpec((1, N_HEADS_B, DK_B, DV_B), lambda b, c: (b, 0, 0, 0))
    return pl.pallas_call(
        _ret_body,
        grid=(streams, chunks),
        in_specs=[row(Q_B), row(Q_B), row(V_B), row(V_B), st_spec,
                  const(intra), const(rdec), const(kdec), pl.BlockSpec(memory_space=pltpu.SMEM)],
        out_specs=[row(V_B), st_spec],
        out_shape=[jax.ShapeDtypeStruct((n, V_B), BF16),
                   jax.ShapeDtypeStruct((streams, N_HEADS_B, DK_B, DV_B), F32)],
        compiler_params=_params(("parallel", "arbitrary")),
        name="retention",
    )(q, k, v, rg, s0, intra, rdec, kdec, sdec)


def _merge_body(x_ref, oa_ref, ob_ref, ga_ref, gb_ref, wa_ref, wb_ref, wo_ref, o_ref):
    ya = _dot(oa_ref[...], wa_ref[...])
    yb = _dot(ob_ref[...], wb_ref[...])
    mix = jax.nn.sigmoid(ga_ref[...]) * ya + jax.nn.sigmoid(gb_ref[...]) * yb
    o_ref[...] = x_ref[...] + _dot(mix.astype(BF16), wo_ref[...])


def _merge(x, oa, ob, ga, gb, wa, wb, wo, tm):
    n = x.shape[0]
    row = lambda width: pl.BlockSpec((tm, width), lambda i: (i, 0))
    full = lambda a: pl.BlockSpec(a.shape, lambda i: (0, 0))
    return pl.pallas_call(
        _merge_body,
        grid=(n // tm,),
        in_specs=[row(D_MODEL), row(Q_A), row(V_B), row(D_MODEL), row(D_MODEL), full(wa), full(wb), full(wo)],
        out_specs=row(D_MODEL),
        out_shape=jax.ShapeDtypeStruct((n, D_MODEL), F32),
        compiler_params=_params(("parallel",)),
        name="merge",
    )(x, oa, ob, ga, gb, wa, wb, wo)


def _top16(s, order):
    t = s.shape[1]
    rank = lax.broadcasted_iota(I32, (PEER_TOPK, t), 0)
    vals = jnp.zeros((PEER_TOPK, t), F32)
    ids = jnp.zeros((PEER_TOPK, t), F32)
    for r in range(PEER_TOPK):
        m = jnp.max(s, axis=0, keepdims=True)
        pick = jnp.min(jnp.where(s == m, order, 3e38), axis=0, keepdims=True)
        s = jnp.where(order == pick, -jnp.inf, s)
        vals = jnp.where(rank == r, m, vals)
        ids = jnp.where(rank == r, pick, ids)
    return vals, ids


def _route_body(x_ref, g_ref, wq_ref, k1_ref, k2_ref, idx_ref, gate_ref, h_ref):
    h_ref[...] = _rmsnorm(x_ref[...], g_ref[...])
    h = h_ref[...].astype(BF16)
    t = h.shape[0]
    q_t = _dot_nt(wq_ref[...], h).astype(BF16)
    iota = lambda rows: lax.broadcasted_iota(I32, (rows, t), 0).astype(F32)
    key_id = iota(N_KEYS)
    flat = jnp.concatenate(
        [iota(PEER_TOPK)] + [a * PEER_TOPK + iota(8) for a in range(1, 8)] + [(8 + iota(8)) * PEER_TOPK],
        axis=0)
    rank = lax.broadcasted_iota(I32, (PEER_TOPK, t), 0)
    idx_rows, gate_rows = [], []
    for hd in range(PEER_HEADS):
        base = hd * PEER_QDIM
        s1 = _dot(k1_ref[...], q_t[base:base + PEER_HALF])
        s2 = _dot(k2_ref[...], q_t[base + PEER_HALF:base + PEER_QDIM])
        v1, i1 = _top16(s1, key_id)
        v2, i2 = _top16(s2, key_id)
        cand = jnp.concatenate(
            [v1[0:1] + v2]
            + [v1[a:a + 1] + v2[0:8] for a in range(1, 8)]
            + [v1[8:16] + v2[0:1]], axis=0)
        cidx = jnp.concatenate(
            [i1[0:1] * N_KEYS + i2]
            + [i1[a:a + 1] * N_KEYS + i2[0:8] for a in range(1, 8)]
            + [i1[8:16] * N_KEYS + i2[0:1]], axis=0)
        best, pos = _top16(cand, flat)
        eidx = jnp.zeros((PEER_TOPK, t), F32)
        for r in range(PEER_TOPK):
            picked = jnp.max(jnp.where(flat == pos[r:r + 1], cidx, -1.0), axis=0, keepdims=True)
            eidx = jnp.where(rank == r, picked, eidx)
        e = jnp.exp(best - best[0:1])
        gate_rows.append(e / jnp.sum(e, axis=0, keepdims=True))
        idx_rows.append(eidx.astype(I32))
    idx_ref[...] = jnp.concatenate(idx_rows, axis=0)
    gate_ref[...] = jnp.concatenate(gate_rows, axis=0)


def _route(x, g, wq_t, k1, k2, tm):
    n = x.shape[0]
    full = lambda a: pl.BlockSpec(a.shape, lambda i: (0, 0))
    col = pl.BlockSpec((PICKS, tm), lambda i: (0, i))
    return pl.pallas_call(
        _route_body,
        grid=(n // tm,),
        in_specs=[pl.BlockSpec((tm, D_MODEL), lambda i: (i, 0)), full(g), full(wq_t), full(k1), full(k2)],
        out_specs=[col, col, pl.BlockSpec((tm, D_MODEL), lambda i: (i, 0))],
        out_shape=[jax.ShapeDtypeStruct((PICKS, n), I32), jax.ShapeDtypeStruct((PICKS, n), F32),
                   jax.ShapeDtypeStruct((n, D_MODEL), F32)],
        compiler_params=_params(("parallel",)),
        name="routing",
    )(x, g, wq_t, k1, k2)


SC_CORES = 2
SC_SUBCORES = 16
SC_LANES = 16
SC_CHUNK = 16
SC_GROUP_TOKENS = (16384, 768)


def _gelu_via_exp(x):
    z = 0.7978845608028654 * (x + 0.044715 * x * x * x)
    return 0.5 * x * (2.0 - 2.0 / (1.0 + jnp.exp(2.0 * z)))


def _sc_experts(table, idx, gate, h):
    n = idx.shape[0]
    workers = SC_CORES * SC_SUBCORES
    per_worker = n // workers
    n_chunks = PICKS // SC_CHUNK
    steps = D_MODEL // SC_LANES
    mesh = plsc.VectorSubcoreMesh(core_axis_name="c", subcore_axis_name="s",
                                  num_cores=SC_CORES, num_subcores=SC_SUBCORES)

    def body(tab_hbm, idx_hbm, gate_hbm, h_hbm, y_hbm,
             idx_0, idx_1, gate_0, gate_1, h_0, h_1, y_0, y_1, rows_a, rows_b,
             in_sem_0, in_sem_1, y_sem_0, y_sem_1, sem_a, sem_b):
        base = (lax.axis_index("s") * SC_CORES + lax.axis_index("c")) * per_worker
        last = base + per_worker - 1
        sets = ((idx_0, gate_0, h_0, y_0, in_sem_0, y_sem_0), (idx_1, gate_1, h_1, y_1, in_sem_1, y_sem_1))
        bufs = ((rows_a, sem_a), (rows_b, sem_b))

        def in_copies(t, q):
            idx_v, gate_v, h_v, _, sem, _ = sets[q]
            return (pltpu.make_async_copy(idx_hbm.at[t], idx_v, sem),
                    pltpu.make_async_copy(gate_hbm.at[t], gate_v, sem),
                    pltpu.make_async_copy(h_hbm.at[t], h_v, sem))

        def y_copy(t, q):
            return pltpu.make_async_copy(sets[q][3], y_hbm.at[t], sets[q][5])

        def gather(c, b, q):
            rows, sem = bufs[b]
            return pltpu.make_async_copy(tab_hbm.at[sets[q][0].at[pl.ds(c * SC_CHUNK, SC_CHUNK)]], rows, sem)

        def compute(c, b, q):
            rows, _ = bufs[b]
            _, gate_v, h_v, y_v, _, _ = sets[q]

            def dot_step(j, accs):
                hj = h_v[pl.ds(j * SC_LANES, SC_LANES)]
                return tuple(accs[k] + rows[k, 0, pl.ds(j * SC_LANES, SC_LANES)] * hj for k in range(SC_CHUNK))

            accs = lax.fori_loop(0, steps, dot_step,
                                 tuple(jnp.zeros((SC_LANES,), F32) for _ in range(SC_CHUNK)))
            ws = []
            for k in range(SC_CHUNK):
                s = jnp.full((SC_LANES,), jnp.sum(accs[k]), F32)
                g = plsc.load_gather(gate_v, [jnp.full((SC_LANES,), c * SC_CHUNK + k, I32)])
                ws.append(_gelu_via_exp(s) * g)

            def sum_step(j, carry):
                acc = y_v[pl.ds(j * SC_LANES, SC_LANES)]
                for k in range(SC_CHUNK):
                    acc = acc + ws[k] * rows[k, 0, pl.ds(D_MODEL + j * SC_LANES, SC_LANES)]
                y_v[pl.ds(j * SC_LANES, SC_LANES)] = acc
                return carry

            lax.fori_loop(0, steps, sum_step, 0)

        def token(t, q, first):
            nxt = jnp.minimum(t + 1, last)
            for cp in in_copies(nxt, 1 - q):
                cp.start()
            y_v = sets[q][3]

            @pl.when(jnp.logical_not(first))
            def _():
                y_copy(t, q).wait()

            @pl.loop(0, steps)
            def _(j):
                y_v[pl.ds(j * SC_LANES, SC_LANES)] = jnp.zeros((SC_LANES,), F32)

            @pl.loop(0, n_chunks // 2)
            def _(p):
                c0 = p * 2
                gather(c0 + 1, 1, q).start()
                gather(c0, 0, q).wait()
                compute(c0, 0, q)

                @pl.when(p + 1 < n_chunks // 2)
                def _():
                    gather(c0 + 2, 0, q).start()

                gather(c0 + 1, 1, q).wait()
                compute(c0 + 1, 1, q)

            for cp in in_copies(nxt, 1 - q):
                cp.wait()
            gather(0, 0, 1 - q).start()
            y_copy(t, q).start()

        for cp in in_copies(base, 0):
            cp.start()
        for cp in in_copies(base, 0):
            cp.wait()
        gather(0, 0, 0).start()

        @pl.loop(0, per_worker // 2)
        def _(i):
            token(base + 2 * i, 0, i == 0)
            token(base + 2 * i + 1, 1, i == 0)

        gather(0, 0, 0).wait()
        y_copy(last - 1, 0).wait()
        y_copy(last, 1).wait()

    vec = lambda n_el, dt: [pltpu.VMEM((n_el,), dt), pltpu.VMEM((n_el,), dt)]
    return pl.kernel(
        body, out_type=jax.ShapeDtypeStruct((n, D_MODEL), F32), mesh=mesh,
        scratch_types=vec(PICKS, I32) + vec(PICKS, F32) + vec(D_MODEL, F32) + vec(D_MODEL, F32)
        + [pltpu.VMEM((SC_CHUNK, 1, 2 * D_MODEL), F32), pltpu.VMEM((SC_CHUNK, 1, 2 * D_MODEL), F32)]
        + [pltpu.SemaphoreType.DMA] * 6,
        compiler_params=pltpu.CompilerParams(needs_layout_passes=False),
        name="experts_sc",
    )(table, idx, gate, h)


def _finish_body(x_ref, y_ref, gf_ref, *rest_and_out):
    rest_and_out[-1][...] = _rmsnorm(x_ref[...] + y_ref[...], gf_ref[...])


def _finish(x, y, g_final, rest, tm):
    row = pl.BlockSpec((tm, D_MODEL), lambda i: (i, 0))
    extra = () if rest is None else (rest,)
    return pl.pallas_call(
        _finish_body,
        grid=(y.shape[0] // tm,),
        in_specs=[row, row, pl.BlockSpec(g_final.shape, lambda i: (0, 0))]
        + [pl.BlockSpec(memory_space=pl.ANY) for _ in extra],
        out_specs=row,
        out_shape=jax.ShapeDtypeStruct(x.shape if rest is not None else y.shape, F32),
        input_output_aliases={3: 0} if rest is not None else {},
        compiler_params=_params(("parallel",)),
        name="finish",
    )(x, y, g_final, *extra)


EXP_TOKENS = 256
EXP_SLOTS = 4


def _expert_body(x_ref, g_ref, gf_ref, idx_hbm, gate_ref, tab_hbm, o_ref,
                 idx_smem, buf0, buf1, buf2, buf3, h_buf, y_buf, sems, idx_sem, *, first_block):
    bufs = (buf0, buf1, buf2, buf3)
    ahead = EXP_SLOTS - 1
    i = pl.program_id(0) + first_block
    idx_copy = pltpu.make_async_copy(idx_hbm.at[pl.ds(i * (EXP_TOKENS * PICKS), EXP_TOKENS * PICKS)],
                                     idx_smem, idx_sem)
    idx_copy.start()
    h_buf[...] = _rmsnorm(x_ref[...], g_ref[...])
    idx_copy.wait()

    def issue(t, slot):
        for k in range(PICKS):
            e = idx_smem[t * PICKS + k]
            pltpu.async_copy(tab_hbm.at[e], bufs[slot].at[pl.ds(k, 1), :], sems.at[slot], priority=k % 2)

    def wait_slot(slot):
        pltpu.make_async_copy(bufs[(slot + 1) % EXP_SLOTS], bufs[slot], sems.at[slot]).wait()

    lane = lax.broadcasted_iota(I32, (PICKS, EXP_TOKENS), 1)

    def compute(t, slot):
        h = h_buf[pl.ds(t, 1), :]
        u = bufs[slot][:, :D_MODEL]
        s = jnp.sum(u * h, axis=-1, keepdims=True)
        gate = jnp.sum(jnp.where(lane == t, gate_ref[...], 0.0), axis=-1, keepdims=True)
        w = gate * jax.nn.gelu(s)
        v = bufs[slot][:, D_MODEL:]
        y_buf[pl.ds(t, 1), :] = jnp.sum(w * v, axis=0, keepdims=True)

    def group(t0, issue_upto):
        for s in range(EXP_SLOTS):
            wait_slot(s)
            if s < issue_upto:
                issue(t0 + s + ahead, (s + ahead) % EXP_SLOTS)
            compute(t0 + s, s)

    for s in range(ahead):
        issue(s, s)

    def steady(q, carry):
        group(q * EXP_SLOTS, EXP_SLOTS)
        return carry

    n_groups = EXP_TOKENS // EXP_SLOTS
    lax.fori_loop(0, n_groups - 1, steady, 0)
    group((n_groups - 1) * EXP_SLOTS, EXP_SLOTS - ahead)
    o_ref[...] = _rmsnorm(x_ref[...] + y_buf[...], gf_ref[...])


def _experts(x, g, g_final, idx_flat, gate_t, table, first_token):
    n = x.shape[0]
    first_block = first_token // EXP_TOKENS
    full = lambda a: pl.BlockSpec(a.shape, lambda i: (0, 0))
    row = pl.BlockSpec((EXP_TOKENS, D_MODEL), lambda i: (i + first_block, 0))
    return pl.pallas_call(
        functools.partial(_expert_body, first_block=first_block),
        grid=(n // EXP_TOKENS - first_block,),
        in_specs=[row, full(g), full(g_final),
                  pl.BlockSpec(memory_space=pl.ANY),
                  pl.BlockSpec((PICKS, EXP_TOKENS), lambda i: (0, i + first_block)),
                  pl.BlockSpec(memory_space=pl.ANY)],
        out_specs=row,
        out_shape=jax.ShapeDtypeStruct((n, D_MODEL), F32),
        scratch_shapes=[pltpu.SMEM((EXP_TOKENS * PICKS,), I32)]
        + [pltpu.VMEM((PICKS, 2 * D_MODEL), F32) for _ in range(EXP_SLOTS)]
        + [pltpu.VMEM((EXP_TOKENS, D_MODEL), F32),
           pltpu.VMEM((EXP_TOKENS, D_MODEL), F32),
           pltpu.SemaphoreType.DMA((EXP_SLOTS,)),
           pltpu.SemaphoreType.DMA(())],
        compiler_params=pltpu.CompilerParams(dimension_semantics=("arbitrary",), vmem_limit_bytes=VMEM_LIMIT,
                                             disable_bounds_checks=True),
        name="experts",
    )(x, g, g_final, idx_flat, gate_t, table)


def _attention_masks(tq):
    i = jnp.arange(tq, dtype=I32)[:, None]
    r = jnp.arange(ATT_KEYS, dtype=I32)[None, :]
    negdist = -jnp.abs(WINDOW + i - r).astype(F32)
    return negdist, i, r


def _layer(x, n_groups, rows, swa_inputs, ret_state, ret_len, w, tm, n_sc):
    (norm_mix, w_in, b_gate, sinks, wa, wb, wo, norm_ffn, wq_t, k1, k2, table, norm_final) = w
    qa, ka, va, qb, kb, vb, rg, ga, gb = _project(x, norm_mix, w_in, b_gate, tm)
    oa, k_rows, v_rows = swa_inputs(qa, ka, va, sinks)
    ob, s_fin = _retention(qb, kb, vb, rg, ret_state, n_groups, rows // ret_len, ret_len)
    x1 = _merge(x, oa, ob, ga, gb, wa, wb, wo, tm)
    idx_t, gate_t, h2 = _route(x1, norm_ffn, wq_t, k1, k2, tm)
    y = None
    if n_sc < x.shape[0]:
        y = _experts(x1, norm_ffn, norm_final, idx_t.T.reshape(-1), gate_t, table, n_sc)
    if n_sc:
        y_sc = _sc_experts(table, idx_t[:, :n_sc].T, gate_t[:, :n_sc].T, h2)
        y = _finish(x1, y_sc, norm_final, y, tm)
    return y, k_rows, v_rows, s_fin


def kernel(x_prompt, x_sample, cache_swa_k, cache_swa_v, state_ret, norm_mix, w_in, b_gate, attn_sinks,
           w_branch_a, w_branch_b, w_out, norm_ffn, peer_w_q, peer_sub_k1, peer_sub_k2, peer_u, peer_v,
           norm_final):
    batch, seq, _ = x_prompt.shape
    dec_batch, dec_seq, _ = x_sample.shape
    assert norm_mix.shape[0] == 1, "single-layer trunk"
    table = jnp.concatenate([peer_u[0], peer_v[0]], axis=1)[:, None, :]
    table, x_prompt, x_sample = lax.optimization_barrier((table, x_prompt, x_sample))
    w = (norm_mix[0][None], w_in[0].astype(BF16), b_gate[0][None], attn_sinks[0],
         w_branch_a[0].astype(BF16), w_branch_b[0].astype(BF16), w_out[0].astype(BF16),
         norm_ffn[0][None], peer_w_q[0].T.astype(BF16), peer_sub_k1[0].astype(BF16),
         peer_sub_k2[0].astype(BF16), table, norm_final[None])

    def swa_prompt(qa, ka, va, sinks):
        streams = qa.shape[0] // seq
        tq = 2 * CHUNK
        negdist, i, r = _attention_masks(tq)
        lq, lk = i // CHUNK, r // CHUNK
        band = (lk >= lq) & (lk <= lq + 2)
        allow = jnp.stack([band & (r >= WINDOW), band]).astype(F32)
        oa = _swa(sinks, qa, ka, va, negdist, allow, streams, seq // tq, tq)
        tail = lambda a: a.reshape(streams, seq, N_KV_A, HEAD_DIM_A)[:, seq - WINDOW:]
        return oa, tail(ka), tail(va)

    def swa_sample(qa, ka, va, sinks):
        negdist, i, r = _attention_masks(dec_seq)
        visible = jnp.broadcast_to(r < WINDOW + dec_seq, (dec_seq, ATT_KEYS))
        allow = jnp.stack([visible, visible]).astype(F32)
        pad = jnp.zeros((dec_batch, ATT_KEYS - WINDOW - dec_seq, KV_A), F32)
        k_all = jnp.concatenate([cache_swa_k[0].reshape(dec_batch, WINDOW, KV_A),
                                 ka.reshape(dec_batch, dec_seq, KV_A), pad], axis=1)
        v_all = jnp.concatenate([cache_swa_v[0].reshape(dec_batch, WINDOW, KV_A),
                                 va.reshape(dec_batch, dec_seq, KV_A), pad], axis=1)
        oa = _swa(sinks, qa, k_all.reshape(-1, KV_A), v_all.reshape(-1, KV_A), negdist, allow,
                  dec_batch, 1, dec_seq)
        tail = lambda a: a[:, dec_seq:WINDOW + dec_seq].reshape(dec_batch, WINDOW, N_KV_A, HEAD_DIM_A)
        return oa, tail(k_all), tail(v_all)

    ys, ks, vs, ss = _layer(x_sample.reshape(dec_batch * dec_seq, D_MODEL), dec_batch, dec_seq, swa_sample,
                            state_ret[0], dec_seq, w, 128, 0)

    per_group = batch // len(SC_GROUP_TOKENS)
    outs = []
    for g, n_sc in enumerate(SC_GROUP_TOKENS):
        xg = x_prompt[g * per_group:(g + 1) * per_group].reshape(per_group * seq, D_MODEL)
        s0 = jnp.zeros((per_group, N_HEADS_B, DK_B, DV_B), F32)
        outs.append(_layer(xg, per_group, seq, swa_prompt, s0, 256, w, 256, n_sc))
    yp, kp, vp, sp = (jnp.concatenate(parts, axis=0) for parts in zip(*outs))

    return (yp.reshape(batch, seq, D_MODEL), ys.reshape(dec_batch, dec_seq, D_MODEL),
            kp[None], vp[None], sp[None], ks[None], vs[None], ss[None])
```

```python
import functools

import jax
import jax.numpy as jnp
from jax import lax
from jax.experimental import pallas as pl
from jax.experimental.pallas import tpu as pltpu
from jax.experimental.pallas import tpu_sc as plsc

F32 = jnp.float32
BF16 = jnp.bfloat16
I32 = jnp.int32

D_MODEL = 1024
CHUNK = 64
EPS = 1e-6
NEG_INF = -1e30
PAST_LEN = 2048

N_HEADS_A = 8
N_KV_A = 2
GROUP_A = N_HEADS_A // N_KV_A
HEAD_DIM_A = 64
WINDOW = 128
N_HEADS_B = 4
DK_B = 128
DV_B = 256
Q_A = N_HEADS_A * HEAD_DIM_A
KV_A = N_KV_A * HEAD_DIM_A
Q_B = N_HEADS_B * DK_B
V_B = N_HEADS_B * DV_B
D_IN = Q_A + 2 * KV_A + 2 * Q_B + 2 * V_B + 2 * D_MODEL
OFF_QA = 0
OFF_KA = OFF_QA + Q_A
OFF_VA = OFF_KA + KV_A
OFF_QB = OFF_VA + KV_A
OFF_KB = OFF_QB + Q_B
OFF_VB = OFF_KB + Q_B
OFF_RG = OFF_VB + V_B
OFF_GA = OFF_RG + V_B
OFF_GB = OFF_GA + D_MODEL

N_KEYS = 128
N_EXPERTS = N_KEYS * N_KEYS
PEER_HEADS = 8
PEER_QDIM = 256
PEER_HALF = PEER_QDIM // 2
PEER_TOPK = 16
PICKS = PEER_HEADS * PEER_TOPK

LANES = 128
ATT_KEYS = 2 * WINDOW
VMEM_LIMIT = 56 * 1024 * 1024


def _rmsnorm(x, g):
    return x * lax.rsqrt(jnp.mean(x * x, axis=-1, keepdims=True) + EPS) * g


def _dot(a, b):
    return jnp.dot(a, b, preferred_element_type=F32)


def _dot_nt(a, b):
    return lax.dot_general(a, b, (((1,), (1,)), ((), ())), preferred_element_type=F32)


def _dot_tn(a, b):
    return lax.dot_general(a, b, (((0,), (0,)), ((), ())), preferred_element_type=F32)


def _params(sem):
    return pltpu.CompilerParams(dimension_semantics=sem, vmem_limit_bytes=VMEM_LIMIT)


def _proj_body(x_ref, g_ref, w_ref, bg_ref, qa_ref, ka_ref, va_ref, qb_ref, kb_ref, vb_ref,
               rg_ref, ga_ref, gb_ref):
    h = _rmsnorm(x_ref[...], g_ref[...]).astype(BF16)

    def mm(lo, width):
        return _dot(h, w_ref[:, lo:lo + width])

    qa_ref[...] = (mm(OFF_QA, Q_A) * (HEAD_DIM_A ** -0.5)).astype(BF16)
    ka_ref[...] = mm(OFF_KA, KV_A)
    va_ref[...] = mm(OFF_VA, KV_A)
    qb_ref[...] = mm(OFF_QB, Q_B).astype(BF16)
    kb_ref[...] = (mm(OFF_KB, Q_B) * (DK_B ** -0.5)).astype(BF16)
    vb_ref[...] = mm(OFF_VB, V_B).astype(BF16)
    rg_ref[...] = mm(OFF_RG, V_B)
    ga_ref[...] = mm(OFF_GA, D_MODEL) + bg_ref[:, :D_MODEL]
    gb_ref[...] = mm(OFF_GB, D_MODEL) + bg_ref[:, D_MODEL:]


def _project(x, g, w_bf16, b_gate, tm):
    n = x.shape[0]
    row = lambda width: pl.BlockSpec((tm, width), lambda i: (i, 0))
    full = lambda a: pl.BlockSpec(a.shape, lambda i: (0, 0))
    widths = (Q_A, KV_A, KV_A, Q_B, Q_B, V_B, V_B, D_MODEL, D_MODEL)
    dtypes = (BF16, F32, F32, BF16, BF16, BF16, F32, F32, F32)
    return pl.pallas_call(
        _proj_body,
        grid=(n // tm,),
        in_specs=[row(D_MODEL), full(g), full(w_bf16), full(b_gate)],
        out_specs=[row(w) for w in widths],
        out_shape=[jax.ShapeDtypeStruct((n, w), d) for w, d in zip(widths, dtypes)],
        compiler_params=_params(("parallel",)),
        name="projection",
    )(x, g, w_bf16, b_gate)


def _swa_body(sink_ref, q_ref, kp_ref, kc_ref, vp_ref, vc_ref, nd_ref, al_ref, o_ref):
    k = jnp.concatenate([kp_ref[...], kc_ref[...]], axis=0)
    v = jnp.concatenate([vp_ref[...], vc_ref[...]], axis=0)
    lane = lax.broadcasted_iota(I32, k.shape, 1)
    low = lane < HEAD_DIM_A
    k_sw = pltpu.roll(k, HEAD_DIM_A, axis=1)
    v_sw = pltpu.roll(v, HEAD_DIM_A, axis=1)
    negdist = nd_ref[...]
    allowed = al_ref[0] > 0.5
    for kv in range(N_KV_A):
        own = low if kv == 0 else jnp.logical_not(low)
        k_rep = jnp.where(own, k, k_sw)
        v_rep = jnp.where(own, v, v_sw)
        zero = jnp.zeros_like(k_rep)
        kk = jnp.concatenate([jnp.where(low, k_rep, zero), jnp.where(low, zero, k_rep)], axis=0).astype(BF16)
        vv = jnp.concatenate([jnp.where(low, v_rep, zero), jnp.where(low, zero, v_rep)], axis=0).astype(BF16)
        for a in range(GROUP_A // 2):
            col = kv * GROUP_A * HEAD_DIM_A + a * LANES
            s2 = _dot_nt(q_ref[:, col:col + LANES], kk)
            ps = []
            for u in range(2):
                head = kv * GROUP_A + 2 * a + u
                slope = 2.0 ** (-8.0 * (head + 1) / N_HEADS_A)
                sink = sink_ref[head]
                s = s2[:, u * ATT_KEYS:(u + 1) * ATT_KEYS] + slope * negdist
                s = jnp.where(allowed, s, NEG_INF)
                m = jnp.maximum(jnp.max(s, axis=-1, keepdims=True), sink)
                p = jnp.exp(s - m)
                denom = jnp.sum(p, axis=-1, keepdims=True) + jnp.exp(sink - m)
                ps.append((p / denom).astype(BF16))
            o = _dot(jnp.concatenate(ps, axis=1), vv)
            o_ref[:, col:col + LANES] = o.astype(BF16)


def _swa(sinks, q, k, v, negdist, allow, groups, q_blocks, tq):
    n = q.shape[0]
    kblocks = k.shape[0] // LANES // groups
    shift = kblocks - q_blocks
    prev = lambda b, j: (b * kblocks + jnp.maximum(j + shift - 1, 0), 0)
    cur = lambda b, j: (b * kblocks + j + shift, 0)
    kv_spec = lambda f: pl.BlockSpec((LANES, KV_A), f)
    return pl.pallas_call(
        _swa_body,
        grid=(groups, q_blocks),
        in_specs=[pl.BlockSpec(memory_space=pltpu.SMEM),
                  pl.BlockSpec((tq, Q_A), lambda b, j: (b * q_blocks + j, 0)),
                  kv_spec(prev), kv_spec(cur), kv_spec(prev), kv_spec(cur),
                  pl.BlockSpec((tq, ATT_KEYS), lambda b, j: (0, 0)),
                  pl.BlockSpec((1, tq, ATT_KEYS), lambda b, j: (jnp.minimum(j + shift, 1), 0, 0))],
        out_specs=pl.BlockSpec((tq, Q_A), lambda b, j: (b * q_blocks + j, 0)),
        out_shape=jax.ShapeDtypeStruct((n, Q_A), BF16),
        compiler_params=_params(("parallel", "parallel")),
        name="attention",
    )(sinks, q, k, k, v, v, negdist, allow)


def _ret_body(q_ref, k_ref, v_ref, rg_ref, s0_ref, intra_ref, rdec_ref, kdec_ref, sdec_ref,
              o_ref, st_ref):
    c = pl.program_id(1)

    @pl.when(c == 0)
    def _():
        st_ref[...] = s0_ref[...]

    for h in range(N_HEADS_B):
        q = q_ref[:, h * DK_B:(h + 1) * DK_B]
        k = k_ref[:, h * DK_B:(h + 1) * DK_B]
        v = v_ref[:, h * DV_B:(h + 1) * DV_B]
        st = st_ref[0, h]
        scores = (_dot_nt(q, k) * intra_ref[h]).astype(BF16)
        o = _dot(scores, v) + _dot(q, st.astype(BF16)) * rdec_ref[h]
        k_dec = (k.astype(F32) * kdec_ref[h]).astype(BF16)
        st_ref[0, h] = sdec_ref[h] * st + _dot_tn(k_dec, v)
        mu = jnp.mean(o, axis=-1, keepdims=True)
        d = o - mu
        var = jnp.mean(d * d, axis=-1, keepdims=True)
        on = d * lax.rsqrt(var + EPS)
        r = rg_ref[:, h * DV_B:(h + 1) * DV_B]
        o_ref[:, h * DV_B:(h + 1) * DV_B] = (on * (r * jax.nn.sigmoid(r))).astype(BF16)


def _retention_consts(length):
    lg = jnp.log1p(-(2.0 ** (-5.0 - jnp.arange(N_HEADS_B, dtype=F32))))
    n = jnp.arange(length, dtype=F32)
    diff = n[:, None] - n[None, :]
    intra = jnp.where(diff >= 0, jnp.exp(jnp.maximum(diff, 0.0) * lg[:, None, None]), 0.0)
    rdec = jnp.exp((n[None, :] + 1.0) * lg[:, None])[..., None]
    kdec = jnp.exp((length - 1.0 - n)[None, :] * lg[:, None])[..., None]
    sdec = jnp.exp(length * lg)
    return intra, rdec, kdec, sdec


def _retention(q, k, v, rg, s0, streams, chunks, length):
    n = q.shape[0]
    intra, rdec, kdec, sdec = _retention_consts(length)
    row = lambda width: pl.BlockSpec((length, width), lambda b, c: (b * chunks + c, 0))
    const = lambda a: pl.BlockSpec(a.shape, lambda b, c: (0,) * a.ndim)
    st_spec = pl.BlockSpec((1, N_HEADS_B, DK_B, DV_B), lambda b, c: (b, 0, 0, 0))
    return pl.pallas_call(
        _ret_body,
        grid=(streams, chunks),
        in_specs=[row(Q_B), row(Q_B), row(V_B), row(V_B), st_spec,
                  const(intra), const(rdec), const(kdec), pl.BlockSpec(memory_space=pltpu.SMEM)],
        out_specs=[row(V_B), st_spec],
        out_shape=[jax.ShapeDtypeStruct((n, V_B), BF16),
                   jax.ShapeDtypeStruct((streams, N_HEADS_B, DK_B, DV_B), F32)],
        compiler_params=_params(("parallel", "arbitrary")),
        name="retention",
    )(q, k, v, rg, s0, intra, rdec, kdec, sdec)


def _merge_body(x_ref, oa_ref, ob_ref, ga_ref, gb_ref, wa_ref, wb_ref, wo_ref, o_ref):
    ya = _dot(oa_ref[...], wa_ref[...])
    yb = _dot(ob_ref[...], wb_ref[...])
    mix = jax.nn.sigmoid(ga_ref[...]) * ya + jax.nn.sigmoid(gb_ref[...]) * yb
    o_ref[...] = x_ref[...] + _dot(mix.astype(BF16), wo_ref[...])


def _merge(x, oa, ob, ga, gb, wa, wb, wo, tm):
    n = x.shape[0]
    row = lambda width: pl.BlockSpec((tm, width), lambda i: (i, 0))
    full = lambda a: pl.BlockSpec(a.shape, lambda i: (0, 0))
    return pl.pallas_call(
        _merge_body,
        grid=(n // tm,),
        in_specs=[row(D_MODEL), row(Q_A), row(V_B), row(D_MODEL), row(D_MODEL), full(wa), full(wb), full(wo)],
        out_specs=row(D_MODEL),
        out_shape=jax.ShapeDtypeStruct((n, D_MODEL), F32),
        compiler_params=_params(("parallel",)),
        name="merge",
    )(x, oa, ob, ga, gb, wa, wb, wo)


def _top16(s, order):
    t = s.shape[1]
    rank = lax.broadcasted_iota(I32, (PEER_TOPK, t), 0)
    vals = jnp.zeros((PEER_TOPK, t), F32)
    ids = jnp.zeros((PEER_TOPK, t), F32)
    for r in range(PEER_TOPK):
        m = jnp.max(s, axis=0, keepdims=True)
        pick = jnp.min(jnp.where(s == m, order, 3e38), axis=0, keepdims=True)
        s = jnp.where(order == pick, -jnp.inf, s)
        vals = jnp.where(rank == r, m, vals)
        ids = jnp.where(rank == r, pick, ids)
    return vals, ids


def _route_body(x_ref, g_ref, wq_ref, k1_ref, k2_ref, idx_ref, gate_ref, h_ref):
    h_ref[...] = _rmsnorm(x_ref[...], g_ref[...])
    h = h_ref[...].astype(BF16)
    t = h.shape[0]
    q_t = _dot_nt(wq_ref[...], h).astype(BF16)
    iota = lambda rows: lax.broadcasted_iota(I32, (rows, t), 0).astype(F32)
    key_id = iota(N_KEYS)
    flat = jnp.concatenate(
        [iota(PEER_TOPK)] + [a * PEER_TOPK + iota(8) for a in range(1, 8)] + [(8 + iota(8)) * PEER_TOPK],
        axis=0)
    rank = lax.broadcasted_iota(I32, (PEER_TOPK, t), 0)
    idx_rows, gate_rows = [], []
    for hd in range(PEER_HEADS):
        base = hd * PEER_QDIM
        s1 = _dot(k1_ref[...], q_t[base:base + PEER_HALF])
        s2 = _dot(k2_ref[...], q_t[base + PEER_HALF:base + PEER_QDIM])
        v1, i1 = _top16(s1, key_id)
        v2, i2 = _top16(s2, key_id)
        cand = jnp.concatenate(
            [v1[0:1] + v2]
            + [v1[a:a + 1] + v2[0:8] for a in range(1, 8)]
            + [v1[8:16] + v2[0:1]], axis=0)
        cidx = jnp.concatenate(
            [i1[0:1] * N_KEYS + i2]
            + [i1[a:a + 1] * N_KEYS + i2[0:8] for a in range(1, 8)]
            + [i1[8:16] * N_KEYS + i2[0:1]], axis=0)
        best, pos = _top16(cand, flat)
        eidx = jnp.zeros((PEER_TOPK, t), F32)
        for r in range(PEER_TOPK):
            picked = jnp.max(jnp.where(flat == pos[r:r + 1], cidx, -1.0), axis=0, keepdims=True)
            eidx = jnp.where(rank == r, picked, eidx)
        e = jnp.exp(best - best[0:1])
        gate_rows.append(e / jnp.sum(e, axis=0, keepdims=True))
        idx_rows.append(eidx.astype(I32))
    idx_ref[...] = jnp.concatenate(idx_rows, axis=0)
    gate_ref[...] = jnp.concatenate(gate_rows, axis=0)


def _route(x, g, wq_t, k1, k2, tm):
    n = x.shape[0]
    full = lambda a: pl.BlockSpec(a.shape, lambda i: (0, 0))
    col = pl.BlockSpec((PICKS, tm), lambda i: (0, i))
    return pl.pallas_call(
        _route_body,
        grid=(n // tm,),
        in_specs=[pl.BlockSpec((tm, D_MODEL), lambda i: (i, 0)), full(g), full(wq_t), full(k1), full(k2)],
        out_specs=[col, col, pl.BlockSpec((tm, D_MODEL), lambda i: (i, 0))],
        out_shape=[jax.ShapeDtypeStruct((PICKS, n), I32), jax.ShapeDtypeStruct((PICKS, n), F32),
                   jax.ShapeDtypeStruct((n, D_MODEL), F32)],
        compiler_params=_params(("parallel",)),
        name="routing",
    )(x, g, wq_t, k1, k2)


SC_CORES = 2
SC_SUBCORES = 16
SC_LANES = 16
SC_CHUNK = 16
SC_GROUP_BOUNDS = ((0, 2560, 16128), (0, 1024))


def _gelu_via_exp(x):
    z = 0.7978845608028654 * (x + 0.044715 * x * x * x)
    return 0.5 * x * (2.0 - 2.0 / (1.0 + jnp.exp(2.0 * z)))


def _sc_experts(table, idx, gate, h, first):
    n = idx.shape[0]
    workers = SC_CORES * SC_SUBCORES
    per_worker = n // workers
    n_chunks = PICKS // SC_CHUNK
    steps = D_MODEL // SC_LANES
    mesh = plsc.VectorSubcoreMesh(core_axis_name="c", subcore_axis_name="s",
                                  num_cores=SC_CORES, num_subcores=SC_SUBCORES)

    def body(tab_hbm, idx_hbm, gate_hbm, h_hbm, y_hbm,
             idx_0, idx_1, gate_0, gate_1, h_0, h_1, y_0, y_1, rows_a, rows_b,
             in_sem_0, in_sem_1, y_sem_0, y_sem_1, sem_a, sem_b):
        base = (lax.axis_index("s") * SC_CORES + lax.axis_index("c")) * per_worker
        last = base + per_worker - 1
        sets = ((idx_0, gate_0, h_0, y_0, in_sem_0, y_sem_0), (idx_1, gate_1, h_1, y_1, in_sem_1, y_sem_1))
        bufs = ((rows_a, sem_a), (rows_b, sem_b))

        def in_copies(t, q):
            idx_v, gate_v, h_v, _, sem, _ = sets[q]
            return (pltpu.make_async_copy(idx_hbm.at[t], idx_v, sem),
                    pltpu.make_async_copy(gate_hbm.at[t], gate_v, sem),
                    pltpu.make_async_copy(h_hbm.at[t + first], h_v, sem))

        def y_copy(t, q):
            return pltpu.make_async_copy(sets[q][3], y_hbm.at[t], sets[q][5])

        def gather(c, b, q):
            rows, sem = bufs[b]
            return pltpu.make_async_copy(tab_hbm.at[sets[q][0].at[pl.ds(c * SC_CHUNK, SC_CHUNK)]], rows, sem)

        def compute(c, b, q):
            rows, _ = bufs[b]
            _, gate_v, h_v, y_v, _, _ = sets[q]

            def dot_step(j, accs):
                hj = h_v[pl.ds(j * SC_LANES, SC_LANES)]
                return tuple(accs[k] + rows[k, 0, pl.ds(j * SC_LANES, SC_LANES)] * hj for k in range(SC_CHUNK))

            accs = lax.fori_loop(0, steps, dot_step,
                                 tuple(jnp.zeros((SC_LANES,), F32) for _ in range(SC_CHUNK)))
            ws = []
            for k in range(SC_CHUNK):
                s = jnp.full((SC_LANES,), jnp.sum(accs[k]), F32)
                g = plsc.load_gather(gate_v, [jnp.full((SC_LANES,), c * SC_CHUNK + k, I32)])
                ws.append(_gelu_via_exp(s) * g)

            def sum_step(j, carry):
                acc = y_v[pl.ds(j * SC_LANES, SC_LANES)]
                for k in range(SC_CHUNK):
                    acc = acc + ws[k] * rows[k, 0, pl.ds(D_MODEL + j * SC_LANES, SC_LANES)]
                y_v[pl.ds(j * SC_LANES, SC_LANES)] = acc
                return carry

            lax.fori_loop(0, steps, sum_step, 0)

        def token(t, q, first):
            nxt = jnp.minimum(t + 1, last)
            for cp in in_copies(nxt, 1 - q):
                cp.start()
            y_v = sets[q][3]

            @pl.when(jnp.logical_not(first))
            def _():
                y_copy(t, q).wait()

            @pl.loop(0, steps)
            def _(j):
                y_v[pl.ds(j * SC_LANES, SC_LANES)] = jnp.zeros((SC_LANES,), F32)

            @pl.loop(0, n_chunks // 2)
            def _(p):
                c0 = p * 2
                gather(c0 + 1, 1, q).start()
                gather(c0, 0, q).wait()
                compute(c0, 0, q)

                @pl.when(p + 1 < n_chunks // 2)
                def _():
                    gather(c0 + 2, 0, q).start()

                gather(c0 + 1, 1, q).wait()
                compute(c0 + 1, 1, q)

            for cp in in_copies(nxt, 1 - q):
                cp.wait()
            gather(0, 0, 1 - q).start()
            y_copy(t, q).start()

        for cp in in_copies(base, 0):
            cp.start()
        for cp in in_copies(base, 0):
            cp.wait()
        gather(0, 0, 0).start()

        @pl.loop(0, per_worker // 2)
        def _(i):
            token(base + 2 * i, 0, i == 0)
            token(base + 2 * i + 1, 1, i == 0)

        gather(0, 0, 0).wait()
        y_copy(last - 1, 0).wait()
        y_copy(last, 1).wait()

    vec = lambda n_el, dt: [pltpu.VMEM((n_el,), dt), pltpu.VMEM((n_el,), dt)]
    return pl.kernel(
        body, out_type=jax.ShapeDtypeStruct((n, D_MODEL), F32), mesh=mesh,
        scratch_types=vec(PICKS, I32) + vec(PICKS, F32) + vec(D_MODEL, F32) + vec(D_MODEL, F32)
        + [pltpu.VMEM((SC_CHUNK, 1, 2 * D_MODEL), F32), pltpu.VMEM((SC_CHUNK, 1, 2 * D_MODEL), F32)]
        + [pltpu.SemaphoreType.DMA] * 6,
        compiler_params=pltpu.CompilerParams(needs_layout_passes=False),
        cost_estimate=pl.CostEstimate(flops=4 * n * PICKS * D_MODEL, transcendentals=n * PICKS,
                                      bytes_accessed=4 * n * (PICKS * 2 * D_MODEL + 2 * PICKS + 2 * D_MODEL)),
        name="experts_sc",
    )(table, idx, gate, h)


def _finish_body(x_ref, y_ref, gf_ref, *rest_and_out):
    rest_and_out[-1][...] = _rmsnorm(x_ref[...] + y_ref[...], gf_ref[...])


def _finish(x, y, g_final, rest, tm, first_row=0):
    first_block = first_row // tm
    row = pl.BlockSpec((tm, D_MODEL), lambda i: (i + first_block, 0))
    extra = () if rest is None else (rest,)
    return pl.pallas_call(
        _finish_body,
        grid=(y.shape[0] // tm,),
        in_specs=[row, pl.BlockSpec((tm, D_MODEL), lambda i: (i, 0)), pl.BlockSpec(g_final.shape, lambda i: (0, 0))]
        + [pl.BlockSpec(memory_space=pl.ANY) for _ in extra],
        out_specs=row,
        out_shape=jax.ShapeDtypeStruct(x.shape if rest is not None else y.shape, F32),
        input_output_aliases={3: 0} if rest is not None else {},
        compiler_params=_params(("parallel",)),
        name="finish",
    )(x, y, g_final, *extra)


EXP_TOKENS = 256
EXP_SLOTS = 4


def _expert_body(x_ref, g_ref, gf_ref, idx_hbm, gate_ref, tab_hbm, o_ref,
                 idx_smem, buf0, buf1, buf2, buf3, h_buf, y_buf, sems, idx_sem, *, first_block):
    bufs = (buf0, buf1, buf2, buf3)
    ahead = EXP_SLOTS - 1
    i = pl.program_id(0) + first_block
    idx_copy = pltpu.make_async_copy(idx_hbm.at[pl.ds(i * (EXP_TOKENS * PICKS), EXP_TOKENS * PICKS)],
                                     idx_smem, idx_sem)
    idx_copy.start()
    h_buf[...] = _rmsnorm(x_ref[...], g_ref[...])
    idx_copy.wait()

    def issue(t, slot):
        for k in range(PICKS):
            e = idx_smem[t * PICKS + k]
            pltpu.async_copy(tab_hbm.at[e], bufs[slot].at[pl.ds(k, 1), :], sems.at[slot], priority=k % 2)

    def wait_slot(slot):
        pltpu.make_async_copy(bufs[(slot + 1) % EXP_SLOTS], bufs[slot], sems.at[slot]).wait()

    lane = lax.broadcasted_iota(I32, (PICKS, EXP_TOKENS), 1)

    def compute(t, slot):
        h = h_buf[pl.ds(t, 1), :]
        u = bufs[slot][:, :D_MODEL]
        s = jnp.sum(u * h, axis=-1, keepdims=True)
        gate = jnp.sum(jnp.where(lane == t, gate_ref[...], 0.0), axis=-1, keepdims=True)
        w = gate * jax.nn.gelu(s)
        v = bufs[slot][:, D_MODEL:]
        y_buf[pl.ds(t, 1), :] = jnp.sum(w * v, axis=0, keepdims=True)

    def group(t0, issue_upto):
        for s in range(EXP_SLOTS):
            wait_slot(s)
            if s < issue_upto:
                issue(t0 + s + ahead, (s + ahead) % EXP_SLOTS)
            compute(t0 + s, s)

    for s in range(ahead):
        issue(s, s)

    def steady(q, carry):
        group(q * EXP_SLOTS, EXP_SLOTS)
        return carry

    n_groups = EXP_TOKENS // EXP_SLOTS
    lax.fori_loop(0, n_groups - 1, steady, 0)
    group((n_groups - 1) * EXP_SLOTS, EXP_SLOTS - ahead)
    o_ref[...] = _rmsnorm(x_ref[...] + y_buf[...], gf_ref[...])


def _experts(x, g, g_final, idx_flat, gate_t, table, first_token):
    n = x.shape[0]
    first_block = first_token // EXP_TOKENS
    full = lambda a: pl.BlockSpec(a.shape, lambda i: (0, 0))
    row = pl.BlockSpec((EXP_TOKENS, D_MODEL), lambda i: (i + first_block, 0))
    return pl.pallas_call(
        functools.partial(_expert_body, first_block=first_block),
        grid=(n // EXP_TOKENS - first_block,),
        in_specs=[row, full(g), full(g_final),
                  pl.BlockSpec(memory_space=pl.ANY),
                  pl.BlockSpec((PICKS, EXP_TOKENS), lambda i: (0, i + first_block)),
                  pl.BlockSpec(memory_space=pl.ANY)],
        out_specs=row,
        out_shape=jax.ShapeDtypeStruct((n, D_MODEL), F32),
        scratch_shapes=[pltpu.SMEM((EXP_TOKENS * PICKS,), I32)]
        + [pltpu.VMEM((PICKS, 2 * D_MODEL), F32) for _ in range(EXP_SLOTS)]
        + [pltpu.VMEM((EXP_TOKENS, D_MODEL), F32),
           pltpu.VMEM((EXP_TOKENS, D_MODEL), F32),
           pltpu.SemaphoreType.DMA((EXP_SLOTS,)),
           pltpu.SemaphoreType.DMA(())],
        compiler_params=pltpu.CompilerParams(dimension_semantics=("arbitrary",), vmem_limit_bytes=VMEM_LIMIT,
                                             disable_bounds_checks=True),
        name="experts",
    )(x, g, g_final, idx_flat, gate_t, table)


def _attention_masks(tq):
    i = jnp.arange(tq, dtype=I32)[:, None]
    r = jnp.arange(ATT_KEYS, dtype=I32)[None, :]
    negdist = -jnp.abs(WINDOW + i - r).astype(F32)
    return negdist, i, r


def _layer(x, n_groups, rows, swa_inputs, ret_state, ret_len, w, tm, sc_bounds=(0,), tie=None):
    (norm_mix, w_in, b_gate, sinks, wa, wb, wo, norm_ffn, wq_t, k1, k2, table, norm_final) = w
    assert sc_bounds[-1] < x.shape[0]
    qa, ka, va, qb, kb, vb, rg, ga, gb = _project(x, norm_mix, w_in, b_gate, tm)
    oa, k_rows, v_rows = swa_inputs(qa, ka, va, sinks)
    ob, s_fin = _retention(qb, kb, vb, rg, ret_state, n_groups, rows // ret_len, ret_len)
    x1 = _merge(x, oa, ob, ga, gb, wa, wb, wo, tm)
    idx_t, gate_t, h2 = _route(x1, norm_ffn, wq_t, k1, k2, tm)
    y_sc = [(lo, _sc_experts(table, idx_t[:, lo:hi].T, gate_t[:, lo:hi].T, h2, lo))
            for lo, hi in zip(sc_bounds[:-1], sc_bounds[1:])]
    if tie is not None:
        tie, gate_t = lax.optimization_barrier((tie, gate_t))
    y_tc = _experts(x1, norm_ffn, norm_final, idx_t.T.reshape(-1), gate_t, table, sc_bounds[-1])
    return (x1, y_sc, y_tc), k_rows, v_rows, s_fin, tie


def _layer_output(parts, norm_final, tm):
    x1, y_sc, y = parts
    for first_row, y_part in y_sc:
        y = _finish(x1, y_part, norm_final, y, tm, first_row)
    return y


def kernel(x_prompt, x_sample, cache_swa_k, cache_swa_v, state_ret, norm_mix, w_in, b_gate, attn_sinks,
           w_branch_a, w_branch_b, w_out, norm_ffn, peer_w_q, peer_sub_k1, peer_sub_k2, peer_u, peer_v,
           norm_final):
    batch, seq, _ = x_prompt.shape
    dec_batch, dec_seq, _ = x_sample.shape
    assert norm_mix.shape[0] == 1, "single-layer trunk"
    table = jnp.concatenate([peer_u[0], peer_v[0]], axis=1)[:, None, :]
    table, x_prompt, x_sample = lax.optimization_barrier((table, x_prompt, x_sample))
    w = (norm_mix[0][None], w_in[0].astype(BF16), b_gate[0][None], attn_sinks[0],
         w_branch_a[0].astype(BF16), w_branch_b[0].astype(BF16), w_out[0].astype(BF16),
         norm_ffn[0][None], peer_w_q[0].T.astype(BF16), peer_sub_k1[0].astype(BF16),
         peer_sub_k2[0].astype(BF16), table, norm_final[None])

    def swa_prompt(qa, ka, va, sinks):
        streams = qa.shape[0] // seq
        tq = 2 * CHUNK
        negdist, i, r = _attention_masks(tq)
        lq, lk = i // CHUNK, r // CHUNK
        band = (lk >= lq) & (lk <= lq + 2)
        allow = jnp.stack([band & (r >= WINDOW), band]).astype(F32)
        oa = _swa(sinks, qa, ka, va, negdist, allow, streams, seq // tq, tq)
        tail = lambda a: a.reshape(streams, seq, N_KV_A, HEAD_DIM_A)[:, seq - WINDOW:]
        return oa, tail(ka), tail(va)

    def swa_sample(qa, ka, va, sinks):
        negdist, i, r = _attention_masks(dec_seq)
        visible = jnp.broadcast_to(r < WINDOW + dec_seq, (dec_seq, ATT_KEYS))
        allow = jnp.stack([visible, visible]).astype(F32)
        pad = jnp.zeros((dec_batch, ATT_KEYS - WINDOW - dec_seq, KV_A), F32)
        k_all = jnp.concatenate([cache_swa_k[0].reshape(dec_batch, WINDOW, KV_A),
                                 ka.reshape(dec_batch, dec_seq, KV_A), pad], axis=1)
        v_all = jnp.concatenate([cache_swa_v[0].reshape(dec_batch, WINDOW, KV_A),
                                 va.reshape(dec_batch, dec_seq, KV_A), pad], axis=1)
        oa = _swa(sinks, qa, k_all.reshape(-1, KV_A), v_all.reshape(-1, KV_A), negdist, allow,
                  dec_batch, 1, dec_seq)
        tail = lambda a: a[:, dec_seq:WINDOW + dec_seq].reshape(dec_batch, WINDOW, N_KV_A, HEAD_DIM_A)
        return oa, tail(k_all), tail(v_all)

    parts, ks, vs, ss, _ = _layer(x_sample.reshape(dec_batch * dec_seq, D_MODEL), dec_batch, dec_seq,
                                  swa_sample, state_ret[0], dec_seq, w, 128)
    ys = _layer_output(parts, w[-1], 128)

    n_grp = len(SC_GROUP_BOUNDS)
    per_group = batch // n_grp
    xs = [x_prompt[g * per_group:(g + 1) * per_group].reshape(per_group * seq, D_MODEL) for g in range(n_grp)]
    s0 = jnp.zeros((per_group, N_HEADS_B, DK_B, DV_B), F32)
    outs, tie = [], None
    for g, bounds in enumerate(SC_GROUP_BOUNDS):
        (x1, y_sc, y_tc), kg, vg, sg, tie = _layer(xs[g], per_group, seq, swa_prompt, s0, 256, w, 256, bounds, tie)
        if tie is not None:
            (px1, py_sc, py_tc), pk, pv, ps = outs[-1]
            outs[-1] = ((px1, [(py_sc[0][0], tie)] + py_sc[1:], py_tc), pk, pv, ps)
            tie = None
        if g + 1 < n_grp:
            y_tc, xs[g + 1] = lax.optimization_barrier((y_tc, xs[g + 1]))
            tie = y_sc[0][1]
        outs.append(((x1, y_sc, y_tc), kg, vg, sg))
    outs = [(_layer_output(p, w[-1], 256), kg, vg, sg) for p, kg, vg, sg in outs]
    yp, kp, vp, sp = (jnp.concatenate(parts, axis=0) for parts in zip(*outs))

    return (yp.reshape(batch, seq, D_MODEL), ys.reshape(dec_batch, dec_seq, D_MODEL),
            kp[None], vp[None], sp[None], ks[None], vs[None], ss[None])
```

```python
import functools

import jax
import jax.numpy as jnp
from jax import lax
from jax.experimental import pallas as pl
from jax.experimental.pallas import tpu as pltpu
from jax.experimental.pallas import tpu_sc as plsc

F32 = jnp.float32
BF16 = jnp.bfloat16
I32 = jnp.int32

D_MODEL = 1024
CHUNK = 64
EPS = 1e-6
NEG_INF = -1e30
PAST_LEN = 2048

N_HEADS_A = 8
N_KV_A = 2
GROUP_A = N_HEADS_A // N_KV_A
HEAD_DIM_A = 64
WINDOW = 128
N_HEADS_B = 4
DK_B = 128
DV_B = 256
Q_A = N_HEADS_A * HEAD_DIM_A
KV_A = N_KV_A * HEAD_DIM_A
Q_B = N_HEADS_B * DK_B
V_B = N_HEADS_B * DV_B
D_IN = Q_A + 2 * KV_A + 2 * Q_B + 2 * V_B + 2 * D_MODEL
OFF_QA = 0
OFF_KA = OFF_QA + Q_A
OFF_VA = OFF_KA + KV_A
OFF_QB = OFF_VA + KV_A
OFF_KB = OFF_QB + Q_B
OFF_VB = OFF_KB + Q_B
OFF_RG = OFF_VB + V_B
OFF_GA = OFF_RG + V_B
OFF_GB = OFF_GA + D_MODEL

N_KEYS = 128
N_EXPERTS = N_KEYS * N_KEYS
PEER_HEADS = 8
PEER_QDIM = 256
PEER_HALF = PEER_QDIM // 2
PEER_TOPK = 16
PICKS = PEER_HEADS * PEER_TOPK

LANES = 128
ATT_KEYS = 2 * WINDOW
VMEM_LIMIT = 56 * 1024 * 1024


def _rmsnorm(x, g):
    return x * lax.rsqrt(jnp.mean(x * x, axis=-1, keepdims=True) + EPS) * g


def _dot(a, b):
    return jnp.dot(a, b, preferred_element_type=F32)


def _dot_nt(a, b):
    return lax.dot_general(a, b, (((1,), (1,)), ((), ())), preferred_element_type=F32)


def _dot_tn(a, b):
    return lax.dot_general(a, b, (((0,), (0,)), ((), ())), preferred_element_type=F32)


def _params(sem):
    return pltpu.CompilerParams(dimension_semantics=sem, vmem_limit_bytes=VMEM_LIMIT)


def _proj_body(x_ref, g_ref, w_ref, bg_ref, qa_ref, ka_ref, va_ref, qb_ref, kb_ref, vb_ref,
               rg_ref, ga_ref, gb_ref):
    h = _rmsnorm(x_ref[...], g_ref[...]).astype(BF16)

    def mm(lo, width):
        return _dot(h, w_ref[:, lo:lo + width])

    qa_ref[...] = (mm(OFF_QA, Q_A) * (HEAD_DIM_A ** -0.5)).astype(BF16)
    ka_ref[...] = mm(OFF_KA, KV_A)
    va_ref[...] = mm(OFF_VA, KV_A)
    qb_ref[...] = mm(OFF_QB, Q_B).astype(BF16)
    kb_ref[...] = (mm(OFF_KB, Q_B) * (DK_B ** -0.5)).astype(BF16)
    vb_ref[...] = mm(OFF_VB, V_B).astype(BF16)
    rg_ref[...] = mm(OFF_RG, V_B)
    ga_ref[...] = mm(OFF_GA, D_MODEL) + bg_ref[:, :D_MODEL]
    gb_ref[...] = mm(OFF_GB, D_MODEL) + bg_ref[:, D_MODEL:]


def _project(x, g, w_bf16, b_gate, tm):
    n = x.shape[0]
    row = lambda width: pl.BlockSpec((tm, width), lambda i: (i, 0))
    full = lambda a: pl.BlockSpec(a.shape, lambda i: (0, 0))
    widths = (Q_A, KV_A, KV_A, Q_B, Q_B, V_B, V_B, D_MODEL, D_MODEL)
    dtypes = (BF16, F32, F32, BF16, BF16, BF16, F32, F32, F32)
    return pl.pallas_call(
        _proj_body,
        grid=(n // tm,),
        in_specs=[row(D_MODEL), full(g), full(w_bf16), full(b_gate)],
        out_specs=[row(w) for w in widths],
        out_shape=[jax.ShapeDtypeStruct((n, w), d) for w, d in zip(widths, dtypes)],
        compiler_params=_params(("parallel",)),
        name="projection",
    )(x, g, w_bf16, b_gate)


def _swa_body(sink_ref, q_ref, kp_ref, kc_ref, vp_ref, vc_ref, nd_ref, al_ref, o_ref):
    k = jnp.concatenate([kp_ref[...], kc_ref[...]], axis=0)
    v = jnp.concatenate([vp_ref[...], vc_ref[...]], axis=0)
    lane = lax.broadcasted_iota(I32, k.shape, 1)
    low = lane < HEAD_DIM_A
    k_sw = pltpu.roll(k, HEAD_DIM_A, axis=1)
    v_sw = pltpu.roll(v, HEAD_DIM_A, axis=1)
    negdist = nd_ref[...]
    allowed = al_ref[0] > 0.5
    for kv in range(N_KV_A):
        own = low if kv == 0 else jnp.logical_not(low)
        k_rep = jnp.where(own, k, k_sw)
        v_rep = jnp.where(own, v, v_sw)
        zero = jnp.zeros_like(k_rep)
        kk = jnp.concatenate([jnp.where(low, k_rep, zero), jnp.where(low, zero, k_rep)], axis=0).astype(BF16)
        vv = jnp.concatenate([jnp.where(low, v_rep, zero), jnp.where(low, zero, v_rep)], axis=0).astype(BF16)
        for a in range(GROUP_A // 2):
            col = kv * GROUP_A * HEAD_DIM_A + a * LANES
            s2 = _dot_nt(q_ref[:, col:col + LANES], kk)
            ps = []
            for u in range(2):
                head = kv * GROUP_A + 2 * a + u
                slope = 2.0 ** (-8.0 * (head + 1) / N_HEADS_A)
                sink = sink_ref[head]
                s = s2[:, u * ATT_KEYS:(u + 1) * ATT_KEYS] + slope * negdist
                s = jnp.where(allowed, s, NEG_INF)
                m = jnp.maximum(jnp.max(s, axis=-1, keepdims=True), sink)
                p = jnp.exp(s - m)
                denom = jnp.sum(p, axis=-1, keepdims=True) + jnp.exp(sink - m)
                ps.append((p / denom).astype(BF16))
            o = _dot(jnp.concatenate(ps, axis=1), vv)
            o_ref[:, col:col + LANES] = o.astype(BF16)


def _swa(sinks, q, k, v, negdist, allow, groups, q_blocks, tq):
    n = q.shape[0]
    kblocks = k.shape[0] // LANES // groups
    shift = kblocks - q_blocks
    prev = lambda b, j: (b * kblocks + jnp.maximum(j + shift - 1, 0), 0)
    cur = lambda b, j: (b * kblocks + j + shift, 0)
    kv_spec = lambda f: pl.BlockSpec((LANES, KV_A), f)
    return pl.pallas_call(
        _swa_body,
        grid=(groups, q_blocks),
        in_specs=[pl.BlockSpec(memory_space=pltpu.SMEM),
                  pl.BlockSpec((tq, Q_A), lambda b, j: (b * q_blocks + j, 0)),
                  kv_spec(prev), kv_spec(cur), kv_spec(prev), kv_spec(cur),
                  pl.BlockSpec((tq, ATT_KEYS), lambda b, j: (0, 0)),
                  pl.BlockSpec((1, tq, ATT_KEYS), lambda b, j: (jnp.minimum(j + shift, 1), 0, 0))],
        out_specs=pl.BlockSpec((tq, Q_A), lambda b, j: (b * q_blocks + j, 0)),
        out_shape=jax.ShapeDtypeStruct((n, Q_A), BF16),
        compiler_params=_params(("parallel", "parallel")),
        name="attention",
    )(sinks, q, k, k, v, v, negdist, allow)


def _ret_body(q_ref, k_ref, v_ref, rg_ref, s0_ref, intra_ref, rdec_ref, kdec_ref, sdec_ref,
              o_ref, st_ref):
    c = pl.program_id(1)

    @pl.when(c == 0)
    def _():
        st_ref[...] = s0_ref[...]

    for h in range(N_HEADS_B):
        q = q_ref[:, h * DK_B:(h + 1) * DK_B]
        k = k_ref[:, h * DK_B:(h + 1) * DK_B]
        v = v_ref[:, h * DV_B:(h + 1) * DV_B]
        st = st_ref[0, h]
        scores = (_dot_nt(q, k) * intra_ref[h]).astype(BF16)
        o = _dot(scores, v) + _dot(q, st.astype(BF16)) * rdec_ref[h]
        k_dec = (k.astype(F32) * kdec_ref[h]).astype(BF16)
        st_ref[0, h] = sdec_ref[h] * st + _dot_tn(k_dec, v)
        mu = jnp.mean(o, axis=-1, keepdims=True)
        d = o - mu
        var = jnp.mean(d * d, axis=-1, keepdims=True)
        on = d * lax.rsqrt(var + EPS)
        r = rg_ref[:, h * DV_B:(h + 1) * DV_B]
        o_ref[:, h * DV_B:(h + 1) * DV_B] = (on * (r * jax.nn.sigmoid(r))).astype(BF16)


def _retention_consts(length):
    lg = jnp.log1p(-(2.0 ** (-5.0 - jnp.arange(N_HEADS_B, dtype=F32))))
    n = jnp.arange(length, dtype=F32)
    diff = n[:, None] - n[None, :]
    intra = jnp.where(diff >= 0, jnp.exp(jnp.maximum(diff, 0.0) * lg[:, None, None]), 0.0)
    rdec = jnp.exp((n[None, :] + 1.0) * lg[:, None])[..., None]
    kdec = jnp.exp((length - 1.0 - n)[None, :] * lg[:, None])[..., None]
    sdec = jnp.exp(length * lg)
    return intra, rdec, kdec, sdec


def _retention(q, k, v, rg, s0, streams, chunks, length):
    n = q.shape[0]
    intra, rdec, kdec, sdec = _retention_consts(length)
    row = lambda width: pl.BlockSpec((length, width), lambda b, c: (b * chunks + c, 0))
    const = lambda a: pl.BlockSpec(a.shape, lambda b, c: (0,) * a.ndim)
    st_spec = pl.BlockSpec((1, N_HEADS_B, DK_B, DV_B), lambda b, c: (b, 0, 0, 0))
    return pl.pallas_call(
        _ret_body,
        grid=(streams, chunks),
        in_specs=[row(Q_B), row(Q_B), row(V_B), row(V_B), st_spec,
                  const(intra), const(rdec), const(kdec), pl.BlockSpec(memory_space=pltpu.SMEM)],
        out_specs=[row(V_B), st_spec],
        out_shape=[jax.ShapeDtypeStruct((n, V_B), BF16),
                   jax.ShapeDtypeStruct((streams, N_HEADS_B, DK_B, DV_B), F32)],
        compiler_params=_params(("parallel", "arbitrary")),
        name="retention",
    )(q, k, v, rg, s0, intra, rdec, kdec, sdec)


def _merge_body(x_ref, oa_ref, ob_ref, ga_ref, gb_ref, wa_ref, wb_ref, wo_ref, o_ref):
    ya = _dot(oa_ref[...], wa_ref[...])
    yb = _dot(ob_ref[...], wb_ref[...])
    mix = jax.nn.sigmoid(ga_ref[...]) * ya + jax.nn.sigmoid(gb_ref[...]) * yb
    o_ref[...] = x_ref[...] + _dot(mix.astype(BF16), wo_ref[...])


def _merge(x, oa, ob, ga, gb, wa, wb, wo, tm):
    n = x.shape[0]
    row = lambda width: pl.BlockSpec((tm, width), lambda i: (i, 0))
    full = lambda a: pl.BlockSpec(a.shape, lambda i: (0, 0))
    return pl.pallas_call(
        _merge_body,
        grid=(n // tm,),
        in_specs=[row(D_MODEL), row(Q_A), row(V_B), row(D_MODEL), row(D_MODEL), full(wa), full(wb), full(wo)],
        out_specs=row(D_MODEL),
        out_shape=jax.ShapeDtypeStruct((n, D_MODEL), F32),
        compiler_params=_params(("parallel",)),
        name="merge",
    )(x, oa, ob, ga, gb, wa, wb, wo)


def _top16(s, order):
    t = s.shape[1]
    rank = lax.broadcasted_iota(I32, (PEER_TOPK, t), 0)
    vals = jnp.zeros((PEER_TOPK, t), F32)
    ids = jnp.zeros((PEER_TOPK, t), F32)
    for r in range(PEER_TOPK):
        m = jnp.max(s, axis=0, keepdims=True)
        pick = jnp.min(jnp.where(s == m, order, 3e38), axis=0, keepdims=True)
        s = jnp.where(order == pick, -jnp.inf, s)
        vals = jnp.where(rank == r, m, vals)
        ids = jnp.where(rank == r, pick, ids)
    return vals, ids


def _route_body(x_ref, g_ref, wq_ref, k1_ref, k2_ref, idx_ref, gate_ref, h_ref):
    h_ref[...] = _rmsnorm(x_ref[...], g_ref[...])
    h = h_ref[...].astype(BF16)
    t = h.shape[0]
    q_t = _dot_nt(wq_ref[...], h).astype(BF16)
    iota = lambda rows: lax.broadcasted_iota(I32, (rows, t), 0).astype(F32)
    key_id = iota(N_KEYS)
    flat = jnp.concatenate(
        [iota(PEER_TOPK)] + [a * PEER_TOPK + iota(8) for a in range(1, 8)] + [(8 + iota(8)) * PEER_TOPK],
        axis=0)
    rank = lax.broadcasted_iota(I32, (PEER_TOPK, t), 0)
    idx_rows, gate_rows = [], []
    for hd in range(PEER_HEADS):
        base = hd * PEER_QDIM
        s1 = _dot(k1_ref[...], q_t[base:base + PEER_HALF])
        s2 = _dot(k2_ref[...], q_t[base + PEER_HALF:base + PEER_QDIM])
        v1, i1 = _top16(s1, key_id)
        v2, i2 = _top16(s2, key_id)
        cand = jnp.concatenate(
            [v1[0:1] + v2]
            + [v1[a:a + 1] + v2[0:8] for a in range(1, 8)]
            + [v1[8:16] + v2[0:1]], axis=0)
        cidx = jnp.concatenate(
            [i1[0:1] * N_KEYS + i2]
            + [i1[a:a + 1] * N_KEYS + i2[0:8] for a in range(1, 8)]
            + [i1[8:16] * N_KEYS + i2[0:1]], axis=0)
        best, pos = _top16(cand, flat)
        eidx = jnp.zeros((PEER_TOPK, t), F32)
        for r in range(PEER_TOPK):
            picked = jnp.max(jnp.where(flat == pos[r:r + 1], cidx, -1.0), axis=0, keepdims=True)
            eidx = jnp.where(rank == r, picked, eidx)
        e = jnp.exp(best - best[0:1])
        gate_rows.append(e / jnp.sum(e, axis=0, keepdims=True))
        idx_rows.append(eidx.astype(I32))
    idx_ref[...] = jnp.concatenate(idx_rows, axis=0)
    gate_ref[...] = jnp.concatenate(gate_rows, axis=0)


def _route(x, g, wq_t, k1, k2, tm):
    n = x.shape[0]
    full = lambda a: pl.BlockSpec(a.shape, lambda i: (0, 0))
    col = pl.BlockSpec((PICKS, tm), lambda i: (0, i))
    return pl.pallas_call(
        _route_body,
        grid=(n // tm,),
        in_specs=[pl.BlockSpec((tm, D_MODEL), lambda i: (i, 0)), full(g), full(wq_t), full(k1), full(k2)],
        out_specs=[col, col, pl.BlockSpec((tm, D_MODEL), lambda i: (i, 0))],
        out_shape=[jax.ShapeDtypeStruct((PICKS, n), I32), jax.ShapeDtypeStruct((PICKS, n), F32),
                   jax.ShapeDtypeStruct((n, D_MODEL), F32)],
        compiler_params=_params(("parallel",)),
        name="routing",
    )(x, g, wq_t, k1, k2)


SC_CORES = 2
SC_SUBCORES = 16
SC_LANES = 16
SC_CHUNK = 16
SC_GROUP_BOUNDS = ((0, 14336),)


def _gelu_via_exp(x):
    z = 0.7978845608028654 * (x + 0.044715 * x * x * x)
    return 0.5 * x * (2.0 - 2.0 / (1.0 + jnp.exp(2.0 * z)))


class _Copies:
    def __init__(self, copies):
        self.copies = copies

    def start(self):
        for cp in self.copies:
            cp.start()

    def wait(self):
        for cp in self.copies:
            cp.wait()


def _sc_experts(u_tab, v_tab, idx, gate, h, first):
    n = idx.shape[0]
    workers = SC_CORES * SC_SUBCORES
    per_worker = n // workers
    n_chunks = PICKS // SC_CHUNK
    steps = D_MODEL // SC_LANES
    mesh = plsc.VectorSubcoreMesh(core_axis_name="c", subcore_axis_name="s",
                                  num_cores=SC_CORES, num_subcores=SC_SUBCORES)

    def body(u_hbm, v_hbm, idx_hbm, gate_hbm, h_hbm, y_hbm,
             idx_0, idx_1, gate_0, gate_1, h_0, h_1, y_0, y_1, u_rows_a, u_rows_b, v_rows_a, v_rows_b,
             in_sem_0, in_sem_1, y_sem_0, y_sem_1, sem_a, sem_b):
        base = (lax.axis_index("s") * SC_CORES + lax.axis_index("c")) * per_worker
        last = base + per_worker - 1
        sets = ((idx_0, gate_0, h_0, y_0, in_sem_0, y_sem_0), (idx_1, gate_1, h_1, y_1, in_sem_1, y_sem_1))
        bufs = ((u_rows_a, v_rows_a, sem_a), (u_rows_b, v_rows_b, sem_b))

        def in_copies(t, q):
            idx_v, gate_v, h_v, _, sem, _ = sets[q]
            return (pltpu.make_async_copy(idx_hbm.at[t], idx_v, sem),
                    pltpu.make_async_copy(gate_hbm.at[t], gate_v, sem),
                    pltpu.make_async_copy(h_hbm.at[t + first], h_v, sem))

        def y_copy(t, q):
            return pltpu.make_async_copy(sets[q][3], y_hbm.at[t], sets[q][5])

        def gather(c, b, q):
            u_rows, v_rows, sem = bufs[b]
            picks = sets[q][0].at[pl.ds(c * SC_CHUNK, SC_CHUNK)]
            return _Copies((pltpu.make_async_copy(u_hbm.at[picks], u_rows, sem),
                            pltpu.make_async_copy(v_hbm.at[picks], v_rows, sem)))

        def compute(c, b, q):
            u_rows, v_rows, _ = bufs[b]
            _, gate_v, h_v, y_v, _, _ = sets[q]

            def dot_step(j, accs):
                hj = h_v[pl.ds(j * SC_LANES, SC_LANES)]
                return tuple(accs[k] + u_rows[k, pl.ds(j * SC_LANES, SC_LANES)] * hj for k in range(SC_CHUNK))

            accs = lax.fori_loop(0, steps, dot_step,
                                 tuple(jnp.zeros((SC_LANES,), F32) for _ in range(SC_CHUNK)))
            ws = []
            for k in range(SC_CHUNK):
                s = jnp.full((SC_LANES,), jnp.sum(accs[k]), F32)
                g = plsc.load_gather(gate_v, [jnp.full((SC_LANES,), c * SC_CHUNK + k, I32)])
                ws.append(_gelu_via_exp(s) * g)

            def sum_step(j, carry):
                acc = y_v[pl.ds(j * SC_LANES, SC_LANES)]
                for k in range(SC_CHUNK):
                    acc = acc + ws[k] * v_rows[k, pl.ds(j * SC_LANES, SC_LANES)]
                y_v[pl.ds(j * SC_LANES, SC_LANES)] = acc
                return carry

            lax.fori_loop(0, steps, sum_step, 0)

        def token(t, q, first):
            nxt = jnp.minimum(t + 1, last)
            for cp in in_copies(nxt, 1 - q):
                cp.start()
            y_v = sets[q][3]

            @pl.when(jnp.logical_not(first))
            def _():
                y_copy(t, q).wait()

            @pl.loop(0, steps)
            def _(j):
                y_v[pl.ds(j * SC_LANES, SC_LANES)] = jnp.zeros((SC_LANES,), F32)

            @pl.loop(0, n_chunks // 2)
            def _(p):
                c0 = p * 2
                gather(c0 + 1, 1, q).start()
                gather(c0, 0, q).wait()
                compute(c0, 0, q)

                @pl.when(p + 1 < n_chunks // 2)
                def _():
                    gather(c0 + 2, 0, q).start()

                gather(c0 + 1, 1, q).wait()
                compute(c0 + 1, 1, q)

            for cp in in_copies(nxt, 1 - q):
                cp.wait()
            gather(0, 0, 1 - q).start()
            y_copy(t, q).start()

        for cp in in_copies(base, 0):
            cp.start()
        for cp in in_copies(base, 0):
            cp.wait()
        gather(0, 0, 0).start()

        @pl.loop(0, per_worker // 2)
        def _(i):
            token(base + 2 * i, 0, i == 0)
            token(base + 2 * i + 1, 1, i == 0)

        gather(0, 0, 0).wait()
        y_copy(last - 1, 0).wait()
        y_copy(last, 1).wait()

    vec = lambda n_el, dt: [pltpu.VMEM((n_el,), dt), pltpu.VMEM((n_el,), dt)]
    return pl.kernel(
        body, out_type=jax.ShapeDtypeStruct((n, D_MODEL), F32), mesh=mesh,
        scratch_types=vec(PICKS, I32) + vec(PICKS, F32) + vec(D_MODEL, F32) + vec(D_MODEL, F32)
        + [pltpu.VMEM((SC_CHUNK, D_MODEL), F32) for _ in range(4)]
        + [pltpu.SemaphoreType.DMA] * 6,
        compiler_params=pltpu.CompilerParams(needs_layout_passes=False),
        cost_estimate=pl.CostEstimate(flops=4 * n * PICKS * D_MODEL, transcendentals=n * PICKS,
                                      bytes_accessed=4 * n * (PICKS * 2 * D_MODEL + 2 * PICKS + 2 * D_MODEL)),
        name="experts_sc",
    )(u_tab, v_tab, idx, gate, h)


def _finish_body(x_ref, y_ref, gf_ref, *rest_and_out):
    rest_and_out[-1][...] = _rmsnorm(x_ref[...] + y_ref[...], gf_ref[...])


def _finish(x, y, g_final, rest, tm, first_row=0):
    first_block = first_row // tm
    row = pl.BlockSpec((tm, D_MODEL), lambda i: (i + first_block, 0))
    extra = () if rest is None else (rest,)
    return pl.pallas_call(
        _finish_body,
        grid=(y.shape[0] // tm,),
        in_specs=[row, pl.BlockSpec((tm, D_MODEL), lambda i: (i, 0)), pl.BlockSpec(g_final.shape, lambda i: (0, 0))]
        + [pl.BlockSpec(memory_space=pl.ANY) for _ in extra],
        out_specs=row,
        out_shape=jax.ShapeDtypeStruct(x.shape if rest is not None else y.shape, F32),
        input_output_aliases={3: 0} if rest is not None else {},
        compiler_params=_params(("parallel",)),
        name="finish",
    )(x, y, g_final, *extra)


EXP_TOKENS = 256
EXP_SLOTS = 4


def _expert_body(x_ref, g_ref, gf_ref, idx_hbm, gate_ref, tab_hbm, o_ref,
                 idx_smem, buf0, buf1, buf2, buf3, h_buf, y_buf, sems, idx_sem, *, first_block):
    bufs = (buf0, buf1, buf2, buf3)
    ahead = EXP_SLOTS - 1
    i = pl.program_id(0) + first_block
    idx_copy = pltpu.make_async_copy(idx_hbm.at[pl.ds(i * (EXP_TOKENS * PICKS), EXP_TOKENS * PICKS)],
                                     idx_smem, idx_sem)
    idx_copy.start()
    h_buf[...] = _rmsnorm(x_ref[...], g_ref[...])
    idx_copy.wait()

    def issue(t, slot):
        for k in range(PICKS):
            e = idx_smem[t * PICKS + k]
            pltpu.async_copy(tab_hbm.at[e], bufs[slot].at[pl.ds(k, 1), :], sems.at[slot], priority=k % 2)

    def wait_slot(slot):
        pltpu.make_async_copy(bufs[(slot + 1) % EXP_SLOTS], bufs[slot], sems.at[slot]).wait()

    lane = lax.broadcasted_iota(I32, (PICKS, EXP_TOKENS), 1)

    def compute(t, slot):
        h = h_buf[pl.ds(t, 1), :]
        u = lax.bitcast_convert_type(bufs[slot][...] & jnp.int32(-65536), F32)
        s = jnp.sum(u * h, axis=-1, keepdims=True)
        gate = jnp.sum(jnp.where(lane == t, gate_ref[...], 0.0), axis=-1, keepdims=True)
        w = gate * jax.nn.gelu(s)
        v = lax.bitcast_convert_type(bufs[slot][...] << 16, F32)
        y_buf[pl.ds(t, 1), :] = jnp.sum(w * v, axis=0, keepdims=True)

    def group(t0, issue_upto):
        for s in range(EXP_SLOTS):
            wait_slot(s)
            if s < issue_upto:
                issue(t0 + s + ahead, (s + ahead) % EXP_SLOTS)
            compute(t0 + s, s)

    for s in range(ahead):
        issue(s, s)

    def steady(q, carry):
        group(q * EXP_SLOTS, EXP_SLOTS)
        return carry

    n_groups = EXP_TOKENS // EXP_SLOTS
    lax.fori_loop(0, n_groups - 1, steady, 0)
    group((n_groups - 1) * EXP_SLOTS, EXP_SLOTS - ahead)
    o_ref[...] = _rmsnorm(x_ref[...] + y_buf[...], gf_ref[...])


def _experts(x, g, g_final, idx_flat, gate_t, table, first_token):
    n = x.shape[0]
    first_block = first_token // EXP_TOKENS
    full = lambda a: pl.BlockSpec(a.shape, lambda i: (0, 0))
    row = pl.BlockSpec((EXP_TOKENS, D_MODEL), lambda i: (i + first_block, 0))
    return pl.pallas_call(
        functools.partial(_expert_body, first_block=first_block),
        grid=(n // EXP_TOKENS - first_block,),
        in_specs=[row, full(g), full(g_final),
                  pl.BlockSpec(memory_space=pl.ANY),
                  pl.BlockSpec((PICKS, EXP_TOKENS), lambda i: (0, i + first_block)),
                  pl.BlockSpec(memory_space=pl.ANY)],
        out_specs=row,
        out_shape=jax.ShapeDtypeStruct((n, D_MODEL), F32),
        scratch_shapes=[pltpu.SMEM((EXP_TOKENS * PICKS,), I32)]
        + [pltpu.VMEM((PICKS, D_MODEL), I32) for _ in range(EXP_SLOTS)]
        + [pltpu.VMEM((EXP_TOKENS, D_MODEL), F32),
           pltpu.VMEM((EXP_TOKENS, D_MODEL), F32),
           pltpu.SemaphoreType.DMA((EXP_SLOTS,)),
           pltpu.SemaphoreType.DMA(())],
        compiler_params=pltpu.CompilerParams(dimension_semantics=("arbitrary",), vmem_limit_bytes=VMEM_LIMIT,
                                             disable_bounds_checks=True),
        name="experts",
    )(x, g, g_final, idx_flat, gate_t, table)


def _attention_masks(tq):
    i = jnp.arange(tq, dtype=I32)[:, None]
    r = jnp.arange(ATT_KEYS, dtype=I32)[None, :]
    negdist = -jnp.abs(WINDOW + i - r).astype(F32)
    return negdist, i, r


def _layer(x, n_groups, rows, swa_inputs, ret_state, ret_len, w, tm, sc_bounds=(0,), tie=None):
    (norm_mix, w_in, b_gate, sinks, wa, wb, wo, norm_ffn, wq_t, k1, k2, (packed, u_tab, v_tab), norm_final) = w
    assert sc_bounds[-1] < x.shape[0]
    qa, ka, va, qb, kb, vb, rg, ga, gb = _project(x, norm_mix, w_in, b_gate, tm)
    oa, k_rows, v_rows = swa_inputs(qa, ka, va, sinks)
    ob, s_fin = _retention(qb, kb, vb, rg, ret_state, n_groups, rows // ret_len, ret_len)
    x1 = _merge(x, oa, ob, ga, gb, wa, wb, wo, tm)
    idx_t, gate_t, h2 = _route(x1, norm_ffn, wq_t, k1, k2, tm)
    y_sc = [(lo, _sc_experts(u_tab, v_tab, idx_t[:, lo:hi].T, gate_t[:, lo:hi].T, h2, lo))
            for lo, hi in zip(sc_bounds[:-1], sc_bounds[1:])]
    if tie is not None:
        tie, gate_t = lax.optimization_barrier((tie, gate_t))
    y_tc = _experts(x1, norm_ffn, norm_final, idx_t.T.reshape(-1), gate_t, packed, sc_bounds[-1])
    return (x1, y_sc, y_tc), k_rows, v_rows, s_fin, tie


def _layer_output(parts, norm_final, tm):
    x1, y_sc, y = parts
    for first_row, y_part in y_sc:
        y = _finish(x1, y_part, norm_final, y, tm, first_row)
    return y


def kernel(x_prompt, x_sample, cache_swa_k, cache_swa_v, state_ret, norm_mix, w_in, b_gate, attn_sinks,
           w_branch_a, w_branch_b, w_out, norm_ffn, peer_w_q, peer_sub_k1, peer_sub_k2, peer_u, peer_v,
           norm_final):
    batch, seq, _ = x_prompt.shape
    dec_batch, dec_seq, _ = x_sample.shape
    assert norm_mix.shape[0] == 1, "single-layer trunk"
    half = lambda a: lax.bitcast_convert_type(a.astype(BF16), jnp.uint16).astype(jnp.uint32)
    packed = lax.bitcast_convert_type((half(peer_u[0]) << 16) | half(peer_v[0]), I32)[:, None, :]
    packed, x_prompt, x_sample = lax.optimization_barrier((packed, x_prompt, x_sample))
    table = (packed, peer_u[0], peer_v[0])
    w = (norm_mix[0][None], w_in[0].astype(BF16), b_gate[0][None], attn_sinks[0],
         w_branch_a[0].astype(BF16), w_branch_b[0].astype(BF16), w_out[0].astype(BF16),
         norm_ffn[0][None], peer_w_q[0].T.astype(BF16), peer_sub_k1[0].astype(BF16),
         peer_sub_k2[0].astype(BF16), table, norm_final[None])

    def swa_prompt(qa, ka, va, sinks):
        streams = qa.shape[0] // seq
        tq = 2 * CHUNK
        negdist, i, r = _attention_masks(tq)
        lq, lk = i // CHUNK, r // CHUNK
        band = (lk >= lq) & (lk <= lq + 2)
        allow = jnp.stack([band & (r >= WINDOW), band]).astype(F32)
        oa = _swa(sinks, qa, ka, va, negdist, allow, streams, seq // tq, tq)
        tail = lambda a: a.reshape(streams, seq, N_KV_A, HEAD_DIM_A)[:, seq - WINDOW:]
        return oa, tail(ka), tail(va)

    def swa_sample(qa, ka, va, sinks):
        negdist, i, r = _attention_masks(dec_seq)
        visible = jnp.broadcast_to(r < WINDOW + dec_seq, (dec_seq, ATT_KEYS))
        allow = jnp.stack([visible, visible]).astype(F32)
        pad = jnp.zeros((dec_batch, ATT_KEYS - WINDOW - dec_seq, KV_A), F32)
        k_all = jnp.concatenate([cache_swa_k[0].reshape(dec_batch, WINDOW, KV_A),
                                 ka.reshape(dec_batch, dec_seq, KV_A), pad], axis=1)
        v_all = jnp.concatenate([cache_swa_v[0].reshape(dec_batch, WINDOW, KV_A),
                                 va.reshape(dec_batch, dec_seq, KV_A), pad], axis=1)
        oa = _swa(sinks, qa, k_all.reshape(-1, KV_A), v_all.reshape(-1, KV_A), negdist, allow,
                  dec_batch, 1, dec_seq)
        tail = lambda a: a[:, dec_seq:WINDOW + dec_seq].reshape(dec_batch, WINDOW, N_KV_A, HEAD_DIM_A)
        return oa, tail(k_all), tail(v_all)

    parts, ks, vs, ss, _ = _layer(x_sample.reshape(dec_batch * dec_seq, D_MODEL), dec_batch, dec_seq,
                                  swa_sample, state_ret[0], dec_seq, w, 128)
    ys = _layer_output(parts, w[-1], 128)

    n_grp = len(SC_GROUP_BOUNDS)
    per_group = batch // n_grp
    xs = [x_prompt[g * per_group:(g + 1) * per_group].reshape(per_group * seq, D_MODEL) for g in range(n_grp)]
    s0 = jnp.zeros((per_group, N_HEADS_B, DK_B, DV_B), F32)
    outs, tie = [], None
    for g, bounds in enumerate(SC_GROUP_BOUNDS):
        (x1, y_sc, y_tc), kg, vg, sg, tie = _layer(xs[g], per_group, seq, swa_prompt, s0, 256, w, 256, bounds, tie)
        if tie is not None:
            (px1, py_sc, py_tc), pk, pv, ps = outs[-1]
            outs[-1] = ((px1, [(py_sc[0][0], tie)] + py_sc[1:], py_tc), pk, pv, ps)
            tie = None
        if g + 1 < n_grp:
            y_tc, xs[g + 1] = lax.optimization_barrier((y_tc, xs[g + 1]))
            tie = y_sc[0][1]
        outs.append(((x1, y_sc, y_tc), kg, vg, sg))
    outs = [(_layer_output(p, w[-1], 256), kg, vg, sg) for p, kg, vg, sg in outs]
    yp, kp, vp, sp = (jnp.concatenate(parts, axis=0) for parts in zip(*outs))

    return (yp.reshape(batch, seq, D_MODEL), ys.reshape(dec_batch, dec_seq, D_MODEL),
            kp[None], vp[None], sp[None], ks[None], vs[None], ss[None])
```

```python
import functools

import jax
import jax.numpy as jnp
from jax import lax
from jax.experimental import pallas as pl
from jax.experimental.pallas import tpu as pltpu
from jax.experimental.pallas import tpu_sc as plsc

F32 = jnp.float32
BF16 = jnp.bfloat16
I32 = jnp.int32

D_MODEL = 1024
CHUNK = 64
EPS = 1e-6
NEG_INF = -1e30
PAST_LEN = 2048

N_HEADS_A = 8
N_KV_A = 2
GROUP_A = N_HEADS_A // N_KV_A
HEAD_DIM_A = 64
WINDOW = 128
N_HEADS_B = 4
DK_B = 128
DV_B = 256
Q_A = N_HEADS_A * HEAD_DIM_A
KV_A = N_KV_A * HEAD_DIM_A
Q_B = N_HEADS_B * DK_B
V_B = N_HEADS_B * DV_B
D_IN = Q_A + 2 * KV_A + 2 * Q_B + 2 * V_B + 2 * D_MODEL
OFF_QA = 0
OFF_KA = OFF_QA + Q_A
OFF_VA = OFF_KA + KV_A
OFF_QB = OFF_VA + KV_A
OFF_KB = OFF_QB + Q_B
OFF_VB = OFF_KB + Q_B
OFF_RG = OFF_VB + V_B
OFF_GA = OFF_RG + V_B
OFF_GB = OFF_GA + D_MODEL

N_KEYS = 128
N_EXPERTS = N_KEYS * N_KEYS
PEER_HEADS = 8
PEER_QDIM = 256
PEER_HALF = PEER_QDIM // 2
PEER_TOPK = 16
PICKS = PEER_HEADS * PEER_TOPK

LANES = 128
ATT_KEYS = 2 * WINDOW
VMEM_LIMIT = 56 * 1024 * 1024


def _rmsnorm(x, g):
    return x * lax.rsqrt(jnp.mean(x * x, axis=-1, keepdims=True) + EPS) * g


def _dot(a, b):
    return jnp.dot(a, b, preferred_element_type=F32)


def _dot_nt(a, b):
    return lax.dot_general(a, b, (((1,), (1,)), ((), ())), preferred_element_type=F32)


def _dot_tn(a, b):
    return lax.dot_general(a, b, (((0,), (0,)), ((), ())), preferred_element_type=F32)


def _params(sem):
    return pltpu.CompilerParams(dimension_semantics=sem, vmem_limit_bytes=VMEM_LIMIT)


def _proj_body(x_ref, g_ref, w_ref, bg_ref, qa_ref, ka_ref, va_ref, qb_ref, kb_ref, vb_ref,
               rg_ref, ga_ref, gb_ref):
    h = _rmsnorm(x_ref[...], g_ref[...]).astype(BF16)

    def mm(lo, width):
        return _dot(h, w_ref[:, lo:lo + width])

    qa_ref[...] = (mm(OFF_QA, Q_A) * (HEAD_DIM_A ** -0.5)).astype(BF16)
    ka_ref[...] = mm(OFF_KA, KV_A)
    va_ref[...] = mm(OFF_VA, KV_A)
    qb_ref[...] = mm(OFF_QB, Q_B).astype(BF16)
    kb_ref[...] = (mm(OFF_KB, Q_B) * (DK_B ** -0.5)).astype(BF16)
    vb_ref[...] = mm(OFF_VB, V_B).astype(BF16)
    rg_ref[...] = mm(OFF_RG, V_B)
    ga_ref[...] = mm(OFF_GA, D_MODEL) + bg_ref[:, :D_MODEL]
    gb_ref[...] = mm(OFF_GB, D_MODEL) + bg_ref[:, D_MODEL:]


def _project(x, g, w_bf16, b_gate, tm):
    n = x.shape[0]
    row = lambda width: pl.BlockSpec((tm, width), lambda i: (i, 0))
    full = lambda a: pl.BlockSpec(a.shape, lambda i: (0, 0))
    widths = (Q_A, KV_A, KV_A, Q_B, Q_B, V_B, V_B, D_MODEL, D_MODEL)
    dtypes = (BF16, F32, F32, BF16, BF16, BF16, F32, F32, F32)
    return pl.pallas_call(
        _proj_body,
        grid=(n // tm,),
        in_specs=[row(D_MODEL), full(g), full(w_bf16), full(b_gate)],
        out_specs=[row(w) for w in widths],
        out_shape=[jax.ShapeDtypeStruct((n, w), d) for w, d in zip(widths, dtypes)],
        compiler_params=_params(("parallel",)),
        name="projection",
    )(x, g, w_bf16, b_gate)


def _swa_body(sink_ref, q_ref, kp_ref, kc_ref, vp_ref, vc_ref, nd_ref, al_ref, o_ref):
    k = jnp.concatenate([kp_ref[...], kc_ref[...]], axis=0)
    v = jnp.concatenate([vp_ref[...], vc_ref[...]], axis=0)
    lane = lax.broadcasted_iota(I32, k.shape, 1)
    low = lane < HEAD_DIM_A
    k_sw = pltpu.roll(k, HEAD_DIM_A, axis=1)
    v_sw = pltpu.roll(v, HEAD_DIM_A, axis=1)
    negdist = nd_ref[...]
    allowed = al_ref[0] > 0.5
    for kv in range(N_KV_A):
        own = low if kv == 0 else jnp.logical_not(low)
        k_rep = jnp.where(own, k, k_sw)
        v_rep = jnp.where(own, v, v_sw)
        zero = jnp.zeros_like(k_rep)
        kk = jnp.concatenate([jnp.where(low, k_rep, zero), jnp.where(low, zero, k_rep)], axis=0).astype(BF16)
        vv = jnp.concatenate([jnp.where(low, v_rep, zero), jnp.where(low, zero, v_rep)], axis=0).astype(BF16)
        for a in range(GROUP_A // 2):
            col = kv * GROUP_A * HEAD_DIM_A + a * LANES
            s2 = _dot_nt(q_ref[:, col:col + LANES], kk)
            ps = []
            for u in range(2):
                head = kv * GROUP_A + 2 * a + u
                slope = 2.0 ** (-8.0 * (head + 1) / N_HEADS_A)
                sink = sink_ref[head]
                s = s2[:, u * ATT_KEYS:(u + 1) * ATT_KEYS] + slope * negdist
                s = jnp.where(allowed, s, NEG_INF)
                m = jnp.maximum(jnp.max(s, axis=-1, keepdims=True), sink)
                p = jnp.exp(s - m)
                denom = jnp.sum(p, axis=-1, keepdims=True) + jnp.exp(sink - m)
                ps.append((p / denom).astype(BF16))
            o = _dot(jnp.concatenate(ps, axis=1), vv)
            o_ref[:, col:col + LANES] = o.astype(BF16)


def _swa(sinks, q, k, v, negdist, allow, groups, q_blocks, tq):
    n = q.shape[0]
    kblocks = k.shape[0] // LANES // groups
    shift = kblocks - q_blocks
    prev = lambda b, j: (b * kblocks + jnp.maximum(j + shift - 1, 0), 0)
    cur = lambda b, j: (b * kblocks + j + shift, 0)
    kv_spec = lambda f: pl.BlockSpec((LANES, KV_A), f)
    return pl.pallas_call(
        _swa_body,
        grid=(groups, q_blocks),
        in_specs=[pl.BlockSpec(memory_space=pltpu.SMEM),
                  pl.BlockSpec((tq, Q_A), lambda b, j: (b * q_blocks + j, 0)),
                  kv_spec(prev), kv_spec(cur), kv_spec(prev), kv_spec(cur),
                  pl.BlockSpec((tq, ATT_KEYS), lambda b, j: (0, 0)),
                  pl.BlockSpec((1, tq, ATT_KEYS), lambda b, j: (jnp.minimum(j + shift, 1), 0, 0))],
        out_specs=pl.BlockSpec((tq, Q_A), lambda b, j: (b * q_blocks + j, 0)),
        out_shape=jax.ShapeDtypeStruct((n, Q_A), BF16),
        compiler_params=_params(("parallel", "parallel")),
        name="attention",
    )(sinks, q, k, k, v, v, negdist, allow)


def _ret_body(q_ref, k_ref, v_ref, rg_ref, s0_ref, intra_ref, rdec_ref, kdec_ref, sdec_ref,
              o_ref, st_ref):
    c = pl.program_id(1)

    @pl.when(c == 0)
    def _():
        st_ref[...] = s0_ref[...]

    for h in range(N_HEADS_B):
        q = q_ref[:, h * DK_B:(h + 1) * DK_B]
        k = k_ref[:, h * DK_B:(h + 1) * DK_B]
        v = v_ref[:, h * DV_B:(h + 1) * DV_B]
        st = st_ref[0, h]
        scores = (_dot_nt(q, k) * intra_ref[h]).astype(BF16)
        o = _dot(scores, v) + _dot(q, st.astype(BF16)) * rdec_ref[h]
        k_dec = (k.astype(F32) * kdec_ref[h]).astype(BF16)
        st_ref[0, h] = sdec_ref[h] * st + _dot_tn(k_dec, v)
        mu = jnp.mean(o, axis=-1, keepdims=True)
        d = o - mu
        var = jnp.mean(d * d, axis=-1, keepdims=True)
        on = d * lax.rsqrt(var + EPS)
        r = rg_ref[:, h * DV_B:(h + 1) * DV_B]
        o_ref[:, h * DV_B:(h + 1) * DV_B] = (on * (r * jax.nn.sigmoid(r))).astype(BF16)


def _retention_consts(length):
    lg = jnp.log1p(-(2.0 ** (-5.0 - jnp.arange(N_HEADS_B, dtype=F32))))
    n = jnp.arange(length, dtype=F32)
    diff = n[:, None] - n[None, :]
    intra = jnp.where(diff >= 0, jnp.exp(jnp.maximum(diff, 0.0) * lg[:, None, None]), 0.0)
    rdec = jnp.exp((n[None, :] + 1.0) * lg[:, None])[..., None]
    kdec = jnp.exp((length - 1.0 - n)[None, :] * lg[:, None])[..., None]
    sdec = jnp.exp(length * lg)
    return intra, rdec, kdec, sdec


def _retention(q, k, v, rg, s0, streams, chunks, length):
    n = q.shape[0]
    intra, rdec, kdec, sdec = _retention_consts(length)
    row = lambda width: pl.BlockSpec((length, width), lambda b, c: (b * chunks + c, 0))
    const = lambda a: pl.BlockSpec(a.shape, lambda b, c: (0,) * a.ndim)
    st_spec = pl.BlockSpec((1, N_HEADS_B, DK_B, DV_B), lambda b, c: (b, 0, 0, 0))
    return pl.pallas_call(
        _ret_body,
        grid=(streams, chunks),
        in_specs=[row(Q_B), row(Q_B), row(V_B), row(V_B), st_spec,
                  const(intra), const(rdec), const(kdec), pl.BlockSpec(memory_space=pltpu.SMEM)],
        out_specs=[row(V_B), st_spec],
        out_shape=[jax.ShapeDtypeStruct((n, V_B), BF16),
                   jax.ShapeDtypeStruct((streams, N_HEADS_B, DK_B, DV_B), F32)],
        compiler_params=_params(("parallel", "arbitrary")),
        name="retention",
    )(q, k, v, rg, s0, intra, rdec, kdec, sdec)


def _merge_body(x_ref, oa_ref, ob_ref, ga_ref, gb_ref, wa_ref, wb_ref, wo_ref, o_ref):
    ya = _dot(oa_ref[...], wa_ref[...])
    yb = _dot(ob_ref[...], wb_ref[...])
    mix = jax.nn.sigmoid(ga_ref[...]) * ya + jax.nn.sigmoid(gb_ref[...]) * yb
    o_ref[...] = x_ref[...] + _dot(mix.astype(BF16), wo_ref[...])


def _merge(x, oa, ob, ga, gb, wa, wb, wo, tm):
    n = x.shape[0]
    row = lambda width: pl.BlockSpec((tm, width), lambda i: (i, 0))
    full = lambda a: pl.BlockSpec(a.shape, lambda i: (0, 0))
    return pl.pallas_call(
        _merge_body,
        grid=(n // tm,),
        in_specs=[row(D_MODEL), row(Q_A), row(V_B), row(D_MODEL), row(D_MODEL), full(wa), full(wb), full(wo)],
        out_specs=row(D_MODEL),
        out_shape=jax.ShapeDtypeStruct((n, D_MODEL), F32),
        compiler_params=_params(("parallel",)),
        name="merge",
    )(x, oa, ob, ga, gb, wa, wb, wo)


def _top16(s, order):
    t = s.shape[1]
    rank = lax.broadcasted_iota(I32, (PEER_TOPK, t), 0)
    vals = jnp.zeros((PEER_TOPK, t), F32)
    ids = jnp.zeros((PEER_TOPK, t), F32)
    for r in range(PEER_TOPK):
        m = jnp.max(s, axis=0, keepdims=True)
        pick = jnp.min(jnp.where(s == m, order, 3e38), axis=0, keepdims=True)
        s = jnp.where(order == pick, -jnp.inf, s)
        vals = jnp.where(rank == r, m, vals)
        ids = jnp.where(rank == r, pick, ids)
    return vals, ids


def _route_body(x_ref, g_ref, wq_ref, k1_ref, k2_ref, idx_ref, gate_ref, h_ref):
    h_ref[...] = _rmsnorm(x_ref[...], g_ref[...])
    h = h_ref[...].astype(BF16)
    t = h.shape[0]
    q_t = _dot_nt(wq_ref[...], h).astype(BF16)
    iota = lambda rows: lax.broadcasted_iota(I32, (rows, t), 0).astype(F32)
    key_id = iota(N_KEYS)
    flat = jnp.concatenate(
        [iota(PEER_TOPK)] + [a * PEER_TOPK + iota(8) for a in range(1, 8)] + [(8 + iota(8)) * PEER_TOPK],
        axis=0)
    rank = lax.broadcasted_iota(I32, (PEER_TOPK, t), 0)
    idx_rows, gate_rows = [], []
    for hd in range(PEER_HEADS):
        base = hd * PEER_QDIM
        s1 = _dot(k1_ref[...], q_t[base:base + PEER_HALF])
        s2 = _dot(k2_ref[...], q_t[base + PEER_HALF:base + PEER_QDIM])
        v1, i1 = _top16(s1, key_id)
        v2, i2 = _top16(s2, key_id)
        cand = jnp.concatenate(
            [v1[0:1] + v2]
            + [v1[a:a + 1] + v2[0:8] for a in range(1, 8)]
            + [v1[8:16] + v2[0:1]], axis=0)
        cidx = jnp.concatenate(
            [i1[0:1] * N_KEYS + i2]
            + [i1[a:a + 1] * N_KEYS + i2[0:8] for a in range(1, 8)]
            + [i1[8:16] * N_KEYS + i2[0:1]], axis=0)
        best, pos = _top16(cand, flat)
        eidx = jnp.zeros((PEER_TOPK, t), F32)
        for r in range(PEER_TOPK):
            picked = jnp.max(jnp.where(flat == pos[r:r + 1], cidx, -1.0), axis=0, keepdims=True)
            eidx = jnp.where(rank == r, picked, eidx)
        e = jnp.exp(best - best[0:1])
        gate_rows.append(e / jnp.sum(e, axis=0, keepdims=True))
        idx_rows.append(eidx.astype(I32))
    idx_ref[...] = jnp.concatenate(idx_rows, axis=0)
    gate_ref[...] = jnp.concatenate(gate_rows, axis=0)


def _route(x, g, wq_t, k1, k2, tm, first_row, n):
    first_block = first_row // tm
    full = lambda a: pl.BlockSpec(a.shape, lambda i: (0, 0))
    col = pl.BlockSpec((PICKS, tm), lambda i: (0, i))
    return pl.pallas_call(
        _route_body,
        grid=(n // tm,),
        in_specs=[pl.BlockSpec((tm, D_MODEL), lambda i: (i + first_block, 0)), full(g), full(wq_t), full(k1),
                  full(k2)],
        out_specs=[col, col, pl.BlockSpec((tm, D_MODEL), lambda i: (i, 0))],
        out_shape=[jax.ShapeDtypeStruct((PICKS, n), I32), jax.ShapeDtypeStruct((PICKS, n), F32),
                   jax.ShapeDtypeStruct((n, D_MODEL), F32)],
        compiler_params=_params(("parallel",)),
        name="routing",
    )(x, g, wq_t, k1, k2)


SC_CORES = 2
SC_SUBCORES = 16
SC_LANES = 16
SC_CHUNK = 16
SC_GROUP_BOUNDS = ((0, 14336),)


def _gelu_via_exp(x):
    z = 0.7978845608028654 * (x + 0.044715 * x * x * x)
    return 0.5 * x * (2.0 - 2.0 / (1.0 + jnp.exp(2.0 * z)))


class _Copies:
    def __init__(self, copies):
        self.copies = copies

    def start(self):
        for cp in self.copies:
            cp.start()

    def wait(self):
        for cp in self.copies:
            cp.wait()


def _sc_experts(u_tab, v_tab, idx, gate, h, first):
    n = idx.shape[0]
    workers = SC_CORES * SC_SUBCORES
    per_worker = n // workers
    n_chunks = PICKS // SC_CHUNK
    steps = D_MODEL // SC_LANES
    mesh = plsc.VectorSubcoreMesh(core_axis_name="c", subcore_axis_name="s",
                                  num_cores=SC_CORES, num_subcores=SC_SUBCORES)

    def body(u_hbm, v_hbm, idx_hbm, gate_hbm, h_hbm, y_hbm,
             idx_0, idx_1, gate_0, gate_1, h_0, h_1, y_0, y_1, u_rows_a, u_rows_b, v_rows_a, v_rows_b,
             in_sem_0, in_sem_1, y_sem_0, y_sem_1, sem_a, sem_b):
        base = (lax.axis_index("s") * SC_CORES + lax.axis_index("c")) * per_worker
        last = base + per_worker - 1
        sets = ((idx_0, gate_0, h_0, y_0, in_sem_0, y_sem_0), (idx_1, gate_1, h_1, y_1, in_sem_1, y_sem_1))
        bufs = ((u_rows_a, v_rows_a, sem_a), (u_rows_b, v_rows_b, sem_b))

        def in_copies(t, q):
            idx_v, gate_v, h_v, _, sem, _ = sets[q]
            return (pltpu.make_async_copy(idx_hbm.at[t], idx_v, sem),
                    pltpu.make_async_copy(gate_hbm.at[t], gate_v, sem),
                    pltpu.make_async_copy(h_hbm.at[t + first], h_v, sem))

        def y_copy(t, q):
            return pltpu.make_async_copy(sets[q][3], y_hbm.at[t], sets[q][5])

        def gather(c, b, q):
            u_rows, v_rows, sem = bufs[b]
            picks = sets[q][0].at[pl.ds(c * SC_CHUNK, SC_CHUNK)]
            return _Copies((pltpu.make_async_copy(u_hbm.at[picks], u_rows, sem),
                            pltpu.make_async_copy(v_hbm.at[picks], v_rows, sem)))

        def compute(c, b, q):
            u_rows, v_rows, _ = bufs[b]
            _, gate_v, h_v, y_v, _, _ = sets[q]

            def dot_step(j, accs):
                hj = h_v[pl.ds(j * SC_LANES, SC_LANES)]
                return tuple(accs[k] + u_rows[k, pl.ds(j * SC_LANES, SC_LANES)] * hj for k in range(SC_CHUNK))

            accs = lax.fori_loop(0, steps, dot_step,
                                 tuple(jnp.zeros((SC_LANES,), F32) for _ in range(SC_CHUNK)))
            ws = []
            for k in range(SC_CHUNK):
                s = jnp.full((SC_LANES,), jnp.sum(accs[k]), F32)
                g = plsc.load_gather(gate_v, [jnp.full((SC_LANES,), c * SC_CHUNK + k, I32)])
                ws.append(_gelu_via_exp(s) * g)

            def sum_step(j, carry):
                acc = y_v[pl.ds(j * SC_LANES, SC_LANES)]
                for k in range(SC_CHUNK):
                    acc = acc + ws[k] * v_rows[k, pl.ds(j * SC_LANES, SC_LANES)]
                y_v[pl.ds(j * SC_LANES, SC_LANES)] = acc
                return carry

            lax.fori_loop(0, steps, sum_step, 0)

        def token(t, q, first):
            nxt = jnp.minimum(t + 1, last)
            for cp in in_copies(nxt, 1 - q):
                cp.start()
            y_v = sets[q][3]

            @pl.when(jnp.logical_not(first))
            def _():
                y_copy(t, q).wait()

            @pl.loop(0, steps)
            def _(j):
                y_v[pl.ds(j * SC_LANES, SC_LANES)] = jnp.zeros((SC_LANES,), F32)

            @pl.loop(0, n_chunks // 2)
            def _(p):
                c0 = p * 2
                gather(c0 + 1, 1, q).start()
                gather(c0, 0, q).wait()
                compute(c0, 0, q)

                @pl.when(p + 1 < n_chunks // 2)
                def _():
                    gather(c0 + 2, 0, q).start()

                gather(c0 + 1, 1, q).wait()
                compute(c0 + 1, 1, q)

            for cp in in_copies(nxt, 1 - q):
                cp.wait()
            gather(0, 0, 1 - q).start()
            y_copy(t, q).start()

        for cp in in_copies(base, 0):
            cp.start()
        for cp in in_copies(base, 0):
            cp.wait()
        gather(0, 0, 0).start()

        @pl.loop(0, per_worker // 2)
        def _(i):
            token(base + 2 * i, 0, i == 0)
            token(base + 2 * i + 1, 1, i == 0)

        gather(0, 0, 0).wait()
        y_copy(last - 1, 0).wait()
        y_copy(last, 1).wait()

    vec = lambda n_el, dt: [pltpu.VMEM((n_el,), dt), pltpu.VMEM((n_el,), dt)]
    return pl.kernel(
        body, out_type=jax.ShapeDtypeStruct((n, D_MODEL), F32), mesh=mesh,
        scratch_types=vec(PICKS, I32) + vec(PICKS, F32) + vec(D_MODEL, F32) + vec(D_MODEL, F32)
        + [pltpu.VMEM((SC_CHUNK, D_MODEL), F32) for _ in range(4)]
        + [pltpu.SemaphoreType.DMA] * 6,
        compiler_params=pltpu.CompilerParams(needs_layout_passes=False),
        cost_estimate=pl.CostEstimate(flops=4 * n * PICKS * D_MODEL, transcendentals=n * PICKS,
                                      bytes_accessed=4 * n * (PICKS * 2 * D_MODEL + 2 * PICKS + 2 * D_MODEL)),
        name="experts_sc",
    )(u_tab, v_tab, idx, gate, h)


def _finish_body(x_ref, y_ref, gf_ref, *rest_and_out):
    rest_and_out[-1][...] = _rmsnorm(x_ref[...] + y_ref[...], gf_ref[...])


def _finish(x, y, g_final, rest, tm, first_row=0):
    first_block = first_row // tm
    row = pl.BlockSpec((tm, D_MODEL), lambda i: (i + first_block, 0))
    extra = () if rest is None else (rest,)
    return pl.pallas_call(
        _finish_body,
        grid=(y.shape[0] // tm,),
        in_specs=[row, pl.BlockSpec((tm, D_MODEL), lambda i: (i, 0)), pl.BlockSpec(g_final.shape, lambda i: (0, 0))]
        + [pl.BlockSpec(memory_space=pl.ANY) for _ in extra],
        out_specs=row,
        out_shape=jax.ShapeDtypeStruct(x.shape if rest is not None else y.shape, F32),
        input_output_aliases={3: 0} if rest is not None else {},
        compiler_params=_params(("parallel",)),
        name="finish",
    )(x, y, g_final, *extra)


EXP_TOKENS = 256
EXP_SLOTS = 4


def _expert_body(x_ref, g_ref, gf_ref, idx_hbm, gate_ref, tab_hbm, o_ref,
                 idx_smem, buf0, buf1, buf2, buf3, h_buf, y_buf, sems, idx_sem):
    bufs = (buf0, buf1, buf2, buf3)
    ahead = EXP_SLOTS - 1
    i = pl.program_id(0)
    idx_copy = pltpu.make_async_copy(idx_hbm.at[pl.ds(i * (EXP_TOKENS * PICKS), EXP_TOKENS * PICKS)],
                                     idx_smem, idx_sem)
    idx_copy.start()
    h_buf[...] = _rmsnorm(x_ref[...], g_ref[...])
    idx_copy.wait()

    def issue(t, slot):
        for k in range(PICKS):
            e = idx_smem[t * PICKS + k]
            pltpu.async_copy(tab_hbm.at[e], bufs[slot].at[pl.ds(k, 1), :], sems.at[slot], priority=k % 2)

    def wait_slot(slot):
        pltpu.make_async_copy(bufs[(slot + 1) % EXP_SLOTS], bufs[slot], sems.at[slot]).wait()

    lane = lax.broadcasted_iota(I32, (PICKS, EXP_TOKENS), 1)

    def compute(t, slot):
        h = h_buf[pl.ds(t, 1), :]
        u = lax.bitcast_convert_type(bufs[slot][...] & jnp.int32(-65536), F32)
        s = jnp.sum(u * h, axis=-1, keepdims=True)
        gate = jnp.sum(jnp.where(lane == t, gate_ref[...], 0.0), axis=-1, keepdims=True)
        w = gate * jax.nn.gelu(s)
        v = lax.bitcast_convert_type(bufs[slot][...] << 16, F32)
        y_buf[pl.ds(t, 1), :] = jnp.sum(w * v, axis=0, keepdims=True)

    def group(t0, issue_upto):
        for s in range(EXP_SLOTS):
            wait_slot(s)
            if s < issue_upto:
                issue(t0 + s + ahead, (s + ahead) % EXP_SLOTS)
            compute(t0 + s, s)

    for s in range(ahead):
        issue(s, s)

    def steady(q, carry):
        group(q * EXP_SLOTS, EXP_SLOTS)
        return carry

    n_groups = EXP_TOKENS // EXP_SLOTS
    lax.fori_loop(0, n_groups - 1, steady, 0)
    group((n_groups - 1) * EXP_SLOTS, EXP_SLOTS - ahead)
    o_ref[...] = _rmsnorm(x_ref[...] + y_buf[...], gf_ref[...])


def _experts(x, g, g_final, idx_flat, gate_t, table, first_token):
    n = x.shape[0]
    first_block = first_token // EXP_TOKENS
    full = lambda a: pl.BlockSpec(a.shape, lambda i: (0, 0))
    row = pl.BlockSpec((EXP_TOKENS, D_MODEL), lambda i: (i + first_block, 0))
    return pl.pallas_call(
        _expert_body,
        grid=(n // EXP_TOKENS - first_block,),
        in_specs=[row, full(g), full(g_final),
                  pl.BlockSpec(memory_space=pl.ANY),
                  pl.BlockSpec((PICKS, EXP_TOKENS), lambda i: (0, i)),
                  pl.BlockSpec(memory_space=pl.ANY)],
        out_specs=row,
        out_shape=jax.ShapeDtypeStruct((n, D_MODEL), F32),
        scratch_shapes=[pltpu.SMEM((EXP_TOKENS * PICKS,), I32)]
        + [pltpu.VMEM((PICKS, D_MODEL), I32) for _ in range(EXP_SLOTS)]
        + [pltpu.VMEM((EXP_TOKENS, D_MODEL), F32),
           pltpu.VMEM((EXP_TOKENS, D_MODEL), F32),
           pltpu.SemaphoreType.DMA((EXP_SLOTS,)),
           pltpu.SemaphoreType.DMA(())],
        compiler_params=pltpu.CompilerParams(dimension_semantics=("arbitrary",), vmem_limit_bytes=VMEM_LIMIT,
                                             disable_bounds_checks=True),
        name="experts",
    )(x, g, g_final, idx_flat, gate_t, table)


def _attention_masks(tq):
    i = jnp.arange(tq, dtype=I32)[:, None]
    r = jnp.arange(ATT_KEYS, dtype=I32)[None, :]
    negdist = -jnp.abs(WINDOW + i - r).astype(F32)
    return negdist, i, r


def _layer(x, n_groups, rows, swa_inputs, ret_state, ret_len, w, tm, sc_bounds=(0,), tie=None):
    (norm_mix, w_in, b_gate, sinks, wa, wb, wo, norm_ffn, wq_t, k1, k2, (packed, u_tab, v_tab), norm_final) = w
    assert sc_bounds[-1] < x.shape[0]
    qa, ka, va, qb, kb, vb, rg, ga, gb = _project(x, norm_mix, w_in, b_gate, tm)
    oa, k_rows, v_rows = swa_inputs(qa, ka, va, sinks)
    ob, s_fin = _retention(qb, kb, vb, rg, ret_state, n_groups, rows // ret_len, ret_len)
    x1 = _merge(x, oa, ob, ga, gb, wa, wb, wo, tm)
    n_sc = sc_bounds[-1]
    y_sc = []
    if n_sc:
        idx_t, gate_t, h2 = _route(x1, norm_ffn, wq_t, k1, k2, tm, 0, n_sc)
        y_sc = [(lo, _sc_experts(u_tab, v_tab, idx_t[:, lo:hi].T, gate_t[:, lo:hi].T, h2, lo))
                for lo, hi in zip(sc_bounds[:-1], sc_bounds[1:])]
    idx_t, gate_t, _ = _route(x1, norm_ffn, wq_t, k1, k2, tm, n_sc, x.shape[0] - n_sc)
    if tie is not None:
        tie, gate_t = lax.optimization_barrier((tie, gate_t))
    y_tc = _experts(x1, norm_ffn, norm_final, idx_t.T.reshape(-1), gate_t, packed, n_sc)
    return (x1, y_sc, y_tc), k_rows, v_rows, s_fin, tie


def _layer_output(parts, norm_final, tm):
    x1, y_sc, y = parts
    for first_row, y_part in y_sc:
        y = _finish(x1, y_part, norm_final, y, tm, first_row)
    return y


def kernel(x_prompt, x_sample, cache_swa_k, cache_swa_v, state_ret, norm_mix, w_in, b_gate, attn_sinks,
           w_branch_a, w_branch_b, w_out, norm_ffn, peer_w_q, peer_sub_k1, peer_sub_k2, peer_u, peer_v,
           norm_final):
    batch, seq, _ = x_prompt.shape
    dec_batch, dec_seq, _ = x_sample.shape
    assert norm_mix.shape[0] == 1, "single-layer trunk"
    half = lambda a: lax.bitcast_convert_type(a.astype(BF16), jnp.uint16).astype(jnp.uint32)
    packed = lax.bitcast_convert_type((half(peer_u[0]) << 16) | half(peer_v[0]), I32)[:, None, :]
    packed, x_prompt, x_sample = lax.optimization_barrier((packed, x_prompt, x_sample))
    table = (packed, peer_u[0], peer_v[0])
    w = (norm_mix[0][None], w_in[0].astype(BF16), b_gate[0][None], attn_sinks[0],
         w_branch_a[0].astype(BF16), w_branch_b[0].astype(BF16), w_out[0].astype(BF16),
         norm_ffn[0][None], peer_w_q[0].T.astype(BF16), peer_sub_k1[0].astype(BF16),
         peer_sub_k2[0].astype(BF16), table, norm_final[None])

    def swa_prompt(qa, ka, va, sinks):
        streams = qa.shape[0] // seq
        tq = 2 * CHUNK
        negdist, i, r = _attention_masks(tq)
        lq, lk = i // CHUNK, r // CHUNK
        band = (lk >= lq) & (lk <= lq + 2)
        allow = jnp.stack([band & (r >= WINDOW), band]).astype(F32)
        oa = _swa(sinks, qa, ka, va, negdist, allow, streams, seq // tq, tq)
        tail = lambda a: a.reshape(streams, seq, N_KV_A, HEAD_DIM_A)[:, seq - WINDOW:]
        return oa, tail(ka), tail(va)

    def swa_sample(qa, ka, va, sinks):
        negdist, i, r = _attention_masks(dec_seq)
        visible = jnp.broadcast_to(r < WINDOW + dec_seq, (dec_seq, ATT_KEYS))
        allow = jnp.stack([visible, visible]).astype(F32)
        pad = jnp.zeros((dec_batch, ATT_KEYS - WINDOW - dec_seq, KV_A), F32)
        k_all = jnp.concatenate([cache_swa_k[0].reshape(dec_batch, WINDOW, KV_A),
                                 ka.reshape(dec_batch, dec_seq, KV_A), pad], axis=1)
        v_all = jnp.concatenate([cache_swa_v[0].reshape(dec_batch, WINDOW, KV_A),
                                 va.reshape(dec_batch, dec_seq, KV_A), pad], axis=1)
        oa = _swa(sinks, qa, k_all.reshape(-1, KV_A), v_all.reshape(-1, KV_A), negdist, allow,
                  dec_batch, 1, dec_seq)
        tail = lambda a: a[:, dec_seq:WINDOW + dec_seq].reshape(dec_batch, WINDOW, N_KV_A, HEAD_DIM_A)
        return oa, tail(k_all), tail(v_all)

    parts, ks, vs, ss, _ = _layer(x_sample.reshape(dec_batch * dec_seq, D_MODEL), dec_batch, dec_seq,
                                  swa_sample, state_ret[0], dec_seq, w, 128)
    ys = _layer_output(parts, w[-1], 128)

    n_grp = len(SC_GROUP_BOUNDS)
    per_group = batch // n_grp
    xs = [x_prompt[g * per_group:(g + 1) * per_group].reshape(per_group * seq, D_MODEL) for g in range(n_grp)]
    s0 = jnp.zeros((per_group, N_HEADS_B, DK_B, DV_B), F32)
    outs, tie = [], None
    for g, bounds in enumerate(SC_GROUP_BOUNDS):
        (x1, y_sc, y_tc), kg, vg, sg, tie = _layer(xs[g], per_group, seq, swa_prompt, s0, 256, w, 256, bounds, tie)
        if tie is not None:
            (px1, py_sc, py_tc), pk, pv, ps = outs[-1]
            outs[-1] = ((px1, [(py_sc[0][0], tie)] + py_sc[1:], py_tc), pk, pv, ps)
            tie = None
        if g + 1 < n_grp:
            y_tc, xs[g + 1] = lax.optimization_barrier((y_tc, xs[g + 1]))
            tie = y_sc[0][1]
        outs.append(((x1, y_sc, y_tc), kg, vg, sg))
    outs = [(_layer_output(p, w[-1], 256), kg, vg, sg) for p, kg, vg, sg in outs]
    yp, kp, vp, sp = (jnp.concatenate(parts, axis=0) for parts in zip(*outs))

    return (yp.reshape(batch, seq, D_MODEL), ys.reshape(dec_batch, dec_seq, D_MODEL),
            kp[None], vp[None], sp[None], ks[None], vs[None], ss[None])
```

```python
import functools

import jax
import jax.numpy as jnp
from jax import lax
from jax.experimental import pallas as pl
from jax.experimental.pallas import tpu as pltpu
from jax.experimental.pallas import tpu_sc as plsc

F32 = jnp.float32
BF16 = jnp.bfloat16
I32 = jnp.int32

D_MODEL = 1024
CHUNK = 64
EPS = 1e-6
NEG_INF = -1e30
PAST_LEN = 2048

N_HEADS_A = 8
N_KV_A = 2
GROUP_A = N_HEADS_A // N_KV_A
HEAD_DIM_A = 64
WINDOW = 128
N_HEADS_B = 4
DK_B = 128
DV_B = 256
Q_A = N_HEADS_A * HEAD_DIM_A
KV_A = N_KV_A * HEAD_DIM_A
Q_B = N_HEADS_B * DK_B
V_B = N_HEADS_B * DV_B
D_IN = Q_A + 2 * KV_A + 2 * Q_B + 2 * V_B + 2 * D_MODEL
OFF_QA = 0
OFF_KA = OFF_QA + Q_A
OFF_VA = OFF_KA + KV_A
OFF_QB = OFF_VA + KV_A
OFF_KB = OFF_QB + Q_B
OFF_VB = OFF_KB + Q_B
OFF_RG = OFF_VB + V_B
OFF_GA = OFF_RG + V_B
OFF_GB = OFF_GA + D_MODEL

N_KEYS = 128
N_EXPERTS = N_KEYS * N_KEYS
PEER_HEADS = 8
PEER_QDIM = 256
PEER_HALF = PEER_QDIM // 2
PEER_TOPK = 16
PICKS = PEER_HEADS * PEER_TOPK

LANES = 128
ATT_KEYS = 2 * WINDOW
VMEM_LIMIT = 56 * 1024 * 1024


def _rmsnorm(x, g):
    return x * lax.rsqrt(jnp.mean(x * x, axis=-1, keepdims=True) + EPS) * g


def _dot(a, b):
    return jnp.dot(a, b, preferred_element_type=F32)


def _dot_nt(a, b):
    return lax.dot_general(a, b, (((1,), (1,)), ((), ())), preferred_element_type=F32)


def _dot_tn(a, b):
    return lax.dot_general(a, b, (((0,), (0,)), ((), ())), preferred_element_type=F32)


def _params(sem):
    return pltpu.CompilerParams(dimension_semantics=sem, vmem_limit_bytes=VMEM_LIMIT)


def _proj_body(x_ref, g_ref, w_ref, bg_ref, qa_ref, ka_ref, va_ref, qb_ref, kb_ref, vb_ref,
               rg_ref, ga_ref, gb_ref):
    h = _rmsnorm(x_ref[...], g_ref[...]).astype(BF16)

    def mm(lo, width):
        return _dot(h, w_ref[:, lo:lo + width])

    qa_ref[...] = (mm(OFF_QA, Q_A) * (HEAD_DIM_A ** -0.5)).astype(BF16)
    ka_ref[...] = mm(OFF_KA, KV_A)
    va_ref[...] = mm(OFF_VA, KV_A)
    qb_ref[...] = mm(OFF_QB, Q_B).astype(BF16)
    kb_ref[...] = (mm(OFF_KB, Q_B) * (DK_B ** -0.5)).astype(BF16)
    vb_ref[...] = mm(OFF_VB, V_B).astype(BF16)
    rg_ref[...] = mm(OFF_RG, V_B)
    ga_ref[...] = mm(OFF_GA, D_MODEL) + bg_ref[:, :D_MODEL]
    gb_ref[...] = mm(OFF_GB, D_MODEL) + bg_ref[:, D_MODEL:]


def _project(x, g, w_bf16, b_gate, tm):
    n = x.shape[0]
    row = lambda width: pl.BlockSpec((tm, width), lambda i: (i, 0))
    full = lambda a: pl.BlockSpec(a.shape, lambda i: (0, 0))
    widths = (Q_A, KV_A, KV_A, Q_B, Q_B, V_B, V_B, D_MODEL, D_MODEL)
    dtypes = (BF16, F32, F32, BF16, BF16, BF16, F32, F32, F32)
    return pl.pallas_call(
        _proj_body,
        grid=(n // tm,),
        in_specs=[row(D_MODEL), full(g), full(w_bf16), full(b_gate)],
        out_specs=[row(w) for w in widths],
        out_shape=[jax.ShapeDtypeStruct((n, w), d) for w, d in zip(widths, dtypes)],
        compiler_params=_params(("parallel",)),
        name="projection",
    )(x, g, w_bf16, b_gate)


def _swa_body(sink_ref, q_ref, kp_ref, kc_ref, vp_ref, vc_ref, nd_ref, al_ref, o_ref):
    k = jnp.concatenate([kp_ref[...], kc_ref[...]], axis=0)
    v = jnp.concatenate([vp_ref[...], vc_ref[...]], axis=0)
    lane = lax.broadcasted_iota(I32, k.shape, 1)
    low = lane < HEAD_DIM_A
    k_sw = pltpu.roll(k, HEAD_DIM_A, axis=1)
    v_sw = pltpu.roll(v, HEAD_DIM_A, axis=1)
    negdist = nd_ref[...]
    allowed = al_ref[0] > 0.5
    for kv in range(N_KV_A):
        own = low if kv == 0 else jnp.logical_not(low)
        k_rep = jnp.where(own, k, k_sw)
        v_rep = jnp.where(own, v, v_sw)
        zero = jnp.zeros_like(k_rep)
        kk = jnp.concatenate([jnp.where(low, k_rep, zero), jnp.where(low, zero, k_rep)], axis=0).astype(BF16)
        vv = jnp.concatenate([jnp.where(low, v_rep, zero), jnp.where(low, zero, v_rep)], axis=0).astype(BF16)
        for a in range(GROUP_A // 2):
            col = kv * GROUP_A * HEAD_DIM_A + a * LANES
            s2 = _dot_nt(q_ref[:, col:col + LANES], kk)
            ps = []
            for u in range(2):
                head = kv * GROUP_A + 2 * a + u
                slope = 2.0 ** (-8.0 * (head + 1) / N_HEADS_A)
                sink = sink_ref[head]
                s = s2[:, u * ATT_KEYS:(u + 1) * ATT_KEYS] + slope * negdist
                s = jnp.where(allowed, s, NEG_INF)
                m = jnp.maximum(jnp.max(s, axis=-1, keepdims=True), sink)
                p = jnp.exp(s - m)
                denom = jnp.sum(p, axis=-1, keepdims=True) + jnp.exp(sink - m)
                ps.append((p / denom).astype(BF16))
            o = _dot(jnp.concatenate(ps, axis=1), vv)
            o_ref[:, col:col + LANES] = o.astype(BF16)


def _swa(sinks, q, k, v, negdist, allow, groups, q_blocks, tq):
    n = q.shape[0]
    kblocks = k.shape[0] // LANES // groups
    shift = kblocks - q_blocks
    prev = lambda b, j: (b * kblocks + jnp.maximum(j + shift - 1, 0), 0)
    cur = lambda b, j: (b * kblocks + j + shift, 0)
    kv_spec = lambda f: pl.BlockSpec((LANES, KV_A), f)
    return pl.pallas_call(
        _swa_body,
        grid=(groups, q_blocks),
        in_specs=[pl.BlockSpec(memory_space=pltpu.SMEM),
                  pl.BlockSpec((tq, Q_A), lambda b, j: (b * q_blocks + j, 0)),
                  kv_spec(prev), kv_spec(cur), kv_spec(prev), kv_spec(cur),
                  pl.BlockSpec((tq, ATT_KEYS), lambda b, j: (0, 0)),
                  pl.BlockSpec((1, tq, ATT_KEYS), lambda b, j: (jnp.minimum(j + shift, 1), 0, 0))],
        out_specs=pl.BlockSpec((tq, Q_A), lambda b, j: (b * q_blocks + j, 0)),
        out_shape=jax.ShapeDtypeStruct((n, Q_A), BF16),
        compiler_params=_params(("parallel", "parallel")),
        name="attention",
    )(sinks, q, k, k, v, v, negdist, allow)


def _ret_body(q_ref, k_ref, v_ref, rg_ref, s0_ref, intra_ref, rdec_ref, kdec_ref, sdec_ref,
              o_ref, st_ref):
    c = pl.program_id(1)

    @pl.when(c == 0)
    def _():
        st_ref[...] = s0_ref[...]

    for h in range(N_HEADS_B):
        q = q_ref[:, h * DK_B:(h + 1) * DK_B]
        k = k_ref[:, h * DK_B:(h + 1) * DK_B]
        v = v_ref[:, h * DV_B:(h + 1) * DV_B]
        st = st_ref[0, h]
        scores = (_dot_nt(q, k) * intra_ref[h]).astype(BF16)
        o = _dot(scores, v) + _dot(q, st.astype(BF16)) * rdec_ref[h]
        k_dec = (k.astype(F32) * kdec_ref[h]).astype(BF16)
        st_ref[0, h] = sdec_ref[h] * st + _dot_tn(k_dec, v)
        mu = jnp.mean(o, axis=-1, keepdims=True)
        d = o - mu
        var = jnp.mean(d * d, axis=-1, keepdims=True)
        on = d * lax.rsqrt(var + EPS)
        r = rg_ref[:, h * DV_B:(h + 1) * DV_B]
        o_ref[:, h * DV_B:(h + 1) * DV_B] = (on * (r * jax.nn.sigmoid(r))).astype(BF16)


def _retention_consts(length):
    lg = jnp.log1p(-(2.0 ** (-5.0 - jnp.arange(N_HEADS_B, dtype=F32))))
    n = jnp.arange(length, dtype=F32)
    diff = n[:, None] - n[None, :]
    intra = jnp.where(diff >= 0, jnp.exp(jnp.maximum(diff, 0.0) * lg[:, None, None]), 0.0)
    rdec = jnp.exp((n[None, :] + 1.0) * lg[:, None])[..., None]
    kdec = jnp.exp((length - 1.0 - n)[None, :] * lg[:, None])[..., None]
    sdec = jnp.exp(length * lg)
    return intra, rdec, kdec, sdec


def _retention(q, k, v, rg, s0, streams, chunks, length):
    n = q.shape[0]
    intra, rdec, kdec, sdec = _retention_consts(length)
    row = lambda width: pl.BlockSpec((length, width), lambda b, c: (b * chunks + c, 0))
    const = lambda a: pl.BlockSpec(a.shape, lambda b, c: (0,) * a.ndim)
    st_spec = pl.BlockSpec((1, N_HEADS_B, DK_B, DV_B), lambda b, c: (b, 0, 0, 0))
    return pl.pallas_call(
        _ret_body,
        grid=(streams, chunks),
        in_specs=[row(Q_B), row(Q_B), row(V_B), row(V_B), st_spec,
                  const(intra), const(rdec), const(kdec), pl.BlockSpec(memory_space=pltpu.SMEM)],
        out_specs=[row(V_B), st_spec],
        out_shape=[jax.ShapeDtypeStruct((n, V_B), BF16),
                   jax.ShapeDtypeStruct((streams, N_HEADS_B, DK_B, DV_B), F32)],
        compiler_params=_params(("parallel", "arbitrary")),
        name="retention",
    )(q, k, v, rg, s0, intra, rdec, kdec, sdec)


def _merge_body(x_ref, oa_ref, ob_ref, ga_ref, gb_ref, wa_ref, wb_ref, wo_ref, o_ref):
    ya = _dot(oa_ref[...], wa_ref[...])
    yb = _dot(ob_ref[...], wb_ref[...])
    mix = jax.nn.sigmoid(ga_ref[...]) * ya + jax.nn.sigmoid(gb_ref[...]) * yb
    o_ref[...] = x_ref[...] + _dot(mix.astype(BF16), wo_ref[...])


def _merge(x, oa, ob, ga, gb, wa, wb, wo, tm):
    n = x.shape[0]
    row = lambda width: pl.BlockSpec((tm, width), lambda i: (i, 0))
    full = lambda a: pl.BlockSpec(a.shape, lambda i: (0, 0))
    return pl.pallas_call(
        _merge_body,
        grid=(n // tm,),
        in_specs=[row(D_MODEL), row(Q_A), row(V_B), row(D_MODEL), row(D_MODEL), full(wa), full(wb), full(wo)],
        out_specs=row(D_MODEL),
        out_shape=jax.ShapeDtypeStruct((n, D_MODEL), F32),
        compiler_params=_params(("parallel",)),
        name="merge",
    )(x, oa, ob, ga, gb, wa, wb, wo)


def _top16(s, order):
    t = s.shape[1]
    rank = lax.broadcasted_iota(I32, (PEER_TOPK, t), 0)
    vals = jnp.zeros((PEER_TOPK, t), F32)
    ids = jnp.zeros((PEER_TOPK, t), F32)
    for r in range(PEER_TOPK):
        m = jnp.max(s, axis=0, keepdims=True)
        pick = jnp.min(jnp.where(s == m, order, 3e38), axis=0, keepdims=True)
        s = jnp.where(order == pick, -jnp.inf, s)
        vals = jnp.where(rank == r, m, vals)
        ids = jnp.where(rank == r, pick, ids)
    return vals, ids


def _route_body(x_ref, g_ref, wq_ref, k1_ref, k2_ref, idx_ref, gate_ref, h_ref):
    h_ref[...] = _rmsnorm(x_ref[...], g_ref[...])
    h = h_ref[...].astype(BF16)
    t = h.shape[0]
    q_t = _dot_nt(wq_ref[...], h).astype(BF16)
    iota = lambda rows: lax.broadcasted_iota(I32, (rows, t), 0).astype(F32)
    key_id = iota(N_KEYS)
    flat = jnp.concatenate(
        [iota(PEER_TOPK)] + [a * PEER_TOPK + iota(8) for a in range(1, 8)] + [(8 + iota(8)) * PEER_TOPK],
        axis=0)
    rank = lax.broadcasted_iota(I32, (PEER_TOPK, t), 0)
    idx_rows, gate_rows = [], []
    for hd in range(PEER_HEADS):
        base = hd * PEER_QDIM
        s1 = _dot(k1_ref[...], q_t[base:base + PEER_HALF])
        s2 = _dot(k2_ref[...], q_t[base + PEER_HALF:base + PEER_QDIM])
        v1, i1 = _top16(s1, key_id)
        v2, i2 = _top16(s2, key_id)
        cand = jnp.concatenate(
            [v1[0:1] + v2]
            + [v1[a:a + 1] + v2[0:8] for a in range(1, 8)]
            + [v1[8:16] + v2[0:1]], axis=0)
        cidx = jnp.concatenate(
            [i1[0:1] * N_KEYS + i2]
            + [i1[a:a + 1] * N_KEYS + i2[0:8] for a in range(1, 8)]
            + [i1[8:16] * N_KEYS + i2[0:1]], axis=0)
        best, pos = _top16(cand, flat)
        eidx = jnp.zeros((PEER_TOPK, t), F32)
        for r in range(PEER_TOPK):
            picked = jnp.max(jnp.where(flat == pos[r:r + 1], cidx, -1.0), axis=0, keepdims=True)
            eidx = jnp.where(rank == r, picked, eidx)
        e = jnp.exp(best - best[0:1])
        gate_rows.append(e / jnp.sum(e, axis=0, keepdims=True))
        idx_rows.append(eidx.astype(I32))
    idx_ref[...] = jnp.concatenate(idx_rows, axis=0)
    gate_ref[...] = jnp.concatenate(gate_rows, axis=0)


def _route(x, g, wq_t, k1, k2, tm, first_row, n):
    first_block = first_row // tm
    full = lambda a: pl.BlockSpec(a.shape, lambda i: (0, 0))
    col = pl.BlockSpec((PICKS, tm), lambda i: (0, i))
    return pl.pallas_call(
        _route_body,
        grid=(n // tm,),
        in_specs=[pl.BlockSpec((tm, D_MODEL), lambda i: (i + first_block, 0)), full(g), full(wq_t), full(k1),
                  full(k2)],
        out_specs=[col, col, pl.BlockSpec((tm, D_MODEL), lambda i: (i, 0))],
        out_shape=[jax.ShapeDtypeStruct((PICKS, n), I32), jax.ShapeDtypeStruct((PICKS, n), F32),
                   jax.ShapeDtypeStruct((n, D_MODEL), F32)],
        compiler_params=_params(("parallel",)),
        name="routing",
    )(x, g, wq_t, k1, k2)


SC_CORES = 2
SC_SUBCORES = 16
SC_LANES = 16
SC_CHUNK = 16
SC_PROMPT_ROWS = 18432


def _gelu_via_exp(x):
    z = 0.7978845608028654 * (x + 0.044715 * x * x * x)
    return 0.5 * x * (2.0 - 2.0 / (1.0 + jnp.exp(2.0 * z)))


class _Copies:
    def __init__(self, copies):
        self.copies = copies

    def start(self):
        for cp in self.copies:
            cp.start()

    def wait(self):
        for cp in self.copies:
            cp.wait()


def _sc_experts(u_tab, v_tab, idx, gate, h, first):
    n = idx.shape[0]
    workers = SC_CORES * SC_SUBCORES
    per_worker = n // workers
    n_chunks = PICKS // SC_CHUNK
    steps = D_MODEL // SC_LANES
    mesh = plsc.VectorSubcoreMesh(core_axis_name="c", subcore_axis_name="s",
                                  num_cores=SC_CORES, num_subcores=SC_SUBCORES)

    def body(u_hbm, v_hbm, idx_hbm, gate_hbm, h_hbm, y_hbm,
             idx_0, idx_1, gate_0, gate_1, h_0, h_1, y_0, y_1, u_rows_a, u_rows_b, v_rows_a, v_rows_b,
             in_sem_0, in_sem_1, y_sem_0, y_sem_1, sem_a, sem_b):
        base = (lax.axis_index("s") * SC_CORES + lax.axis_index("c")) * per_worker
        last = base + per_worker - 1
        sets = ((idx_0, gate_0, h_0, y_0, in_sem_0, y_sem_0), (idx_1, gate_1, h_1, y_1, in_sem_1, y_sem_1))
        bufs = ((u_rows_a, v_rows_a, sem_a), (u_rows_b, v_rows_b, sem_b))

        def in_copies(t, q):
            idx_v, gate_v, h_v, _, sem, _ = sets[q]
            return (pltpu.make_async_copy(idx_hbm.at[t], idx_v, sem),
                    pltpu.make_async_copy(gate_hbm.at[t], gate_v, sem),
                    pltpu.make_async_copy(h_hbm.at[t + first], h_v, sem))

        def y_copy(t, q):
            return pltpu.make_async_copy(sets[q][3], y_hbm.at[t], sets[q][5])

        def gather(c, b, q):
            u_rows, v_rows, sem = bufs[b]
            picks = sets[q][0].at[pl.ds(c * SC_CHUNK, SC_CHUNK)]
            return _Copies((pltpu.make_async_copy(u_hbm.at[picks], u_rows, sem),
                            pltpu.make_async_copy(v_hbm.at[picks], v_rows, sem)))

        def compute(c, b, q):
            u_rows, v_rows, _ = bufs[b]
            _, gate_v, h_v, y_v, _, _ = sets[q]

            def dot_step(j, accs):
                hj = h_v[pl.ds(j * SC_LANES, SC_LANES)]
                return tuple(accs[k] + u_rows[k, pl.ds(j * SC_LANES, SC_LANES)] * hj for k in range(SC_CHUNK))

            accs = plsc.parallel_loop(0, steps, carry=tuple(jnp.zeros((SC_LANES,), F32) for _ in range(SC_CHUNK)))(
                dot_step)
            lane = lax.iota(I32, SC_LANES)
            s = jnp.zeros((SC_LANES,), F32)
            for k in range(SC_CHUNK):
                s = jnp.where(lane == k, jnp.sum(accs[k]), s)
            w = _gelu_via_exp(s) * gate_v[pl.ds(c * SC_CHUNK, SC_CHUNK)]
            ws = [jnp.full((SC_LANES,), jnp.sum(jnp.where(lane == k, w, 0.0)), F32) for k in range(SC_CHUNK)]

            @plsc.parallel_loop(0, steps)
            def _(j):
                cols = pl.ds(j * SC_LANES, SC_LANES)
                terms = [ws[k] * v_rows[k, cols] for k in range(SC_CHUNK)]
                while len(terms) > 1:
                    terms = [terms[i] + terms[i + 1] for i in range(0, len(terms), 2)]
                y_v[cols] = y_v[cols] + terms[0]

        def token(t, q, first):
            nxt = jnp.minimum(t + 1, last)
            for cp in in_copies(nxt, 1 - q):
                cp.start()
            y_v = sets[q][3]

            @pl.when(jnp.logical_not(first))
            def _():
                y_copy(t, q).wait()

            @pl.loop(0, steps)
            def _(j):
                y_v[pl.ds(j * SC_LANES, SC_LANES)] = jnp.zeros((SC_LANES,), F32)

            @pl.loop(0, n_chunks // 2)
            def _(p):
                c0 = p * 2
                gather(c0 + 1, 1, q).start()
                gather(c0, 0, q).wait()
                compute(c0, 0, q)

                @pl.when(p + 1 < n_chunks // 2)
                def _():
                    gather(c0 + 2, 0, q).start()

                gather(c0 + 1, 1, q).wait()
                compute(c0 + 1, 1, q)

            for cp in in_copies(nxt, 1 - q):
                cp.wait()
            gather(0, 0, 1 - q).start()
            y_copy(t, q).start()

        for cp in in_copies(base, 0):
            cp.start()
        for cp in in_copies(base, 0):
            cp.wait()
        gather(0, 0, 0).start()

        @pl.loop(0, per_worker // 2)
        def _(i):
            token(base + 2 * i, 0, i == 0)
            token(base + 2 * i + 1, 1, i == 0)

        gather(0, 0, 0).wait()
        y_copy(last - 1, 0).wait()
        y_copy(last, 1).wait()

    vec = lambda n_el, dt: [pltpu.VMEM((n_el,), dt), pltpu.VMEM((n_el,), dt)]
    return pl.kernel(
        body, out_type=jax.ShapeDtypeStruct((n, D_MODEL), F32), mesh=mesh,
        scratch_types=vec(PICKS, I32) + vec(PICKS, F32) + vec(D_MODEL, F32) + vec(D_MODEL, F32)
        + [pltpu.VMEM((SC_CHUNK, D_MODEL), F32) for _ in range(4)]
        + [pltpu.SemaphoreType.DMA] * 6,
        compiler_params=pltpu.CompilerParams(needs_layout_passes=False),
        name="experts_sc",
    )(u_tab, v_tab, idx, gate, h)


def _finish_body(x_ref, y_ref, gf_ref, *rest_and_out):
    rest_and_out[-1][...] = _rmsnorm(x_ref[...] + y_ref[...], gf_ref[...])


def _finish(x, y, g_final, rest, tm, first_row=0):
    first_block = first_row // tm
    row = pl.BlockSpec((tm, D_MODEL), lambda i: (i + first_block, 0))
    extra = () if rest is None else (rest,)
    return pl.pallas_call(
        _finish_body,
        grid=(y.shape[0] // tm,),
        in_specs=[row, pl.BlockSpec((tm, D_MODEL), lambda i: (i, 0)), pl.BlockSpec(g_final.shape, lambda i: (0, 0))]
        + [pl.BlockSpec(memory_space=pl.ANY) for _ in extra],
        out_specs=row,
        out_shape=jax.ShapeDtypeStruct(x.shape if rest is not None else y.shape, F32),
        input_output_aliases={3: 0} if rest is not None else {},
        compiler_params=_params(("parallel",)),
        name="finish",
    )(x, y, g_final, *extra)


EXP_TOKENS = 256
EXP_SLOTS = 4


def _expert_body(x_ref, g_ref, gf_ref, idx_hbm, gate_ref, tab_hbm, o_ref,
                 idx_smem, buf0, buf1, buf2, buf3, h_buf, y_buf, sems, idx_sem):
    bufs = (buf0, buf1, buf2, buf3)
    ahead = EXP_SLOTS - 1
    i = pl.program_id(0)
    idx_copy = pltpu.make_async_copy(idx_hbm.at[pl.ds(i * (EXP_TOKENS * PICKS), EXP_TOKENS * PICKS)],
                                     idx_smem, idx_sem)
    idx_copy.start()
    h_buf[...] = _rmsnorm(x_ref[...], g_ref[...])
    idx_copy.wait()

    def issue(t, slot):
        for k in range(PICKS):
            e = idx_smem[t * PICKS + k]
            pltpu.async_copy(tab_hbm.at[e], bufs[slot].at[pl.ds(k, 1), :], sems.at[slot], priority=k % 2)

    def wait_slot(slot):
        pltpu.make_async_copy(bufs[(slot + 1) % EXP_SLOTS], bufs[slot], sems.at[slot]).wait()

    lane = lax.broadcasted_iota(I32, (PICKS, EXP_TOKENS), 1)

    def compute(t, slot):
        h = h_buf[pl.ds(t, 1), :]
        u = lax.bitcast_convert_type(bufs[slot][...] & jnp.int32(-65536), F32)
        s = jnp.sum(u * h, axis=-1, keepdims=True)
        gate = jnp.sum(jnp.where(lane == t, gate_ref[...], 0.0), axis=-1, keepdims=True)
        w = gate * jax.nn.gelu(s)
        v = lax.bitcast_convert_type(bufs[slot][...] << 16, F32)
        y_buf[pl.ds(t, 1), :] = jnp.sum(w * v, axis=0, keepdims=True)

    def group(t0, issue_upto):
        for s in range(EXP_SLOTS):
            wait_slot(s)
            if s < issue_upto:
                issue(t0 + s + ahead, (s + ahead) % EXP_SLOTS)
            compute(t0 + s, s)

    for s in range(ahead):
        issue(s, s)

    def steady(q, carry):
        group(q * EXP_SLOTS, EXP_SLOTS)
        return carry

    n_groups = EXP_TOKENS // EXP_SLOTS
    lax.fori_loop(0, n_groups - 1, steady, 0)
    group((n_groups - 1) * EXP_SLOTS, EXP_SLOTS - ahead)
    o_ref[...] = _rmsnorm(x_ref[...] + y_buf[...], gf_ref[...])


def _experts(x, g, g_final, idx_flat, gate_t, table, first_token):
    n = x.shape[0]
    first_block = first_token // EXP_TOKENS
    full = lambda a: pl.BlockSpec(a.shape, lambda i: (0, 0))
    row = pl.BlockSpec((EXP_TOKENS, D_MODEL), lambda i: (i + first_block, 0))
    return pl.pallas_call(
        _expert_body,
        grid=(n // EXP_TOKENS - first_block,),
        in_specs=[row, full(g), full(g_final),
                  pl.BlockSpec(memory_space=pl.ANY),
                  pl.BlockSpec((PICKS, EXP_TOKENS), lambda i: (0, i)),
                  pl.BlockSpec(memory_space=pl.ANY)],
        out_specs=row,
        out_shape=jax.ShapeDtypeStruct((n, D_MODEL), F32),
        scratch_shapes=[pltpu.SMEM((EXP_TOKENS * PICKS,), I32)]
        + [pltpu.VMEM((PICKS, D_MODEL), I32) for _ in range(EXP_SLOTS)]
        + [pltpu.VMEM((EXP_TOKENS, D_MODEL), F32),
           pltpu.VMEM((EXP_TOKENS, D_MODEL), F32),
           pltpu.SemaphoreType.DMA((EXP_SLOTS,)),
           pltpu.SemaphoreType.DMA(())],
        compiler_params=pltpu.CompilerParams(dimension_semantics=("arbitrary",), vmem_limit_bytes=VMEM_LIMIT,
                                             disable_bounds_checks=True),
        name="experts",
    )(x, g, g_final, idx_flat, gate_t, table)


def _attention_masks(tq):
    i = jnp.arange(tq, dtype=I32)[:, None]
    r = jnp.arange(ATT_KEYS, dtype=I32)[None, :]
    negdist = -jnp.abs(WINDOW + i - r).astype(F32)
    return negdist, i, r


def _layer(x, n_groups, rows, swa_inputs, ret_state, ret_len, w, tm, n_sc):
    (norm_mix, w_in, b_gate, sinks, wa, wb, wo, norm_ffn, wq_t, k1, k2, (packed, u_tab, v_tab), norm_final) = w
    assert n_sc < x.shape[0]
    qa, ka, va, qb, kb, vb, rg, ga, gb = _project(x, norm_mix, w_in, b_gate, tm)
    oa, k_rows, v_rows = swa_inputs(qa, ka, va, sinks)
    ob, s_fin = _retention(qb, kb, vb, rg, ret_state, n_groups, rows // ret_len, ret_len)
    x1 = _merge(x, oa, ob, ga, gb, wa, wb, wo, tm)
    if n_sc:
        idx_t, gate_t, h2 = _route(x1, norm_ffn, wq_t, k1, k2, tm, 0, n_sc)
        y_sc = _sc_experts(u_tab, v_tab, idx_t.T, gate_t.T, h2, 0)
    idx_t, gate_t, _ = _route(x1, norm_ffn, wq_t, k1, k2, tm, n_sc, x.shape[0] - n_sc)
    y = _experts(x1, norm_ffn, norm_final, idx_t.T.reshape(-1), gate_t, packed, n_sc)
    if n_sc:
        y = _finish(x1, y_sc, norm_final, y, tm)
    return y, k_rows, v_rows, s_fin


def kernel(x_prompt, x_sample, cache_swa_k, cache_swa_v, state_ret, norm_mix, w_in, b_gate, attn_sinks,
           w_branch_a, w_branch_b, w_out, norm_ffn, peer_w_q, peer_sub_k1, peer_sub_k2, peer_u, peer_v,
           norm_final):
    batch, seq, _ = x_prompt.shape
    dec_batch, dec_seq, _ = x_sample.shape
    assert norm_mix.shape[0] == 1, "single-layer trunk"
    half = lambda a: lax.bitcast_convert_type(a.astype(BF16), jnp.uint16).astype(jnp.uint32)
    packed = lax.bitcast_convert_type((half(peer_u[0]) << 16) | half(peer_v[0]), I32)[:, None, :]
    packed, x_prompt, x_sample = lax.optimization_barrier((packed, x_prompt, x_sample))
    table = (packed, peer_u[0], peer_v[0])
    w = (norm_mix[0][None], w_in[0].astype(BF16), b_gate[0][None], attn_sinks[0],
         w_branch_a[0].astype(BF16), w_branch_b[0].astype(BF16), w_out[0].astype(BF16),
         norm_ffn[0][None], peer_w_q[0].T.astype(BF16), peer_sub_k1[0].astype(BF16),
         peer_sub_k2[0].astype(BF16), table, norm_final[None])

    def swa_prompt(qa, ka, va, sinks):
        streams = qa.shape[0] // seq
        tq = 2 * CHUNK
        negdist, i, r = _attention_masks(tq)
        lq, lk = i // CHUNK, r // CHUNK
        band = (lk >= lq) & (lk <= lq + 2)
        allow = jnp.stack([band & (r >= WINDOW), band]).astype(F32)
        oa = _swa(sinks, qa, ka, va, negdist, allow, streams, seq // tq, tq)
        tail = lambda a: a.reshape(streams, seq, N_KV_A, HEAD_DIM_A)[:, seq - WINDOW:]
        return oa, tail(ka), tail(va)

    def swa_sample(qa, ka, va, sinks):
        negdist, i, r = _attention_masks(dec_seq)
        visible = jnp.broadcast_to(r < WINDOW + dec_seq, (dec_seq, ATT_KEYS))
        allow = jnp.stack([visible, visible]).astype(F32)
        pad = jnp.zeros((dec_batch, ATT_KEYS - WINDOW - dec_seq, KV_A), F32)
        k_all = jnp.concatenate([cache_swa_k[0].reshape(dec_batch, WINDOW, KV_A),
                                 ka.reshape(dec_batch, dec_seq, KV_A), pad], axis=1)
        v_all = jnp.concatenate([cache_swa_v[0].reshape(dec_batch, WINDOW, KV_A),
                                 va.reshape(dec_batch, dec_seq, KV_A), pad], axis=1)
        oa = _swa(sinks, qa, k_all.reshape(-1, KV_A), v_all.reshape(-1, KV_A), negdist, allow,
                  dec_batch, 1, dec_seq)
        tail = lambda a: a[:, dec_seq:WINDOW + dec_seq].reshape(dec_batch, WINDOW, N_KV_A, HEAD_DIM_A)
        return oa, tail(k_all), tail(v_all)

    ys, ks, vs, ss = _layer(x_sample.reshape(dec_batch * dec_seq, D_MODEL), dec_batch, dec_seq, swa_sample,
                            state_ret[0], dec_seq, w, 128, 0)
    s0 = jnp.zeros((batch, N_HEADS_B, DK_B, DV_B), F32)
    yp, kp, vp, sp = _layer(x_prompt.reshape(batch * seq, D_MODEL), batch, seq, swa_prompt, s0, 256, w, 256,
                            SC_PROMPT_ROWS)

    return (yp.reshape(batch, seq, D_MODEL), ys.reshape(dec_batch, dec_seq, D_MODEL),
            kp[None], vp[None], sp[None], ks[None], vs[None], ss[None])
```

```python
import functools

import jax
import jax.numpy as jnp
from jax import lax
from jax.experimental import pallas as pl
from jax.experimental.pallas import tpu as pltpu
from jax.experimental.pallas import tpu_sc as plsc

F32 = jnp.float32
BF16 = jnp.bfloat16
I32 = jnp.int32

D_MODEL = 1024
CHUNK = 64
EPS = 1e-6
NEG_INF = -1e30
PAST_LEN = 2048

N_HEADS_A = 8
N_KV_A = 2
GROUP_A = N_HEADS_A // N_KV_A
HEAD_DIM_A = 64
WINDOW = 128
N_HEADS_B = 4
DK_B = 128
DV_B = 256
Q_A = N_HEADS_A * HEAD_DIM_A
KV_A = N_KV_A * HEAD_DIM_A
Q_B = N_HEADS_B * DK_B
V_B = N_HEADS_B * DV_B
D_IN = Q_A + 2 * KV_A + 2 * Q_B + 2 * V_B + 2 * D_MODEL
OFF_QA = 0
OFF_KA = OFF_QA + Q_A
OFF_VA = OFF_KA + KV_A
OFF_QB = OFF_VA + KV_A
OFF_KB = OFF_QB + Q_B
OFF_VB = OFF_KB + Q_B
OFF_RG = OFF_VB + V_B
OFF_GA = OFF_RG + V_B
OFF_GB = OFF_GA + D_MODEL

N_KEYS = 128
N_EXPERTS = N_KEYS * N_KEYS
PEER_HEADS = 8
PEER_QDIM = 256
PEER_HALF = PEER_QDIM // 2
PEER_TOPK = 16
PICKS = PEER_HEADS * PEER_TOPK

LANES = 128
ATT_KEYS = 2 * WINDOW
VMEM_LIMIT = 56 * 1024 * 1024


def _rmsnorm(x, g):
    return x * lax.rsqrt(jnp.mean(x * x, axis=-1, keepdims=True) + EPS) * g


def _dot(a, b):
    return jnp.dot(a, b, preferred_element_type=F32)


def _dot_nt(a, b):
    return lax.dot_general(a, b, (((1,), (1,)), ((), ())), preferred_element_type=F32)


def _dot_tn(a, b):
    return lax.dot_general(a, b, (((0,), (0,)), ((), ())), preferred_element_type=F32)


def _params(sem):
    return pltpu.CompilerParams(dimension_semantics=sem, vmem_limit_bytes=VMEM_LIMIT)


def _proj_body(x_ref, g_ref, w_ref, bg_ref, qa_ref, ka_ref, va_ref, qb_ref, kb_ref, vb_ref,
               rg_ref, ga_ref, gb_ref):
    h = _rmsnorm(x_ref[...], g_ref[...]).astype(BF16)

    def mm(lo, width):
        return _dot(h, w_ref[:, lo:lo + width])

    qa_ref[...] = (mm(OFF_QA, Q_A) * (HEAD_DIM_A ** -0.5)).astype(BF16)
    ka_ref[...] = mm(OFF_KA, KV_A)
    va_ref[...] = mm(OFF_VA, KV_A)
    qb_ref[...] = mm(OFF_QB, Q_B).astype(BF16)
    kb_ref[...] = (mm(OFF_KB, Q_B) * (DK_B ** -0.5)).astype(BF16)
    vb_ref[...] = mm(OFF_VB, V_B).astype(BF16)
    rg_ref[...] = mm(OFF_RG, V_B)
    ga_ref[...] = mm(OFF_GA, D_MODEL) + bg_ref[:, :D_MODEL]
    gb_ref[...] = mm(OFF_GB, D_MODEL) + bg_ref[:, D_MODEL:]


def _project(x, g, w_bf16, b_gate, tm):
    n = x.shape[0]
    row = lambda width: pl.BlockSpec((tm, width), lambda i: (i, 0))
    full = lambda a: pl.BlockSpec(a.shape, lambda i: (0, 0))
    widths = (Q_A, KV_A, KV_A, Q_B, Q_B, V_B, V_B, D_MODEL, D_MODEL)
    dtypes = (BF16, F32, F32, BF16, BF16, BF16, F32, F32, F32)
    return pl.pallas_call(
        _proj_body,
        grid=(n // tm,),
        in_specs=[row(D_MODEL), full(g), full(w_bf16), full(b_gate)],
        out_specs=[row(w) for w in widths],
        out_shape=[jax.ShapeDtypeStruct((n, w), d) for w, d in zip(widths, dtypes)],
        compiler_params=_params(("parallel",)),
        name="projection",
    )(x, g, w_bf16, b_gate)


def _swa_body(sink_ref, q_ref, kp_ref, kc_ref, vp_ref, vc_ref, nd_ref, al_ref, o_ref):
    k = jnp.concatenate([kp_ref[...], kc_ref[...]], axis=0)
    v = jnp.concatenate([vp_ref[...], vc_ref[...]], axis=0)
    lane = lax.broadcasted_iota(I32, k.shape, 1)
    low = lane < HEAD_DIM_A
    k_sw = pltpu.roll(k, HEAD_DIM_A, axis=1)
    v_sw = pltpu.roll(v, HEAD_DIM_A, axis=1)
    negdist = nd_ref[...]
    allowed = al_ref[0] > 0.5
    for kv in range(N_KV_A):
        own = low if kv == 0 else jnp.logical_not(low)
        k_rep = jnp.where(own, k, k_sw)
        v_rep = jnp.where(own, v, v_sw)
        zero = jnp.zeros_like(k_rep)
        kk = jnp.concatenate([jnp.where(low, k_rep, zero), jnp.where(low, zero, k_rep)], axis=0).astype(BF16)
        vv = jnp.concatenate([jnp.where(low, v_rep, zero), jnp.where(low, zero, v_rep)], axis=0).astype(BF16)
        for a in range(GROUP_A // 2):
            col = kv * GROUP_A * HEAD_DIM_A + a * LANES
            s2 = _dot_nt(q_ref[:, col:col + LANES], kk)
            ps = []
            for u in range(2):
                head = kv * GROUP_A + 2 * a + u
                slope = 2.0 ** (-8.0 * (head + 1) / N_HEADS_A)
                sink = sink_ref[head]
                s = s2[:, u * ATT_KEYS:(u + 1) * ATT_KEYS] + slope * negdist
                s = jnp.where(allowed, s, NEG_INF)
                m = jnp.maximum(jnp.max(s, axis=-1, keepdims=True), sink)
                p = jnp.exp(s - m)
                denom = jnp.sum(p, axis=-1, keepdims=True) + jnp.exp(sink - m)
                ps.append((p / denom).astype(BF16))
            o = _dot(jnp.concatenate(ps, axis=1), vv)
            o_ref[:, col:col + LANES] = o.astype(BF16)


def _swa(sinks, q, k, v, negdist, allow, groups, q_blocks, tq):
    n = q.shape[0]
    kblocks = k.shape[0] // LANES // groups
    shift = kblocks - q_blocks
    prev = lambda b, j: (b * kblocks + jnp.maximum(j + shift - 1, 0), 0)
    cur = lambda b, j: (b * kblocks + j + shift, 0)
    kv_spec = lambda f: pl.BlockSpec((LANES, KV_A), f)
    return pl.pallas_call(
        _swa_body,
        grid=(groups, q_blocks),
        in_specs=[pl.BlockSpec(memory_space=pltpu.SMEM),
                  pl.BlockSpec((tq, Q_A), lambda b, j: (b * q_blocks + j, 0)),
                  kv_spec(prev), kv_spec(cur), kv_spec(prev), kv_spec(cur),
                  pl.BlockSpec((tq, ATT_KEYS), lambda b, j: (0, 0)),
                  pl.BlockSpec((1, tq, ATT_KEYS), lambda b, j: (jnp.minimum(j + shift, 1), 0, 0))],
        out_specs=pl.BlockSpec((tq, Q_A), lambda b, j: (b * q_blocks + j, 0)),
        out_shape=jax.ShapeDtypeStruct((n, Q_A), BF16),
        compiler_params=_params(("parallel", "parallel")),
        name="attention",
    )(sinks, q, k, k, v, v, negdist, allow)


def _ret_body(q_ref, k_ref, v_ref, rg_ref, s0_ref, intra_ref, rdec_ref, kdec_ref, sdec_ref,
              o_ref, st_ref):
    c = pl.program_id(1)

    @pl.when(c == 0)
    def _():
        st_ref[...] = s0_ref[...]

    for h in range(N_HEADS_B):
        q = q_ref[:, h * DK_B:(h + 1) * DK_B]
        k = k_ref[:, h * DK_B:(h + 1) * DK_B]
        v = v_ref[:, h * DV_B:(h + 1) * DV_B]
        st = st_ref[0, h]
        scores = (_dot_nt(q, k) * intra_ref[h]).astype(BF16)
        o = _dot(scores, v) + _dot(q, st.astype(BF16)) * rdec_ref[h]
        k_dec = (k.astype(F32) * kdec_ref[h]).astype(BF16)
        st_ref[0, h] = sdec_ref[h] * st + _dot_tn(k_dec, v)
        mu = jnp.mean(o, axis=-1, keepdims=True)
        d = o - mu
        var = jnp.mean(d * d, axis=-1, keepdims=True)
        on = d * lax.rsqrt(var + EPS)
        r = rg_ref[:, h * DV_B:(h + 1) * DV_B]
        o_ref[:, h * DV_B:(h + 1) * DV_B] = (on * (r * jax.nn.sigmoid(r))).astype(BF16)


def _retention_consts(length):
    lg = jnp.log1p(-(2.0 ** (-5.0 - jnp.arange(N_HEADS_B, dtype=F32))))
    n = jnp.arange(length, dtype=F32)
    diff = n[:, None] - n[None, :]
    intra = jnp.where(diff >= 0, jnp.exp(jnp.maximum(diff, 0.0) * lg[:, None, None]), 0.0)
    rdec = jnp.exp((n[None, :] + 1.0) * lg[:, None])[..., None]
    kdec = jnp.exp((length - 1.0 - n)[None, :] * lg[:, None])[..., None]
    sdec = jnp.exp(length * lg)
    return intra, rdec, kdec, sdec


def _retention(q, k, v, rg, s0, streams, chunks, length):
    n = q.shape[0]
    intra, rdec, kdec, sdec = _retention_consts(length)
    row = lambda width: pl.BlockSpec((length, width), lambda b, c: (b * chunks + c, 0))
    const = lambda a: pl.BlockSpec(a.shape, lambda b, c: (0,) * a.ndim)
    st_spec = pl.BlockSpec((1, N_HEADS_B, DK_B, DV_B), lambda b, c: (b, 0, 0, 0))
    return pl.pallas_call(
        _ret_body,
        grid=(streams, chunks),
        in_specs=[row(Q_B), row(Q_B), row(V_B), row(V_B), st_spec,
                  const(intra), const(rdec), const(kdec), pl.BlockSpec(memory_space=pltpu.SMEM)],
        out_specs=[row(V_B), st_spec],
        out_shape=[jax.ShapeDtypeStruct((n, V_B), BF16),
                   jax.ShapeDtypeStruct((streams, N_HEADS_B, DK_B, DV_B), F32)],
        compiler_params=_params(("parallel", "arbitrary")),
        name="retention",
    )(q, k, v, rg, s0, intra, rdec, kdec, sdec)


def _merge_body(x_ref, oa_ref, ob_ref, ga_ref, gb_ref, wa_ref, wb_ref, wo_ref, o_ref):
    ya = _dot(oa_ref[...], wa_ref[...])
    yb = _dot(ob_ref[...], wb_ref[...])
    mix = jax.nn.sigmoid(ga_ref[...]) * ya + jax.nn.sigmoid(gb_ref[...]) * yb
    o_ref[...] = x_ref[...] + _dot(mix.astype(BF16), wo_ref[...])


def _merge(x, oa, ob, ga, gb, wa, wb, wo, tm):
    n = x.shape[0]
    row = lambda width: pl.BlockSpec((tm, width), lambda i: (i, 0))
    full = lambda a: pl.BlockSpec(a.shape, lambda i: (0, 0))
    return pl.pallas_call(
        _merge_body,
        grid=(n // tm,),
        in_specs=[row(D_MODEL), row(Q_A), row(V_B), row(D_MODEL), row(D_MODEL), full(wa), full(wb), full(wo)],
        out_specs=row(D_MODEL),
        out_shape=jax.ShapeDtypeStruct((n, D_MODEL), F32),
        compiler_params=_params(("parallel",)),
        name="merge",
    )(x, oa, ob, ga, gb, wa, wb, wo)


def _top16(s, order):
    t = s.shape[1]
    rank = lax.broadcasted_iota(I32, (PEER_TOPK, t), 0)
    vals = jnp.zeros((PEER_TOPK, t), F32)
    ids = jnp.zeros((PEER_TOPK, t), F32)
    for r in range(PEER_TOPK):
        m = jnp.max(s, axis=0, keepdims=True)
        pick = jnp.min(jnp.where(s == m, order, 3e38), axis=0, keepdims=True)
        s = jnp.where(order == pick, -jnp.inf, s)
        vals = jnp.where(rank == r, m, vals)
        ids = jnp.where(rank == r, pick, ids)
    return vals, ids


def _route_body(x_ref, g_ref, wq_ref, k1_ref, k2_ref, idx_ref, gate_ref, h_ref):
    h_ref[...] = _rmsnorm(x_ref[...], g_ref[...])
    h = h_ref[...].astype(BF16)
    t = h.shape[0]
    q_t = _dot_nt(wq_ref[...], h).astype(BF16)
    iota = lambda rows: lax.broadcasted_iota(I32, (rows, t), 0).astype(F32)
    key_id = iota(N_KEYS)
    flat = jnp.concatenate(
        [iota(PEER_TOPK)] + [a * PEER_TOPK + iota(8) for a in range(1, 8)] + [(8 + iota(8)) * PEER_TOPK],
        axis=0)
    rank = lax.broadcasted_iota(I32, (PEER_TOPK, t), 0)
    idx_rows, gate_rows = [], []
    for hd in range(PEER_HEADS):
        base = hd * PEER_QDIM
        s1 = _dot(k1_ref[...], q_t[base:base + PEER_HALF])
        s2 = _dot(k2_ref[...], q_t[base + PEER_HALF:base + PEER_QDIM])
        v1, i1 = _top16(s1, key_id)
        v2, i2 = _top16(s2, key_id)
        cand = jnp.concatenate(
            [v1[0:1] + v2]
            + [v1[a:a + 1] + v2[0:8] for a in range(1, 8)]
            + [v1[8:16] + v2[0:1]], axis=0)
        cidx = jnp.concatenate(
            [i1[0:1] * N_KEYS + i2]
            + [i1[a:a + 1] * N_KEYS + i2[0:8] for a in range(1, 8)]
            + [i1[8:16] * N_KEYS + i2[0:1]], axis=0)
        best, pos = _top16(cand, flat)
        eidx = jnp.zeros((PEER_TOPK, t), F32)
        for r in range(PEER_TOPK):
            picked = jnp.max(jnp.where(flat == pos[r:r + 1], cidx, -1.0), axis=0, keepdims=True)
            eidx = jnp.where(rank == r, picked, eidx)
        e = jnp.exp(best - best[0:1])
        gate_rows.append(e / jnp.sum(e, axis=0, keepdims=True))
        idx_rows.append(eidx.astype(I32))
    idx_ref[...] = jnp.concatenate(idx_rows, axis=0)
    gate_ref[...] = jnp.concatenate(gate_rows, axis=0)


def _route(x, g, wq_t, k1, k2, tm, first_row, n):
    first_block = first_row // tm
    full = lambda a: pl.BlockSpec(a.shape, lambda i: (0, 0))
    col = pl.BlockSpec((PICKS, tm), lambda i: (0, i))
    return pl.pallas_call(
        _route_body,
        grid=(n // tm,),
        in_specs=[pl.BlockSpec((tm, D_MODEL), lambda i: (i + first_block, 0)), full(g), full(wq_t), full(k1),
                  full(k2)],
        out_specs=[col, col, pl.BlockSpec((tm, D_MODEL), lambda i: (i, 0))],
        out_shape=[jax.ShapeDtypeStruct((PICKS, n), I32), jax.ShapeDtypeStruct((PICKS, n), F32),
                   jax.ShapeDtypeStruct((n, D_MODEL), F32)],
        compiler_params=_params(("parallel",)),
        name="routing",
    )(x, g, wq_t, k1, k2)


SC_CORES = 2
SC_SUBCORES = 16
SC_LANES = 16
SC_CHUNK = 16
SC_PROMPT_ROWS = 17408


def _gelu_via_exp(x):
    z = 0.7978845608028654 * (x + 0.044715 * x * x * x)
    return 0.5 * x * (2.0 - 2.0 / (1.0 + jnp.exp(2.0 * z)))


def _sc_experts(table, idx, gate, h, first):
    n = idx.shape[0]
    workers = SC_CORES * SC_SUBCORES
    per_worker = n // workers
    n_chunks = PICKS // SC_CHUNK
    steps = D_MODEL // SC_LANES
    mesh = plsc.VectorSubcoreMesh(core_axis_name="c", subcore_axis_name="s",
                                  num_cores=SC_CORES, num_subcores=SC_SUBCORES)

    def body(tab_hbm, idx_hbm, gate_hbm, h_hbm, y_hbm,
             idx_0, idx_1, gate_0, gate_1, h_0, h_1, y_0, y_1, rows_a, rows_b,
             in_sem_0, in_sem_1, y_sem_0, y_sem_1, sem_a, sem_b):
        base = (lax.axis_index("s") * SC_CORES + lax.axis_index("c")) * per_worker
        last = base + per_worker - 1
        sets = ((idx_0, gate_0, h_0, y_0, in_sem_0, y_sem_0), (idx_1, gate_1, h_1, y_1, in_sem_1, y_sem_1))
        bufs = ((rows_a, sem_a), (rows_b, sem_b))

        def in_copies(t, q):
            idx_v, gate_v, h_v, _, sem, _ = sets[q]
            return (pltpu.make_async_copy(idx_hbm.at[t], idx_v, sem),
                    pltpu.make_async_copy(gate_hbm.at[t], gate_v, sem),
                    pltpu.make_async_copy(h_hbm.at[t + first], h_v, sem))

        def y_copy(t, q):
            return pltpu.make_async_copy(sets[q][3], y_hbm.at[t], sets[q][5])

        def gather(c, b, q):
            rows, sem = bufs[b]
            picks = sets[q][0].at[pl.ds(c * SC_CHUNK, SC_CHUNK)]
            return pltpu.make_async_copy(tab_hbm.at[picks], rows, sem)

        def compute(c, b, q):
            rows, _ = bufs[b]
            _, gate_v, h_v, y_v, _, _ = sets[q]
            u_of = lambda word: lax.bitcast_convert_type(word & jnp.int32(-65536), F32)
            v_of = lambda word: lax.bitcast_convert_type(word << 16, F32)

            def dot_step(j, accs):
                cols = pl.ds(j * SC_LANES, SC_LANES)
                hj = h_v[cols]
                return tuple(accs[k] + u_of(rows[k, 0, cols]) * hj for k in range(SC_CHUNK))

            accs = plsc.parallel_loop(0, steps, carry=tuple(jnp.zeros((SC_LANES,), F32) for _ in range(SC_CHUNK)))(
                dot_step)
            lane = lax.iota(I32, SC_LANES)
            s = jnp.zeros((SC_LANES,), F32)
            for k in range(SC_CHUNK):
                s = jnp.where(lane == k, jnp.sum(accs[k]), s)
            w = _gelu_via_exp(s) * gate_v[pl.ds(c * SC_CHUNK, SC_CHUNK)]
            ws = [jnp.full((SC_LANES,), jnp.sum(jnp.where(lane == k, w, 0.0)), F32) for k in range(SC_CHUNK)]

            @plsc.parallel_loop(0, steps)
            def _(j):
                cols = pl.ds(j * SC_LANES, SC_LANES)
                terms = [ws[k] * v_of(rows[k, 0, cols]) for k in range(SC_CHUNK)]
                while len(terms) > 1:
                    terms = [terms[i] + terms[i + 1] for i in range(0, len(terms), 2)]
                y_v[cols] = y_v[cols] + terms[0]

        def token(t, q, first):
            nxt = jnp.minimum(t + 1, last)
            for cp in in_copies(nxt, 1 - q):
                cp.start()
            y_v = sets[q][3]

            @pl.when(jnp.logical_not(first))
            def _():
                y_copy(t, q).wait()

            @pl.loop(0, steps)
            def _(j):
                y_v[pl.ds(j * SC_LANES, SC_LANES)] = jnp.zeros((SC_LANES,), F32)

            @pl.loop(0, n_chunks // 2)
            def _(p):
                c0 = p * 2
                gather(c0 + 1, 1, q).start()
                gather(c0, 0, q).wait()
                compute(c0, 0, q)

                @pl.when(p + 1 < n_chunks // 2)
                def _():
                    gather(c0 + 2, 0, q).start()

                gather(c0 + 1, 1, q).wait()
                compute(c0 + 1, 1, q)

            for cp in in_copies(nxt, 1 - q):
                cp.wait()
            gather(0, 0, 1 - q).start()
            y_copy(t, q).start()

        for cp in in_copies(base, 0):
            cp.start()
        for cp in in_copies(base, 0):
            cp.wait()
        gather(0, 0, 0).start()

        @pl.loop(0, per_worker // 2)
        def _(i):
            token(base + 2 * i, 0, i == 0)
            token(base + 2 * i + 1, 1, i == 0)

        gather(0, 0, 0).wait()
        y_copy(last - 1, 0).wait()
        y_copy(last, 1).wait()

    vec = lambda n_el, dt: [pltpu.VMEM((n_el,), dt), pltpu.VMEM((n_el,), dt)]
    return pl.kernel(
        body, out_type=jax.ShapeDtypeStruct((n, D_MODEL), F32), mesh=mesh,
        scratch_types=vec(PICKS, I32) + vec(PICKS, F32) + vec(D_MODEL, F32) + vec(D_MODEL, F32)
        + [pltpu.VMEM((SC_CHUNK, 1, D_MODEL), I32) for _ in range(2)]
        + [pltpu.SemaphoreType.DMA] * 6,
        compiler_params=pltpu.CompilerParams(needs_layout_passes=False),
        name="experts_sc",
    )(table, idx, gate, h)


def _finish_body(x_ref, y_ref, gf_ref, *rest_and_out):
    rest_and_out[-1][...] = _rmsnorm(x_ref[...] + y_ref[...], gf_ref[...])


def _finish(x, y, g_final, rest, tm, first_row=0):
    first_block = first_row // tm
    row = pl.BlockSpec((tm, D_MODEL), lambda i: (i + first_block, 0))
    extra = () if rest is None else (rest,)
    return pl.pallas_call(
        _finish_body,
        grid=(y.shape[0] // tm,),
        in_specs=[row, pl.BlockSpec((tm, D_MODEL), lambda i: (i, 0)), pl.BlockSpec(g_final.shape, lambda i: (0, 0))]
        + [pl.BlockSpec(memory_space=pl.ANY) for _ in extra],
        out_specs=row,
        out_shape=jax.ShapeDtypeStruct(x.shape if rest is not None else y.shape, F32),
        input_output_aliases={3: 0} if rest is not None else {},
        compiler_params=_params(("parallel",)),
        name="finish",
    )(x, y, g_final, *extra)


EXP_TOKENS = 256
EXP_SLOTS = 4


def _expert_body(x_ref, g_ref, gf_ref, idx_hbm, gate_ref, tab_hbm, o_ref,
                 idx_smem, buf0, buf1, buf2, buf3, h_buf, y_buf, sems, idx_sem):
    bufs = (buf0, buf1, buf2, buf3)
    ahead = EXP_SLOTS - 1
    i = pl.program_id(0)
    idx_copy = pltpu.make_async_copy(idx_hbm.at[pl.ds(i * (EXP_TOKENS * PICKS), EXP_TOKENS * PICKS)],
                                     idx_smem, idx_sem)
    idx_copy.start()
    h_buf[...] = _rmsnorm(x_ref[...], g_ref[...])
    idx_copy.wait()

    def issue(t, slot):
        for k in range(PICKS):
            e = idx_smem[t * PICKS + k]
            pltpu.async_copy(tab_hbm.at[e], bufs[slot].at[pl.ds(k, 1), :], sems.at[slot], priority=k % 2)

    def wait_slot(slot):
        pltpu.make_async_copy(bufs[(slot + 1) % EXP_SLOTS], bufs[slot], sems.at[slot]).wait()

    lane = lax.broadcasted_iota(I32, (PICKS, EXP_TOKENS), 1)

    def compute(t, slot):
        h = h_buf[pl.ds(t, 1), :]
        u = lax.bitcast_convert_type(bufs[slot][...] & jnp.int32(-65536), F32)
        s = jnp.sum(u * h, axis=-1, keepdims=True)
        gate = jnp.sum(jnp.where(lane == t, gate_ref[...], 0.0), axis=-1, keepdims=True)
        w = gate * jax.nn.gelu(s)
        v = lax.bitcast_convert_type(bufs[slot][...] << 16, F32)
        y_buf[pl.ds(t, 1), :] = jnp.sum(w * v, axis=0, keepdims=True)

    def group(t0, issue_upto):
        for s in range(EXP_SLOTS):
            wait_slot(s)
            if s < issue_upto:
                issue(t0 + s + ahead, (s + ahead) % EXP_SLOTS)
            compute(t0 + s, s)

    for s in range(ahead):
        issue(s, s)

    def steady(q, carry):
        group(q * EXP_SLOTS, EXP_SLOTS)
        return carry

    n_groups = EXP_TOKENS // EXP_SLOTS
    lax.fori_loop(0, n_groups - 1, steady, 0)
    group((n_groups - 1) * EXP_SLOTS, EXP_SLOTS - ahead)
    o_ref[...] = _rmsnorm(x_ref[...] + y_buf[...], gf_ref[...])


def _experts(x, g, g_final, idx_flat, gate_t, table, first_token):
    n = x.shape[0]
    first_block = first_token // EXP_TOKENS
    full = lambda a: pl.BlockSpec(a.shape, lambda i: (0, 0))
    row = pl.BlockSpec((EXP_TOKENS, D_MODEL), lambda i: (i + first_block, 0))
    return pl.pallas_call(
        _expert_body,
        grid=(n // EXP_TOKENS - first_block,),
        in_specs=[row, full(g), full(g_final),
                  pl.BlockSpec(memory_space=pl.ANY),
                  pl.BlockSpec((PICKS, EXP_TOKENS), lambda i: (0, i)),
                  pl.BlockSpec(memory_space=pl.ANY)],
        out_specs=row,
        out_shape=jax.ShapeDtypeStruct((n, D_MODEL), F32),
        scratch_shapes=[pltpu.SMEM((EXP_TOKENS * PICKS,), I32)]
        + [pltpu.VMEM((PICKS, D_MODEL), I32) for _ in range(EXP_SLOTS)]
        + [pltpu.VMEM((EXP_TOKENS, D_MODEL), F32),
           pltpu.VMEM((EXP_TOKENS, D_MODEL), F32),
           pltpu.SemaphoreType.DMA((EXP_SLOTS,)),
           pltpu.SemaphoreType.DMA(())],
        compiler_params=pltpu.CompilerParams(dimension_semantics=("arbitrary",), vmem_limit_bytes=VMEM_LIMIT,
                                             disable_bounds_checks=True),
        name="experts",
    )(x, g, g_final, idx_flat, gate_t, table)


def _attention_masks(tq):
    i = jnp.arange(tq, dtype=I32)[:, None]
    r = jnp.arange(ATT_KEYS, dtype=I32)[None, :]
    negdist = -jnp.abs(WINDOW + i - r).astype(F32)
    return negdist, i, r


def _layer(x, n_groups, rows, swa_inputs, ret_state, ret_len, w, tm, n_sc):
    (norm_mix, w_in, b_gate, sinks, wa, wb, wo, norm_ffn, wq_t, k1, k2, packed, norm_final) = w
    assert n_sc < x.shape[0]
    qa, ka, va, qb, kb, vb, rg, ga, gb = _project(x, norm_mix, w_in, b_gate, tm)
    oa, k_rows, v_rows = swa_inputs(qa, ka, va, sinks)
    ob, s_fin = _retention(qb, kb, vb, rg, ret_state, n_groups, rows // ret_len, ret_len)
    x1 = _merge(x, oa, ob, ga, gb, wa, wb, wo, tm)
    if n_sc:
        idx_t, gate_t, h2 = _route(x1, norm_ffn, wq_t, k1, k2, tm, 0, n_sc)
        y_sc = _sc_experts(packed, idx_t.T, gate_t.T, h2, 0)
    idx_t, gate_t, _ = _route(x1, norm_ffn, wq_t, k1, k2, tm, n_sc, x.shape[0] - n_sc)
    y = _experts(x1, norm_ffn, norm_final, idx_t.T.reshape(-1), gate_t, packed, n_sc)
    if n_sc:
        y = _finish(x1, y_sc, norm_final, y, tm)
    return y, k_rows, v_rows, s_fin


def kernel(x_prompt, x_sample, cache_swa_k, cache_swa_v, state_ret, norm_mix, w_in, b_gate, attn_sinks,
           w_branch_a, w_branch_b, w_out, norm_ffn, peer_w_q, peer_sub_k1, peer_sub_k2, peer_u, peer_v,
           norm_final):
    batch, seq, _ = x_prompt.shape
    dec_batch, dec_seq, _ = x_sample.shape
    assert norm_mix.shape[0] == 1, "single-layer trunk"
    half = lambda a: lax.bitcast_convert_type(a.astype(BF16), jnp.uint16).astype(jnp.uint32)
    packed = lax.bitcast_convert_type((half(peer_u[0]) << 16) | half(peer_v[0]), I32)[:, None, :]
    table, x_prompt, x_sample = lax.optimization_barrier((packed, x_prompt, x_sample))
    w = (norm_mix[0][None], w_in[0].astype(BF16), b_gate[0][None], attn_sinks[0],
         w_branch_a[0].astype(BF16), w_branch_b[0].astype(BF16), w_out[0].astype(BF16),
         norm_ffn[0][None], peer_w_q[0].T.astype(BF16), peer_sub_k1[0].astype(BF16),
         peer_sub_k2[0].astype(BF16), table, norm_final[None])

    def swa_prompt(qa, ka, va, sinks):
        streams = qa.shape[0] // seq
        tq = 2 * CHUNK
        negdist, i, r = _attention_masks(tq)
        lq, lk = i // CHUNK, r // CHUNK
        band = (lk >= lq) & (lk <= lq + 2)
        allow = jnp.stack([band & (r >= WINDOW), band]).astype(F32)
        oa = _swa(sinks, qa, ka, va, negdist, allow, streams, seq // tq, tq)
        tail = lambda a: a.reshape(streams, seq, N_KV_A, HEAD_DIM_A)[:, seq - WINDOW:]
        return oa, tail(ka), tail(va)

    def swa_sample(qa, ka, va, sinks):
        negdist, i, r = _attention_masks(dec_seq)
        visible = jnp.broadcast_to(r < WINDOW + dec_seq, (dec_seq, ATT_KEYS))
        allow = jnp.stack([visible, visible]).astype(F32)
        pad = jnp.zeros((dec_batch, ATT_KEYS - WINDOW - dec_seq, KV_A), F32)
        k_all = jnp.concatenate([cache_swa_k[0].reshape(dec_batch, WINDOW, KV_A),
                                 ka.reshape(dec_batch, dec_seq, KV_A), pad], axis=1)
        v_all = jnp.concatenate([cache_swa_v[0].reshape(dec_batch, WINDOW, KV_A),
                                 va.reshape(dec_batch, dec_seq, KV_A), pad], axis=1)
        oa = _swa(sinks, qa, k_all.reshape(-1, KV_A), v_all.reshape(-1, KV_A), negdist, allow,
                  dec_batch, 1, dec_seq)
        tail = lambda a: a[:, dec_seq:WINDOW + dec_seq].reshape(dec_batch, WINDOW, N_KV_A, HEAD_DIM_A)
        return oa, tail(k_all), tail(v_all)

    ys, ks, vs, ss = _layer(x_sample.reshape(dec_batch * dec_seq, D_MODEL), dec_batch, dec_seq, swa_sample,
                            state_ret[0], dec_seq, w, 128, 0)
    s0 = jnp.zeros((batch, N_HEADS_B, DK_B, DV_B), F32)
    yp, kp, vp, sp = _layer(x_prompt.reshape(batch * seq, D_MODEL), batch, seq, swa_prompt, s0, 256, w, 256,
                            SC_PROMPT_ROWS)

    return (yp.reshape(batch, seq, D_MODEL), ys.reshape(dec_batch, dec_seq, D_MODEL),
            kp[None], vp[None], sp[None], ks[None], vs[None], ss[None])
```

```python
import functools

import jax
import jax.numpy as jnp
from jax import lax
from jax.experimental import pallas as pl
from jax.experimental.pallas import tpu as pltpu
from jax.experimental.pallas import tpu_sc as plsc

F32 = jnp.float32
BF16 = jnp.bfloat16
I32 = jnp.int32

D_MODEL = 1024
CHUNK = 64
EPS = 1e-6
NEG_INF = -1e30
PAST_LEN = 2048

N_HEADS_A = 8
N_KV_A = 2
GROUP_A = N_HEADS_A // N_KV_A
HEAD_DIM_A = 64
WINDOW = 128
N_HEADS_B = 4
DK_B = 128
DV_B = 256
Q_A = N_HEADS_A * HEAD_DIM_A
KV_A = N_KV_A * HEAD_DIM_A
Q_B = N_HEADS_B * DK_B
V_B = N_HEADS_B * DV_B
D_IN = Q_A + 2 * KV_A + 2 * Q_B + 2 * V_B + 2 * D_MODEL
OFF_QA = 0
OFF_KA = OFF_QA + Q_A
OFF_VA = OFF_KA + KV_A
OFF_QB = OFF_VA + KV_A
OFF_KB = OFF_QB + Q_B
OFF_VB = OFF_KB + Q_B
OFF_RG = OFF_VB + V_B
OFF_GA = OFF_RG + V_B
OFF_GB = OFF_GA + D_MODEL

N_KEYS = 128
N_EXPERTS = N_KEYS * N_KEYS
PEER_HEADS = 8
PEER_QDIM = 256
PEER_HALF = PEER_QDIM // 2
PEER_TOPK = 16
PICKS = PEER_HEADS * PEER_TOPK

LANES = 128
ATT_KEYS = 2 * WINDOW
VMEM_LIMIT = 56 * 1024 * 1024


def _rmsnorm(x, g):
    return x * lax.rsqrt(jnp.mean(x * x, axis=-1, keepdims=True) + EPS) * g


def _dot(a, b):
    return jnp.dot(a, b, preferred_element_type=F32)


def _dot_nt(a, b):
    return lax.dot_general(a, b, (((1,), (1,)), ((), ())), preferred_element_type=F32)


def _dot_tn(a, b):
    return lax.dot_general(a, b, (((0,), (0,)), ((), ())), preferred_element_type=F32)


def _params(sem):
    return pltpu.CompilerParams(dimension_semantics=sem, vmem_limit_bytes=VMEM_LIMIT)


def _proj_body(x_ref, g_ref, w_ref, bg_ref, qa_ref, ka_ref, va_ref, qb_ref, kb_ref, vb_ref,
               rg_ref, ga_ref, gb_ref):
    h = _rmsnorm(x_ref[...], g_ref[...]).astype(BF16)

    def mm(lo, width):
        return _dot(h, w_ref[:, lo:lo + width])

    qa_ref[...] = (mm(OFF_QA, Q_A) * (HEAD_DIM_A ** -0.5)).astype(BF16)
    ka_ref[...] = mm(OFF_KA, KV_A)
    va_ref[...] = mm(OFF_VA, KV_A)
    qb_ref[...] = mm(OFF_QB, Q_B).astype(BF16)
    kb_ref[...] = (mm(OFF_KB, Q_B) * (DK_B ** -0.5)).astype(BF16)
    vb_ref[...] = mm(OFF_VB, V_B).astype(BF16)
    rg_ref[...] = mm(OFF_RG, V_B)
    ga_ref[...] = mm(OFF_GA, D_MODEL) + bg_ref[:, :D_MODEL]
    gb_ref[...] = mm(OFF_GB, D_MODEL) + bg_ref[:, D_MODEL:]


def _project(x, g, w_bf16, b_gate, tm):
    n = x.shape[0]
    row = lambda width: pl.BlockSpec((tm, width), lambda i: (i, 0))
    full = lambda a: pl.BlockSpec(a.shape, lambda i: (0, 0))
    widths = (Q_A, KV_A, KV_A, Q_B, Q_B, V_B, V_B, D_MODEL, D_MODEL)
    dtypes = (BF16, F32, F32, BF16, BF16, BF16, F32, F32, F32)
    return pl.pallas_call(
        _proj_body,
        grid=(n // tm,),
        in_specs=[row(D_MODEL), full(g), full(w_bf16), full(b_gate)],
        out_specs=[row(w) for w in widths],
        out_shape=[jax.ShapeDtypeStruct((n, w), d) for w, d in zip(widths, dtypes)],
        compiler_params=_params(("parallel",)),
        name="projection",
    )(x, g, w_bf16, b_gate)


def _swa_body(sink_ref, q_ref, kp_ref, kc_ref, vp_ref, vc_ref, nd_ref, al_ref, o_ref):
    k = jnp.concatenate([kp_ref[...], kc_ref[...]], axis=0)
    v = jnp.concatenate([vp_ref[...], vc_ref[...]], axis=0)
    lane = lax.broadcasted_iota(I32, k.shape, 1)
    low = lane < HEAD_DIM_A
    k_sw = pltpu.roll(k, HEAD_DIM_A, axis=1)
    v_sw = pltpu.roll(v, HEAD_DIM_A, axis=1)
    negdist = nd_ref[...]
    allowed = al_ref[0] > 0.5
    for kv in range(N_KV_A):
        own = low if kv == 0 else jnp.logical_not(low)
        k_rep = jnp.where(own, k, k_sw)
        v_rep = jnp.where(own, v, v_sw)
        zero = jnp.zeros_like(k_rep)
        kk = jnp.concatenate([jnp.where(low, k_rep, zero), jnp.where(low, zero, k_rep)], axis=0).astype(BF16)
        vv = jnp.concatenate([jnp.where(low, v_rep, zero), jnp.where(low, zero, v_rep)], axis=0).astype(BF16)
        for a in range(GROUP_A // 2):
            col = kv * GROUP_A * HEAD_DIM_A + a * LANES
            s2 = _dot_nt(q_ref[:, col:col + LANES], kk)
            ps = []
            for u in range(2):
                head = kv * GROUP_A + 2 * a + u
                slope = 2.0 ** (-8.0 * (head + 1) / N_HEADS_A)
                sink = sink_ref[head]
                s = s2[:, u * ATT_KEYS:(u + 1) * ATT_KEYS] + slope * negdist
                s = jnp.where(allowed, s, NEG_INF)
                m = jnp.maximum(jnp.max(s, axis=-1, keepdims=True), sink)
                p = jnp.exp(s - m)
                denom = jnp.sum(p, axis=-1, keepdims=True) + jnp.exp(sink - m)
                ps.append((p / denom).astype(BF16))
            o = _dot(jnp.concatenate(ps, axis=1), vv)
            o_ref[:, col:col + LANES] = o.astype(BF16)


def _swa(sinks, q, k, v, negdist, allow, groups, q_blocks, tq):
    n = q.shape[0]
    kblocks = k.shape[0] // LANES // groups
    shift = kblocks - q_blocks
    prev = lambda b, j: (b * kblocks + jnp.maximum(j + shift - 1, 0), 0)
    cur = lambda b, j: (b * kblocks + j + shift, 0)
    kv_spec = lambda f: pl.BlockSpec((LANES, KV_A), f)
    return pl.pallas_call(
        _swa_body,
        grid=(groups, q_blocks),
        in_specs=[pl.BlockSpec(memory_space=pltpu.SMEM),
                  pl.BlockSpec((tq, Q_A), lambda b, j: (b * q_blocks + j, 0)),
                  kv_spec(prev), kv_spec(cur), kv_spec(prev), kv_spec(cur),
                  pl.BlockSpec((tq, ATT_KEYS), lambda b, j: (0, 0)),
                  pl.BlockSpec((1, tq, ATT_KEYS), lambda b, j: (jnp.minimum(j + shift, 1), 0, 0))],
        out_specs=pl.BlockSpec((tq, Q_A), lambda b, j: (b * q_blocks + j, 0)),
        out_shape=jax.ShapeDtypeStruct((n, Q_A), BF16),
        compiler_params=_params(("parallel", "parallel")),
        name="attention",
    )(sinks, q, k, k, v, v, negdist, allow)


def _ret_body(q_ref, k_ref, v_ref, rg_ref, s0_ref, intra_ref, rdec_ref, kdec_ref, sdec_ref,
              o_ref, st_ref):
    c = pl.program_id(1)

    @pl.when(c == 0)
    def _():
        st_ref[...] = s0_ref[...]

    for h in range(N_HEADS_B):
        q = q_ref[:, h * DK_B:(h + 1) * DK_B]
        k = k_ref[:, h * DK_B:(h + 1) * DK_B]
        v = v_ref[:, h * DV_B:(h + 1) * DV_B]
        st = st_ref[0, h]
        scores = (_dot_nt(q, k) * intra_ref[h]).astype(BF16)
        o = _dot(scores, v) + _dot(q, st.astype(BF16)) * rdec_ref[h]
        k_dec = (k.astype(F32) * kdec_ref[h]).astype(BF16)
        st_ref[0, h] = sdec_ref[h] * st + _dot_tn(k_dec, v)
        mu = jnp.mean(o, axis=-1, keepdims=True)
        d = o - mu
        var = jnp.mean(d * d, axis=-1, keepdims=True)
        on = d * lax.rsqrt(var + EPS)
        r = rg_ref[:, h * DV_B:(h + 1) * DV_B]
        o_ref[:, h * DV_B:(h + 1) * DV_B] = (on * (r * jax.nn.sigmoid(r))).astype(BF16)


def _retention_consts(length):
    lg = jnp.log1p(-(2.0 ** (-5.0 - jnp.arange(N_HEADS_B, dtype=F32))))
    n = jnp.arange(length, dtype=F32)
    diff = n[:, None] - n[None, :]
    intra = jnp.where(diff >= 0, jnp.exp(jnp.maximum(diff, 0.0) * lg[:, None, None]), 0.0)
    rdec = jnp.exp((n[None, :] + 1.0) * lg[:, None])[..., None]
    kdec = jnp.exp((length - 1.0 - n)[None, :] * lg[:, None])[..., None]
    sdec = jnp.exp(length * lg)
    return intra, rdec, kdec, sdec


def _retention(q, k, v, rg, s0, streams, chunks, length):
    n = q.shape[0]
    intra, rdec, kdec, sdec = _retention_consts(length)
    row = lambda width: pl.BlockSpec((length, width), lambda b, c: (b * chunks + c, 0))
    const = lambda a: pl.BlockSpec(a.shape, lambda b, c: (0,) * a.ndim)
    st_spec = pl.BlockSpec((1, N_HEADS_B, DK_B, DV_B), lambda b, c: (b, 0, 0, 0))
    return pl.pallas_call(
        _ret_body,
        grid=(streams, chunks),
        in_specs=[row(Q_B), row(Q_B), row(V_B), row(V_B), st_spec,
                  const(intra), const(rdec), const(kdec), pl.BlockSpec(memory_space=pltpu.SMEM)],
        out_specs=[row(V_B), st_spec],
        out_shape=[jax.ShapeDtypeStruct((n, V_B), BF16),
                   jax.ShapeDtypeStruct((streams, N_HEADS_B, DK_B, DV_B), F32)],
        compiler_params=_params(("parallel", "arbitrary")),
        name="retention",
    )(q, k, v, rg, s0, intra, rdec, kdec, sdec)


def _merge_body(x_ref, oa_ref, ob_ref, ga_ref, gb_ref, wa_ref, wb_ref, wo_ref, o_ref):
    ya = _dot(oa_ref[...], wa_ref[...])
    yb = _dot(ob_ref[...], wb_ref[...])
    mix = jax.nn.sigmoid(ga_ref[...]) * ya + jax.nn.sigmoid(gb_ref[...]) * yb
    o_ref[...] = x_ref[...] + _dot(mix.astype(BF16), wo_ref[...])


def _merge(x, oa, ob, ga, gb, wa, wb, wo, tm):
    n = x.shape[0]
    row = lambda width: pl.BlockSpec((tm, width), lambda i: (i, 0))
    full = lambda a: pl.BlockSpec(a.shape, lambda i: (0, 0))
    return pl.pallas_call(
        _merge_body,
        grid=(n // tm,),
        in_specs=[row(D_MODEL), row(Q_A), row(V_B), row(D_MODEL), row(D_MODEL), full(wa), full(wb), full(wo)],
        out_specs=row(D_MODEL),
        out_shape=jax.ShapeDtypeStruct((n, D_MODEL), F32),
        compiler_params=_params(("parallel",)),
        name="merge",
    )(x, oa, ob, ga, gb, wa, wb, wo)


def _top16(s, order):
    t = s.shape[1]
    rank = lax.broadcasted_iota(I32, (PEER_TOPK, t), 0)
    vals = jnp.zeros((PEER_TOPK, t), F32)
    ids = jnp.zeros((PEER_TOPK, t), F32)
    for r in range(PEER_TOPK):
        m = jnp.max(s, axis=0, keepdims=True)
        pick = jnp.min(jnp.where(s == m, order, 3e38), axis=0, keepdims=True)
        s = jnp.where(order == pick, -jnp.inf, s)
        vals = jnp.where(rank == r, m, vals)
        ids = jnp.where(rank == r, pick, ids)
    return vals, ids


def _route_body(x_ref, g_ref, wq_ref, k1_ref, k2_ref, idx_ref, gate_ref, h_ref):
    h_ref[...] = _rmsnorm(x_ref[...], g_ref[...])
    h = h_ref[...].astype(BF16)
    t = h.shape[0]
    q_t = _dot_nt(wq_ref[...], h).astype(BF16)
    iota = lambda rows: lax.broadcasted_iota(I32, (rows, t), 0).astype(F32)
    key_id = iota(N_KEYS)
    flat = jnp.concatenate(
        [iota(PEER_TOPK)] + [a * PEER_TOPK + iota(8) for a in range(1, 8)] + [(8 + iota(8)) * PEER_TOPK],
        axis=0)
    rank = lax.broadcasted_iota(I32, (PEER_TOPK, t), 0)
    idx_rows, gate_rows = [], []
    for hd in range(PEER_HEADS):
        base = hd * PEER_QDIM
        s1 = _dot(k1_ref[...], q_t[base:base + PEER_HALF])
        s2 = _dot(k2_ref[...], q_t[base + PEER_HALF:base + PEER_QDIM])
        v1, i1 = _top16(s1, key_id)
        v2, i2 = _top16(s2, key_id)
        cand = jnp.concatenate(
            [v1[0:1] + v2]
            + [v1[a:a + 1] + v2[0:8] for a in range(1, 8)]
            + [v1[8:16] + v2[0:1]], axis=0)
        cidx = jnp.concatenate(
            [i1[0:1] * N_KEYS + i2]
            + [i1[a:a + 1] * N_KEYS + i2[0:8] for a in range(1, 8)]
            + [i1[8:16] * N_KEYS + i2[0:1]], axis=0)
        best, pos = _top16(cand, flat)
        eidx = jnp.zeros((PEER_TOPK, t), F32)
        for r in range(PEER_TOPK):
            picked = jnp.max(jnp.where(flat == pos[r:r + 1], cidx, -1.0), axis=0, keepdims=True)
            eidx = jnp.where(rank == r, picked, eidx)
        e = jnp.exp(best - best[0:1])
        gate_rows.append(e / jnp.sum(e, axis=0, keepdims=True))
        idx_rows.append(eidx.astype(I32))
    idx_ref[...] = jnp.concatenate(idx_rows, axis=0)
    gate_ref[...] = jnp.concatenate(gate_rows, axis=0)


def _route(x, g, wq_t, k1, k2, tm, first_row, n):
    first_block = first_row // tm
    full = lambda a: pl.BlockSpec(a.shape, lambda i: (0, 0))
    col = pl.BlockSpec((PICKS, tm), lambda i: (0, i))
    return pl.pallas_call(
        _route_body,
        grid=(n // tm,),
        in_specs=[pl.BlockSpec((tm, D_MODEL), lambda i: (i + first_block, 0)), full(g), full(wq_t), full(k1),
                  full(k2)],
        out_specs=[col, col, pl.BlockSpec((tm, D_MODEL), lambda i: (i, 0))],
        out_shape=[jax.ShapeDtypeStruct((PICKS, n), I32), jax.ShapeDtypeStruct((PICKS, n), F32),
                   jax.ShapeDtypeStruct((n, D_MODEL), F32)],
        compiler_params=_params(("parallel",)),
        name="routing",
    )(x, g, wq_t, k1, k2)


SC_CORES = 2
SC_SUBCORES = 16
SC_LANES = 16
SC_CHUNK = 16
SC_PROMPT_ROWS = 18432


def _gelu_via_exp(x):
    z = 0.7978845608028654 * (x + 0.044715 * x * x * x)
    return 0.5 * x * (2.0 - 2.0 / (1.0 + jnp.exp(2.0 * z)))


def _sc_experts(table, idx, gate, h, first):
    n = idx.shape[0]
    workers = SC_CORES * SC_SUBCORES
    per_worker = n // workers
    n_chunks = PICKS // SC_CHUNK
    steps = D_MODEL // SC_LANES
    mesh = plsc.VectorSubcoreMesh(core_axis_name="c", subcore_axis_name="s",
                                  num_cores=SC_CORES, num_subcores=SC_SUBCORES)

    def body(tab_hbm, idx_hbm, gate_hbm, h_hbm, y_hbm,
             idx_0, idx_1, gate_0, gate_1, h_0, h_1, y_0, y_1, rows_a, rows_b,
             in_sem_0, in_sem_1, y_sem_0, y_sem_1, sem_a, sem_b):
        base = (lax.axis_index("s") * SC_CORES + lax.axis_index("c")) * per_worker
        last = base + per_worker - 1
        sets = ((idx_0, gate_0, h_0, y_0, in_sem_0, y_sem_0), (idx_1, gate_1, h_1, y_1, in_sem_1, y_sem_1))
        bufs = ((rows_a, sem_a), (rows_b, sem_b))

        def in_copies(t, q):
            idx_v, gate_v, h_v, _, sem, _ = sets[q]
            return (pltpu.make_async_copy(idx_hbm.at[t], idx_v, sem),
                    pltpu.make_async_copy(gate_hbm.at[t], gate_v, sem),
                    pltpu.make_async_copy(h_hbm.at[t + first], h_v, sem))

        def y_copy(t, q):
            return pltpu.make_async_copy(sets[q][3], y_hbm.at[t], sets[q][5])

        def gather(c, b, q):
            rows, sem = bufs[b]
            picks = sets[q][0].at[pl.ds(c * SC_CHUNK, SC_CHUNK)]
            return pltpu.make_async_copy(tab_hbm.at[picks], rows, sem)

        def compute(c, b, q):
            rows, _ = bufs[b]
            _, gate_v, h_v, y_v, _, _ = sets[q]
            u_of = lambda word: lax.bitcast_convert_type(word & jnp.int32(-65536), F32)
            v_of = lambda word: lax.bitcast_convert_type(word << 16, F32)

            def dot_step(j, accs):
                cols = pl.ds(j * SC_LANES, SC_LANES)
                hj = h_v[cols]
                return tuple(accs[k] + u_of(rows[k, 0, cols]) * hj for k in range(SC_CHUNK))

            accs = plsc.parallel_loop(0, steps, carry=tuple(jnp.zeros((SC_LANES,), F32) for _ in range(SC_CHUNK)))(
                dot_step)
            lane = lax.iota(I32, SC_LANES)
            s = jnp.zeros((SC_LANES,), F32)
            for k in range(SC_CHUNK):
                s = jnp.where(lane == k, jnp.sum(accs[k]), s)
            w = _gelu_via_exp(s) * gate_v[pl.ds(c * SC_CHUNK, SC_CHUNK)]
            ws = [jnp.full((SC_LANES,), jnp.sum(jnp.where(lane == k, w, 0.0)), F32) for k in range(SC_CHUNK)]

            @plsc.parallel_loop(0, steps)
            def _(j):
                cols = pl.ds(j * SC_LANES, SC_LANES)
                terms = [ws[k] * v_of(rows[k, 0, cols]) for k in range(SC_CHUNK)]
                while len(terms) > 1:
                    terms = [terms[i] + terms[i + 1] for i in range(0, len(terms), 2)]
                y_v[cols] = y_v[cols] + terms[0]

        def token(t, q, first):
            nxt = jnp.minimum(t + 1, last)
            for cp in in_copies(nxt, 1 - q):
                cp.start()
            y_v = sets[q][3]

            @pl.when(jnp.logical_not(first))
            def _():
                y_copy(t, q).wait()

            @pl.loop(0, steps)
            def _(j):
                y_v[pl.ds(j * SC_LANES, SC_LANES)] = jnp.zeros((SC_LANES,), F32)

            @pl.loop(0, n_chunks // 2)
            def _(p):
                c0 = p * 2
                gather(c0 + 1, 1, q).start()
                gather(c0, 0, q).wait()
                compute(c0, 0, q)

                @pl.when(p + 1 < n_chunks // 2)
                def _():
                    gather(c0 + 2, 0, q).start()

                gather(c0 + 1, 1, q).wait()
                compute(c0 + 1, 1, q)

            for cp in in_copies(nxt, 1 - q):
                cp.wait()
            gather(0, 0, 1 - q).start()
            y_copy(t, q).start()

        for cp in in_copies(base, 0):
            cp.start()
        for cp in in_copies(base, 0):
            cp.wait()
        gather(0, 0, 0).start()

        @pl.loop(0, per_worker // 2)
        def _(i):
            token(base + 2 * i, 0, i == 0)
            token(base + 2 * i + 1, 1, i == 0)

        gather(0, 0, 0).wait()
        y_copy(last - 1, 0).wait()
        y_copy(last, 1).wait()

    vec = lambda n_el, dt: [pltpu.VMEM((n_el,), dt), pltpu.VMEM((n_el,), dt)]
    return pl.kernel(
        body, out_type=jax.ShapeDtypeStruct((n, D_MODEL), F32), mesh=mesh,
        scratch_types=vec(PICKS, I32) + vec(PICKS, F32) + vec(D_MODEL, F32) + vec(D_MODEL, F32)
        + [pltpu.VMEM((SC_CHUNK, 1, D_MODEL), I32) for _ in range(2)]
        + [pltpu.SemaphoreType.DMA] * 6,
        compiler_params=pltpu.CompilerParams(needs_layout_passes=False),
        name="experts_sc",
    )(table, idx, gate, h)


def _finish_body(x_ref, y_ref, gf_ref, *rest_and_out):
    rest_and_out[-1][...] = _rmsnorm(x_ref[...] + y_ref[...], gf_ref[...])


def _finish(x, y, g_final, rest, tm, first_row=0):
    first_block = first_row // tm
    row = pl.BlockSpec((tm, D_MODEL), lambda i: (i + first_block, 0))
    extra = () if rest is None else (rest,)
    return pl.pallas_call(
        _finish_body,
        grid=(y.shape[0] // tm,),
        in_specs=[row, pl.BlockSpec((tm, D_MODEL), lambda i: (i, 0)), pl.BlockSpec(g_final.shape, lambda i: (0, 0))]
        + [pl.BlockSpec(memory_space=pl.ANY) for _ in extra],
        out_specs=row,
        out_shape=jax.ShapeDtypeStruct(x.shape if rest is not None else y.shape, F32),
        input_output_aliases={3: 0} if rest is not None else {},
        compiler_params=_params(("parallel",)),
        name="finish",
    )(x, y, g_final, *extra)


EXP_TOKENS = 256
EXP_SLOTS = 4


def _expert_body(x_ref, g_ref, gf_ref, idx_hbm, gate_ref, tab_hbm, o_ref,
                 idx_smem, buf0, buf1, buf2, buf3, h_buf, y_buf, sems, idx_sem):
    bufs = (buf0, buf1, buf2, buf3)
    ahead = EXP_SLOTS - 1
    i = pl.program_id(0)
    idx_copy = pltpu.make_async_copy(idx_hbm.at[pl.ds(i * (EXP_TOKENS * PICKS), EXP_TOKENS * PICKS)],
                                     idx_smem, idx_sem)
    idx_copy.start()
    h_buf[...] = _rmsnorm(x_ref[...], g_ref[...])
    idx_copy.wait()

    def issue(t, slot):
        for k in range(PICKS):
            e = idx_smem[t * PICKS + k]
            pltpu.async_copy(tab_hbm.at[e], bufs[slot].at[pl.ds(k, 1), :], sems.at[slot], priority=k % 2)

    def wait_slot(slot):
        pltpu.make_async_copy(bufs[(slot + 1) % EXP_SLOTS], bufs[slot], sems.at[slot]).wait()

    lane = lax.broadcasted_iota(I32, (PICKS, EXP_TOKENS), 1)

    def compute(t, slot):
        h = h_buf[pl.ds(t, 1), :]
        u = lax.bitcast_convert_type(bufs[slot][...] & jnp.int32(-65536), F32)
        s = jnp.sum(u * h, axis=-1, keepdims=True)
        gate = jnp.sum(jnp.where(lane == t, gate_ref[...], 0.0), axis=-1, keepdims=True)
        w = gate * jax.nn.gelu(s)
        v = lax.bitcast_convert_type(bufs[slot][...] << 16, F32)
        y_buf[pl.ds(t, 1), :] = jnp.sum(w * v, axis=0, keepdims=True)

    def group(t0, issue_upto):
        for s in range(EXP_SLOTS):
            wait_slot(s)
            if s < issue_upto:
                issue(t0 + s + ahead, (s + ahead) % EXP_SLOTS)
            compute(t0 + s, s)

    for s in range(ahead):
        issue(s, s)

    def steady(q, carry):
        group(q * EXP_SLOTS, EXP_SLOTS)
        return carry

    n_groups = EXP_TOKENS // EXP_SLOTS
    lax.fori_loop(0, n_groups - 1, steady, 0)
    group((n_groups - 1) * EXP_SLOTS, EXP_SLOTS - ahead)
    o_ref[...] = _rmsnorm(x_ref[...] + y_buf[...], gf_ref[...])


def _experts(x, g, g_final, idx_flat, gate_t, table, first_token):
    n = x.shape[0]
    first_block = first_token // EXP_TOKENS
    full = lambda a: pl.BlockSpec(a.shape, lambda i: (0, 0))
    row = pl.BlockSpec((EXP_TOKENS, D_MODEL), lambda i: (i + first_block, 0))
    return pl.pallas_call(
        _expert_body,
        grid=(n // EXP_TOKENS - first_block,),
        in_specs=[row, full(g), full(g_final),
                  pl.BlockSpec(memory_space=pl.ANY),
                  pl.BlockSpec((PICKS, EXP_TOKENS), lambda i: (0, i)),
                  pl.BlockSpec(memory_space=pl.ANY)],
        out_specs=row,
        out_shape=jax.ShapeDtypeStruct((n, D_MODEL), F32),
        scratch_shapes=[pltpu.SMEM((EXP_TOKENS * PICKS,), I32)]
        + [pltpu.VMEM((PICKS, D_MODEL), I32) for _ in range(EXP_SLOTS)]
        + [pltpu.VMEM((EXP_TOKENS, D_MODEL), F32),
           pltpu.VMEM((EXP_TOKENS, D_MODEL), F32),
           pltpu.SemaphoreType.DMA((EXP_SLOTS,)),
           pltpu.SemaphoreType.DMA(())],
        compiler_params=pltpu.CompilerParams(dimension_semantics=("arbitrary",), vmem_limit_bytes=VMEM_LIMIT,
                                             disable_bounds_checks=True),
        name="experts",
    )(x, g, g_final, idx_flat, gate_t, table)


def _attention_masks(tq):
    i = jnp.arange(tq, dtype=I32)[:, None]
    r = jnp.arange(ATT_KEYS, dtype=I32)[None, :]
    negdist = -jnp.abs(WINDOW + i - r).astype(F32)
    return negdist, i, r


def _layer(x, n_groups, rows, swa_inputs, ret_state, ret_len, w, tm, n_sc):
    (norm_mix, w_in, b_gate, sinks, wa, wb, wo, norm_ffn, wq_t, k1, k2, packed, norm_final) = w
    assert n_sc < x.shape[0]
    qa, ka, va, qb, kb, vb, rg, ga, gb = _project(x, norm_mix, w_in, b_gate, tm)
    oa, k_rows, v_rows = swa_inputs(qa, ka, va, sinks)
    ob, s_fin = _retention(qb, kb, vb, rg, ret_state, n_groups, rows // ret_len, ret_len)
    x1 = _merge(x, oa, ob, ga, gb, wa, wb, wo, tm)
    if n_sc:
        idx_t, gate_t, h2 = _route(x1, norm_ffn, wq_t, k1, k2, tm, 0, n_sc)
        y_sc = _sc_experts(packed, idx_t.T, gate_t.T, h2, 0)
    idx_t, gate_t, _ = _route(x1, norm_ffn, wq_t, k1, k2, tm, n_sc, x.shape[0] - n_sc)
    y = _experts(x1, norm_ffn, norm_final, idx_t.T.reshape(-1), gate_t, packed, n_sc)
    if n_sc:
        y = _finish(x1, y_sc, norm_final, y, tm)
    return y, k_rows, v_rows, s_fin


def kernel(x_prompt, x_sample, cache_swa_k, cache_swa_v, state_ret, norm_mix, w_in, b_gate, attn_sinks,
           w_branch_a, w_branch_b, w_out, norm_ffn, peer_w_q, peer_sub_k1, peer_sub_k2, peer_u, peer_v,
           norm_final):
    batch, seq, _ = x_prompt.shape
    dec_batch, dec_seq, _ = x_sample.shape
    assert norm_mix.shape[0] == 1, "single-layer trunk"
    half = lambda a: lax.bitcast_convert_type(a.astype(BF16), jnp.uint16).astype(jnp.uint32)
    packed = lax.bitcast_convert_type((half(peer_u[0]) << 16) | half(peer_v[0]), I32)[:, None, :]
    table, x_prompt, x_sample = lax.optimization_barrier((packed, x_prompt, x_sample))
    w = (norm_mix[0][None], w_in[0].astype(BF16), b_gate[0][None], attn_sinks[0],
         w_branch_a[0].astype(BF16), w_branch_b[0].astype(BF16), w_out[0].astype(BF16),
         norm_ffn[0][None], peer_w_q[0].T.astype(BF16), peer_sub_k1[0].astype(BF16),
         peer_sub_k2[0].astype(BF16), table, norm_final[None])

    def swa_prompt(qa, ka, va, sinks):
        streams = qa.shape[0] // seq
        tq = 2 * CHUNK
        negdist, i, r = _attention_masks(tq)
        lq, lk = i // CHUNK, r // CHUNK
        band = (lk >= lq) & (lk <= lq + 2)
        allow = jnp.stack([band & (r >= WINDOW), band]).astype(F32)
        oa = _swa(sinks, qa, ka, va, negdist, allow, streams, seq // tq, tq)
        tail = lambda a: a.reshape(streams, seq, N_KV_A, HEAD_DIM_A)[:, seq - WINDOW:]
        return oa, tail(ka), tail(va)

    def swa_sample(qa, ka, va, sinks):
        negdist, i, r = _attention_masks(dec_seq)
        visible = jnp.broadcast_to(r < WINDOW + dec_seq, (dec_seq, ATT_KEYS))
        allow = jnp.stack([visible, visible]).astype(F32)
        pad = jnp.zeros((dec_batch, ATT_KEYS - WINDOW - dec_seq, KV_A), F32)
        k_all = jnp.concatenate([cache_swa_k[0].reshape(dec_batch, WINDOW, KV_A),
                                 ka.reshape(dec_batch, dec_seq, KV_A), pad], axis=1)
        v_all = jnp.concatenate([cache_swa_v[0].reshape(dec_batch, WINDOW, KV_A),
                                 va.reshape(dec_batch, dec_seq, KV_A), pad], axis=1)
        oa = _swa(sinks, qa, k_all.reshape(-1, KV_A), v_all.reshape(-1, KV_A), negdist, allow,
                  dec_batch, 1, dec_seq)
        tail = lambda a: a[:, dec_seq:WINDOW + dec_seq].reshape(dec_batch, WINDOW, N_KV_A, HEAD_DIM_A)
        return oa, tail(k_all), tail(v_all)

    ys, ks, vs, ss = _layer(x_sample.reshape(dec_batch * dec_seq, D_MODEL), dec_batch, dec_seq, swa_sample,
                            state_ret[0], dec_seq, w, 128, 0)
    s0 = jnp.zeros((batch, N_HEADS_B, DK_B, DV_B), F32)
    yp, kp, vp, sp = _layer(x_prompt.reshape(batch * seq, D_MODEL), batch, seq, swa_prompt, s0, 256, w, 256,
                            SC_PROMPT_ROWS)

    return (yp.reshape(batch, seq, D_MODEL), ys.reshape(dec_batch, dec_seq, D_MODEL),
            kp[None], vp[None], sp[None], ks[None], vs[None], ss[None])
```

```python
import functools

import jax
import jax.numpy as jnp
from jax import lax
from jax.experimental import pallas as pl
from jax.experimental.pallas import tpu as pltpu
from jax.experimental.pallas import tpu_sc as plsc

F32 = jnp.float32
BF16 = jnp.bfloat16
I32 = jnp.int32

D_MODEL = 1024
CHUNK = 64
EPS = 1e-6
NEG_INF = -1e30
PAST_LEN = 2048

N_HEADS_A = 8
N_KV_A = 2
GROUP_A = N_HEADS_A // N_KV_A
HEAD_DIM_A = 64
WINDOW = 128
N_HEADS_B = 4
DK_B = 128
DV_B = 256
Q_A = N_HEADS_A * HEAD_DIM_A
KV_A = N_KV_A * HEAD_DIM_A
Q_B = N_HEADS_B * DK_B
V_B = N_HEADS_B * DV_B
D_IN = Q_A + 2 * KV_A + 2 * Q_B + 2 * V_B + 2 * D_MODEL
OFF_QA = 0
OFF_KA = OFF_QA + Q_A
OFF_VA = OFF_KA + KV_A
OFF_QB = OFF_VA + KV_A
OFF_KB = OFF_QB + Q_B
OFF_VB = OFF_KB + Q_B
OFF_RG = OFF_VB + V_B
OFF_GA = OFF_RG + V_B
OFF_GB = OFF_GA + D_MODEL

N_KEYS = 128
N_EXPERTS = N_KEYS * N_KEYS
PEER_HEADS = 8
PEER_QDIM = 256
PEER_HALF = PEER_QDIM // 2
PEER_TOPK = 16
PICKS = PEER_HEADS * PEER_TOPK

LANES = 128
ATT_KEYS = 2 * WINDOW
VMEM_LIMIT = 56 * 1024 * 1024


def _rmsnorm(x, g):
    return x * lax.rsqrt(jnp.mean(x * x, axis=-1, keepdims=True) + EPS) * g


def _dot(a, b):
    return jnp.dot(a, b, preferred_element_type=F32)


def _dot_nt(a, b):
    return lax.dot_general(a, b, (((1,), (1,)), ((), ())), preferred_element_type=F32)


def _dot_tn(a, b):
    return lax.dot_general(a, b, (((0,), (0,)), ((), ())), preferred_element_type=F32)


def _params(sem):
    return pltpu.CompilerParams(dimension_semantics=sem, vmem_limit_bytes=VMEM_LIMIT)


def _proj_body(x_ref, g_ref, w_ref, bg_ref, qa_ref, ka_ref, va_ref, qb_ref, kb_ref, vb_ref,
               rg_ref, ga_ref, gb_ref):
    h = _rmsnorm(x_ref[...], g_ref[...]).astype(BF16)

    def mm(lo, width):
        return _dot(h, w_ref[:, lo:lo + width])

    qa_ref[...] = (mm(OFF_QA, Q_A) * (HEAD_DIM_A ** -0.5)).astype(BF16)
    ka_ref[...] = mm(OFF_KA, KV_A)
    va_ref[...] = mm(OFF_VA, KV_A)
    qb_ref[...] = mm(OFF_QB, Q_B).astype(BF16)
    kb_ref[...] = (mm(OFF_KB, Q_B) * (DK_B ** -0.5)).astype(BF16)
    vb_ref[...] = mm(OFF_VB, V_B).astype(BF16)
    rg_ref[...] = mm(OFF_RG, V_B)
    ga_ref[...] = mm(OFF_GA, D_MODEL) + bg_ref[:, :D_MODEL]
    gb_ref[...] = mm(OFF_GB, D_MODEL) + bg_ref[:, D_MODEL:]


def _project(x, g, w_bf16, b_gate, tm):
    n = x.shape[0]
    row = lambda width: pl.BlockSpec((tm, width), lambda i: (i, 0))
    full = lambda a: pl.BlockSpec(a.shape, lambda i: (0, 0))
    widths = (Q_A, KV_A, KV_A, Q_B, Q_B, V_B, V_B, D_MODEL, D_MODEL)
    dtypes = (BF16, F32, F32, BF16, BF16, BF16, F32, F32, F32)
    return pl.pallas_call(
        _proj_body,
        grid=(n // tm,),
        in_specs=[row(D_MODEL), full(g), full(w_bf16), full(b_gate)],
        out_specs=[row(w) for w in widths],
        out_shape=[jax.ShapeDtypeStruct((n, w), d) for w, d in zip(widths, dtypes)],
        compiler_params=_params(("parallel",)),
        name="projection",
    )(x, g, w_bf16, b_gate)


def _swa_body(sink_ref, q_ref, kp_ref, kc_ref, vp_ref, vc_ref, nd_ref, al_ref, o_ref):
    k = jnp.concatenate([kp_ref[...], kc_ref[...]], axis=0)
    v = jnp.concatenate([vp_ref[...], vc_ref[...]], axis=0)
    lane = lax.broadcasted_iota(I32, k.shape, 1)
    low = lane < HEAD_DIM_A
    k_sw = pltpu.roll(k, HEAD_DIM_A, axis=1)
    v_sw = pltpu.roll(v, HEAD_DIM_A, axis=1)
    negdist = nd_ref[...]
    allowed = al_ref[0] > 0.5
    for kv in range(N_KV_A):
        own = low if kv == 0 else jnp.logical_not(low)
        k_rep = jnp.where(own, k, k_sw)
        v_rep = jnp.where(own, v, v_sw)
        zero = jnp.zeros_like(k_rep)
        kk = jnp.concatenate([jnp.where(low, k_rep, zero), jnp.where(low, zero, k_rep)], axis=0).astype(BF16)
        vv = jnp.concatenate([jnp.where(low, v_rep, zero), jnp.where(low, zero, v_rep)], axis=0).astype(BF16)
        for a in range(GROUP_A // 2):
            col = kv * GROUP_A * HEAD_DIM_A + a * LANES
            s2 = _dot_nt(q_ref[:, col:col + LANES], kk)
            ps = []
            for u in range(2):
                head = kv * GROUP_A + 2 * a + u
                slope = 2.0 ** (-8.0 * (head + 1) / N_HEADS_A)
                sink = sink_ref[head]
                s = s2[:, u * ATT_KEYS:(u + 1) * ATT_KEYS] + slope * negdist
                s = jnp.where(allowed, s, NEG_INF)
                m = jnp.maximum(jnp.max(s, axis=-1, keepdims=True), sink)
                p = jnp.exp(s - m)
                denom = jnp.sum(p, axis=-1, keepdims=True) + jnp.exp(sink - m)
                ps.append((p / denom).astype(BF16))
            o = _dot(jnp.concatenate(ps, axis=1), vv)
            o_ref[:, col:col + LANES] = o.astype(BF16)


def _swa(sinks, q, k, v, negdist, allow, groups, q_blocks, tq):
    n = q.shape[0]
    kblocks = k.shape[0] // LANES // groups
    shift = kblocks - q_blocks
    prev = lambda b, j: (b * kblocks + jnp.maximum(j + shift - 1, 0), 0)
    cur = lambda b, j: (b * kblocks + j + shift, 0)
    kv_spec = lambda f: pl.BlockSpec((LANES, KV_A), f)
    return pl.pallas_call(
        _swa_body,
        grid=(groups, q_blocks),
        in_specs=[pl.BlockSpec(memory_space=pltpu.SMEM),
                  pl.BlockSpec((tq, Q_A), lambda b, j: (b * q_blocks + j, 0)),
                  kv_spec(prev), kv_spec(cur), kv_spec(prev), kv_spec(cur),
                  pl.BlockSpec((tq, ATT_KEYS), lambda b, j: (0, 0)),
                  pl.BlockSpec((1, tq, ATT_KEYS), lambda b, j: (jnp.minimum(j + shift, 1), 0, 0))],
        out_specs=pl.BlockSpec((tq, Q_A), lambda b, j: (b * q_blocks + j, 0)),
        out_shape=jax.ShapeDtypeStruct((n, Q_A), BF16),
        compiler_params=_params(("parallel", "parallel")),
        name="attention",
    )(sinks, q, k, k, v, v, negdist, allow)


def _ret_body(q_ref, k_ref, v_ref, rg_ref, s0_ref, intra_ref, rdec_ref, kdec_ref, sdec_ref,
              o_ref, st_ref):
    c = pl.program_id(1)

    @pl.when(c == 0)
    def _():
        st_ref[...] = s0_ref[...]

    for h in range(N_HEADS_B):
        q = q_ref[:, h * DK_B:(h + 1) * DK_B]
        k = k_ref[:, h * DK_B:(h + 1) * DK_B]
        v = v_ref[:, h * DV_B:(h + 1) * DV_B]
        st = st_ref[0, h]
        scores = (_dot_nt(q, k) * intra_ref[h]).astype(BF16)
        o = _dot(scores, v) + _dot(q, st.astype(BF16)) * rdec_ref[h]
        k_dec = (k.astype(F32) * kdec_ref[h]).astype(BF16)
        st_ref[0, h] = sdec_ref[h] * st + _dot_tn(k_dec, v)
        mu = jnp.mean(o, axis=-1, keepdims=True)
        d = o - mu
        var = jnp.mean(d * d, axis=-1, keepdims=True)
        on = d * lax.rsqrt(var + EPS)
        r = rg_ref[:, h * DV_B:(h + 1) * DV_B]
        o_ref[:, h * DV_B:(h + 1) * DV_B] = (on * (r * jax.nn.sigmoid(r))).astype(BF16)


def _retention_consts(length):
    lg = jnp.log1p(-(2.0 ** (-5.0 - jnp.arange(N_HEADS_B, dtype=F32))))
    n = jnp.arange(length, dtype=F32)
    diff = n[:, None] - n[None, :]
    intra = jnp.where(diff >= 0, jnp.exp(jnp.maximum(diff, 0.0) * lg[:, None, None]), 0.0)
    rdec = jnp.exp((n[None, :] + 1.0) * lg[:, None])[..., None]
    kdec = jnp.exp((length - 1.0 - n)[None, :] * lg[:, None])[..., None]
    sdec = jnp.exp(length * lg)
    return intra, rdec, kdec, sdec


def _retention(q, k, v, rg, s0, streams, chunks, length):
    n = q.shape[0]
    intra, rdec, kdec, sdec = _retention_consts(length)
    row = lambda width: pl.BlockSpec((length, width), lambda b, c: (b * chunks + c, 0))
    const = lambda a: pl.BlockSpec(a.shape, lambda b, c: (0,) * a.ndim)
    st_spec = pl.BlockSpec((1, N_HEADS_B, DK_B, DV_B), lambda b, c: (b, 0, 0, 0))
    return pl.pallas_call(
        _ret_body,
        grid=(streams, chunks),
        in_specs=[row(Q_B), row(Q_B), row(V_B), row(V_B), st_spec,
                  const(intra), const(rdec), const(kdec), pl.BlockSpec(memory_space=pltpu.SMEM)],
        out_specs=[row(V_B), st_spec],
        out_shape=[jax.ShapeDtypeStruct((n, V_B), BF16),
                   jax.ShapeDtypeStruct((streams, N_HEADS_B, DK_B, DV_B), F32)],
        compiler_params=_params(("parallel", "arbitrary")),
        name="retention",
    )(q, k, v, rg, s0, intra, rdec, kdec, sdec)


def _merge_body(x_ref, oa_ref, ob_ref, ga_ref, gb_ref, wa_ref, wb_ref, wo_ref, o_ref):
    ya = _dot(oa_ref[...], wa_ref[...])
    yb = _dot(ob_ref[...], wb_ref[...])
    mix = jax.nn.sigmoid(ga_ref[...]) * ya + jax.nn.sigmoid(gb_ref[...]) * yb
    o_ref[...] = x_ref[...] + _dot(mix.astype(BF16), wo_ref[...])


def _merge(x, oa, ob, ga, gb, wa, wb, wo, tm):
    n = x.shape[0]
    row = lambda width: pl.BlockSpec((tm, width), lambda i: (i, 0))
    full = lambda a: pl.BlockSpec(a.shape, lambda i: (0, 0))
    return pl.pallas_call(
        _merge_body,
        grid=(n // tm,),
        in_specs=[row(D_MODEL), row(Q_A), row(V_B), row(D_MODEL), row(D_MODEL), full(wa), full(wb), full(wo)],
        out_specs=row(D_MODEL),
        out_shape=jax.ShapeDtypeStruct((n, D_MODEL), F32),
        compiler_params=_params(("parallel",)),
        name="merge",
    )(x, oa, ob, ga, gb, wa, wb, wo)


def _top16(s, order):
    t = s.shape[1]
    rank = lax.broadcasted_iota(I32, (PEER_TOPK, t), 0)
    vals = jnp.zeros((PEER_TOPK, t), F32)
    ids = jnp.zeros((PEER_TOPK, t), F32)
    for r in range(PEER_TOPK):
        m = jnp.max(s, axis=0, keepdims=True)
        pick = jnp.min(jnp.where(s == m, order, 3e38), axis=0, keepdims=True)
        s = jnp.where(order == pick, -jnp.inf, s)
        vals = jnp.where(rank == r, m, vals)
        ids = jnp.where(rank == r, pick, ids)
    return vals, ids


def _oddeven_merge(lo, hi, r):
    step = r * 2
    if step < hi - lo:
        yield from _oddeven_merge(lo, hi, step)
        yield from _oddeven_merge(lo + r, hi, step)
        yield from ((i, i + r) for i in range(lo + r, hi - r, step))
    else:
        yield (lo, lo + r)


def _oddeven_sort(lo, hi):
    if hi - lo >= 1:
        mid = lo + (hi - lo) // 2
        yield from _oddeven_sort(lo, mid)
        yield from _oddeven_sort(mid + 1, hi)
        yield from _oddeven_merge(lo, hi, 1)


SUBLANES = 8
_SORT_PAIRS = tuple(_oddeven_sort(0, N_KEYS // SUBLANES - 1))


def _top16_keys(s):
    t = s.shape[1]
    n_lists = N_KEYS // SUBLANES
    sub = lax.broadcasted_iota(I32, (SUBLANES, t), 0).astype(F32)
    vals = [s[SUBLANES * i:SUBLANES * (i + 1)] for i in range(n_lists)]
    ids = [sub + float(SUBLANES * i) for i in range(n_lists)]
    for i, j in _SORT_PAIRS:
        a, b, ia, ib = vals[i], vals[j], ids[i], ids[j]
        swap = (b > a) | ((b == a) & (ib < ia))
        vals[i], vals[j] = jnp.where(swap, b, a), jnp.where(swap, a, b)
        ids[i], ids[j] = jnp.where(swap, ib, ia), jnp.where(swap, ia, ib)
    rank = lax.broadcasted_iota(I32, (PEER_TOPK, t), 0)
    out_v = jnp.zeros((PEER_TOPK, t), F32)
    out_i = jnp.zeros((PEER_TOPK, t), F32)
    for r in range(PEER_TOPK):
        m = jnp.max(vals[0], axis=0, keepdims=True)
        pick = jnp.min(jnp.where(vals[0] == m, ids[0], 3e38), axis=0, keepdims=True)
        hit = ids[0] == pick
        out_v = jnp.where(rank == r, m, out_v)
        out_i = jnp.where(rank == r, pick, out_i)
        for d in range(PEER_TOPK - 1 - r):
            vals[d] = jnp.where(hit, vals[d + 1], vals[d])
            ids[d] = jnp.where(hit, ids[d + 1], ids[d])
    return out_v, out_i


def _route_body(x_ref, g_ref, wq_ref, k1_ref, k2_ref, idx_ref, gate_ref, h_ref):
    h_ref[...] = _rmsnorm(x_ref[...], g_ref[...])
    h = h_ref[...].astype(BF16)
    t = h.shape[0]
    q_t = _dot_nt(wq_ref[...], h).astype(BF16)
    iota = lambda rows: lax.broadcasted_iota(I32, (rows, t), 0).astype(F32)
    flat = jnp.concatenate(
        [iota(PEER_TOPK)] + [a * PEER_TOPK + iota(8) for a in range(1, 8)] + [(8 + iota(8)) * PEER_TOPK],
        axis=0)
    rank = lax.broadcasted_iota(I32, (PEER_TOPK, t), 0)
    idx_rows, gate_rows = [], []
    for hd in range(PEER_HEADS):
        base = hd * PEER_QDIM
        s1 = _dot(k1_ref[...], q_t[base:base + PEER_HALF])
        s2 = _dot(k2_ref[...], q_t[base + PEER_HALF:base + PEER_QDIM])
        v1, i1 = _top16_keys(s1)
        v2, i2 = _top16_keys(s2)
        cand = jnp.concatenate(
            [v1[0:1] + v2]
            + [v1[a:a + 1] + v2[0:8] for a in range(1, 8)]
            + [v1[8:16] + v2[0:1]], axis=0)
        cidx = jnp.concatenate(
            [i1[0:1] * N_KEYS + i2]
            + [i1[a:a + 1] * N_KEYS + i2[0:8] for a in range(1, 8)]
            + [i1[8:16] * N_KEYS + i2[0:1]], axis=0)
        best, pos = _top16(cand, flat)
        eidx = jnp.zeros((PEER_TOPK, t), F32)
        for r in range(PEER_TOPK):
            picked = jnp.max(jnp.where(flat == pos[r:r + 1], cidx, -1.0), axis=0, keepdims=True)
            eidx = jnp.where(rank == r, picked, eidx)
        e = jnp.exp(best - best[0:1])
        gate_rows.append(e / jnp.sum(e, axis=0, keepdims=True))
        idx_rows.append(eidx.astype(I32))
    idx_ref[...] = jnp.concatenate(idx_rows, axis=0)
    gate_ref[...] = jnp.concatenate(gate_rows, axis=0)


def _route(x, g, wq_t, k1, k2, tm, first_row, n):
    first_block = first_row // tm
    full = lambda a: pl.BlockSpec(a.shape, lambda i: (0, 0))
    col = pl.BlockSpec((PICKS, tm), lambda i: (0, i))
    return pl.pallas_call(
        _route_body,
        grid=(n // tm,),
        in_specs=[pl.BlockSpec((tm, D_MODEL), lambda i: (i + first_block, 0)), full(g), full(wq_t), full(k1),
                  full(k2)],
        out_specs=[col, col, pl.BlockSpec((tm, D_MODEL), lambda i: (i, 0))],
        out_shape=[jax.ShapeDtypeStruct((PICKS, n), I32), jax.ShapeDtypeStruct((PICKS, n), F32),
                   jax.ShapeDtypeStruct((n, D_MODEL), F32)],
        compiler_params=_params(("parallel",)),
        name="routing",
    )(x, g, wq_t, k1, k2)


SC_CORES = 2
SC_SUBCORES = 16
SC_LANES = 16
SC_CHUNK = 16
SC_PROMPT_ROWS = 18432


def _gelu_via_exp(x):
    z = 0.7978845608028654 * (x + 0.044715 * x * x * x)
    return 0.5 * x * (2.0 - 2.0 / (1.0 + jnp.exp(2.0 * z)))


def _sc_experts(table, idx, gate, h, first):
    n = idx.shape[0]
    workers = SC_CORES * SC_SUBCORES
    per_worker = n // workers
    n_chunks = PICKS // SC_CHUNK
    steps = D_MODEL // SC_LANES
    mesh = plsc.VectorSubcoreMesh(core_axis_name="c", subcore_axis_name="s",
                                  num_cores=SC_CORES, num_subcores=SC_SUBCORES)

    def body(tab_hbm, idx_hbm, gate_hbm, h_hbm, y_hbm,
             idx_0, idx_1, gate_0, gate_1, h_0, h_1, y_0, y_1, rows_a, rows_b,
             in_sem_0, in_sem_1, y_sem_0, y_sem_1, sem_a, sem_b):
        base = (lax.axis_index("s") * SC_CORES + lax.axis_index("c")) * per_worker
        last = base + per_worker - 1
        sets = ((idx_0, gate_0, h_0, y_0, in_sem_0, y_sem_0), (idx_1, gate_1, h_1, y_1, in_sem_1, y_sem_1))
        bufs = ((rows_a, sem_a), (rows_b, sem_b))

        def in_copies(t, q):
            idx_v, gate_v, h_v, _, sem, _ = sets[q]
            return (pltpu.make_async_copy(idx_hbm.at[t], idx_v, sem),
                    pltpu.make_async_copy(gate_hbm.at[t], gate_v, sem),
                    pltpu.make_async_copy(h_hbm.at[t + first], h_v, sem))

        def y_copy(t, q):
            return pltpu.make_async_copy(sets[q][3], y_hbm.at[t], sets[q][5])

        def gather(c, b, q):
            rows, sem = bufs[b]
            picks = sets[q][0].at[pl.ds(c * SC_CHUNK, SC_CHUNK)]
            return pltpu.make_async_copy(tab_hbm.at[picks], rows, sem)

        def compute(c, b, q):
            rows, _ = bufs[b]
            _, gate_v, h_v, y_v, _, _ = sets[q]
            u_of = lambda word: lax.bitcast_convert_type(word & jnp.int32(-65536), F32)
            v_of = lambda word: lax.bitcast_convert_type(word << 16, F32)

            def dot_step(j, accs):
                cols = pl.ds(j * SC_LANES, SC_LANES)
                hj = h_v[cols]
                return tuple(accs[k] + u_of(rows[k, 0, cols]) * hj for k in range(SC_CHUNK))

            accs = plsc.parallel_loop(0, steps, carry=tuple(jnp.zeros((SC_LANES,), F32) for _ in range(SC_CHUNK)))(
                dot_step)
            lane = lax.iota(I32, SC_LANES)
            s = jnp.zeros((SC_LANES,), F32)
            for k in range(SC_CHUNK):
                s = jnp.where(lane == k, jnp.sum(accs[k]), s)
            w = _gelu_via_exp(s) * gate_v[pl.ds(c * SC_CHUNK, SC_CHUNK)]
            ws = [jnp.full((SC_LANES,), jnp.sum(jnp.where(lane == k, w, 0.0)), F32) for k in range(SC_CHUNK)]

            @plsc.parallel_loop(0, steps)
            def _(j):
                cols = pl.ds(j * SC_LANES, SC_LANES)
                terms = [ws[k] * v_of(rows[k, 0, cols]) for k in range(SC_CHUNK)]
                while len(terms) > 1:
                    terms = [terms[i] + terms[i + 1] for i in range(0, len(terms), 2)]
                y_v[cols] = y_v[cols] + terms[0]

        def token(t, q, first):
            nxt = jnp.minimum(t + 1, last)
            for cp in in_copies(nxt, 1 - q):
                cp.start()
            y_v = sets[q][3]

            @pl.when(jnp.logical_not(first))
            def _():
                y_copy(t, q).wait()

            @pl.loop(0, steps)
            def _(j):
                y_v[pl.ds(j * SC_LANES, SC_LANES)] = jnp.zeros((SC_LANES,), F32)

            @pl.loop(0, n_chunks // 2)
            def _(p):
                c0 = p * 2
                gather(c0 + 1, 1, q).start()
                gather(c0, 0, q).wait()
                compute(c0, 0, q)

                @pl.when(p + 1 < n_chunks // 2)
                def _():
                    gather(c0 + 2, 0, q).start()

                gather(c0 + 1, 1, q).wait()
                compute(c0 + 1, 1, q)

            for cp in in_copies(nxt, 1 - q):
                cp.wait()
            gather(0, 0, 1 - q).start()
            y_copy(t, q).start()

        for cp in in_copies(base, 0):
            cp.start()
        for cp in in_copies(base, 0):
            cp.wait()
        gather(0, 0, 0).start()

        @pl.loop(0, per_worker // 2)
        def _(i):
            token(base + 2 * i, 0, i == 0)
            token(base + 2 * i + 1, 1, i == 0)

        gather(0, 0, 0).wait()
        y_copy(last - 1, 0).wait()
        y_copy(last, 1).wait()

    vec = lambda n_el, dt: [pltpu.VMEM((n_el,), dt), pltpu.VMEM((n_el,), dt)]
    return pl.kernel(
        body, out_type=jax.ShapeDtypeStruct((n, D_MODEL), F32), mesh=mesh,
        scratch_types=vec(PICKS, I32) + vec(PICKS, F32) + vec(D_MODEL, F32) + vec(D_MODEL, F32)
        + [pltpu.VMEM((SC_CHUNK, 1, D_MODEL), I32) for _ in range(2)]
        + [pltpu.SemaphoreType.DMA] * 6,
        compiler_params=pltpu.CompilerParams(needs_layout_passes=False),
        name="experts_sc",
    )(table, idx, gate, h)


def _finish_body(x_ref, y_ref, gf_ref, *rest_and_out):
    rest_and_out[-1][...] = _rmsnorm(x_ref[...] + y_ref[...], gf_ref[...])


def _finish(x, y, g_final, rest, tm, first_row=0):
    first_block = first_row // tm
    row = pl.BlockSpec((tm, D_MODEL), lambda i: (i + first_block, 0))
    extra = () if rest is None else (rest,)
    return pl.pallas_call(
        _finish_body,
        grid=(y.shape[0] // tm,),
        in_specs=[row, pl.BlockSpec((tm, D_MODEL), lambda i: (i, 0)), pl.BlockSpec(g_final.shape, lambda i: (0, 0))]
        + [pl.BlockSpec(memory_space=pl.ANY) for _ in extra],
        out_specs=row,
        out_shape=jax.ShapeDtypeStruct(x.shape if rest is not None else y.shape, F32),
        input_output_aliases={3: 0} if rest is not None else {},
        compiler_params=_params(("parallel",)),
        name="finish",
    )(x, y, g_final, *extra)


EXP_TOKENS = 256
EXP_SLOTS = 4


def _expert_body(x_ref, g_ref, gf_ref, idx_hbm, gate_ref, tab_hbm, o_ref,
                 idx_smem, buf0, buf1, buf2, buf3, h_buf, y_buf, sems, idx_sem):
    bufs = (buf0, buf1, buf2, buf3)
    ahead = EXP_SLOTS - 1
    i = pl.program_id(0)
    idx_copy = pltpu.make_async_copy(idx_hbm.at[pl.ds(i * (EXP_TOKENS * PICKS), EXP_TOKENS * PICKS)],
                                     idx_smem, idx_sem)
    idx_copy.start()
    h_buf[...] = _rmsnorm(x_ref[...], g_ref[...])
    idx_copy.wait()

    def issue(t, slot):
        for k in range(PICKS):
            e = idx_smem[t * PICKS + k]
            pltpu.async_copy(tab_hbm.at[e], bufs[slot].at[pl.ds(k, 1), :], sems.at[slot], priority=k % 2)

    def wait_slot(slot):
        pltpu.make_async_copy(bufs[(slot + 1) % EXP_SLOTS], bufs[slot], sems.at[slot]).wait()

    lane = lax.broadcasted_iota(I32, (PICKS, EXP_TOKENS), 1)

    def compute(t, slot):
        h = h_buf[pl.ds(t, 1), :]
        u = lax.bitcast_convert_type(bufs[slot][...] & jnp.int32(-65536), F32)
        s = jnp.sum(u * h, axis=-1, keepdims=True)
        gate = jnp.sum(jnp.where(lane == t, gate_ref[...], 0.0), axis=-1, keepdims=True)
        w = gate * jax.nn.gelu(s)
        v = lax.bitcast_convert_type(bufs[slot][...] << 16, F32)
        y_buf[pl.ds(t, 1), :] = jnp.sum(w * v, axis=0, keepdims=True)

    def group(t0, issue_upto):
        for s in range(EXP_SLOTS):
            wait_slot(s)
            if s < issue_upto:
                issue(t0 + s + ahead, (s + ahead) % EXP_SLOTS)
            compute(t0 + s, s)

    for s in range(ahead):
        issue(s, s)

    def steady(q, carry):
        group(q * EXP_SLOTS, EXP_SLOTS)
        return carry

    n_groups = EXP_TOKENS // EXP_SLOTS
    lax.fori_loop(0, n_groups - 1, steady, 0)
    group((n_groups - 1) * EXP_SLOTS, EXP_SLOTS - ahead)
    o_ref[...] = _rmsnorm(x_ref[...] + y_buf[...], gf_ref[...])


def _experts(x, g, g_final, idx_flat, gate_t, table, first_token):
    n = x.shape[0]
    first_block = first_token // EXP_TOKENS
    full = lambda a: pl.BlockSpec(a.shape, lambda i: (0, 0))
    row = pl.BlockSpec((EXP_TOKENS, D_MODEL), lambda i: (i + first_block, 0))
    return pl.pallas_call(
        _expert_body,
        grid=(n // EXP_TOKENS - first_block,),
        in_specs=[row, full(g), full(g_final),
                  pl.BlockSpec(memory_space=pl.ANY),
                  pl.BlockSpec((PICKS, EXP_TOKENS), lambda i: (0, i)),
                  pl.BlockSpec(memory_space=pl.ANY)],
        out_specs=row,
        out_shape=jax.ShapeDtypeStruct((n, D_MODEL), F32),
        scratch_shapes=[pltpu.SMEM((EXP_TOKENS * PICKS,), I32)]
        + [pltpu.VMEM((PICKS, D_MODEL), I32) for _ in range(EXP_SLOTS)]
        + [pltpu.VMEM((EXP_TOKENS, D_MODEL), F32),
           pltpu.VMEM((EXP_TOKENS, D_MODEL), F32),
           pltpu.SemaphoreType.DMA((EXP_SLOTS,)),
           pltpu.SemaphoreType.DMA(())],
        compiler_params=pltpu.CompilerParams(dimension_semantics=("arbitrary",), vmem_limit_bytes=VMEM_LIMIT,
                                             disable_bounds_checks=True),
        name="experts",
    )(x, g, g_final, idx_flat, gate_t, table)


def _attention_masks(tq):
    i = jnp.arange(tq, dtype=I32)[:, None]
    r = jnp.arange(ATT_KEYS, dtype=I32)[None, :]
    negdist = -jnp.abs(WINDOW + i - r).astype(F32)
    return negdist, i, r


def _layer(x, n_groups, rows, swa_inputs, ret_state, ret_len, w, tm, n_sc):
    (norm_mix, w_in, b_gate, sinks, wa, wb, wo, norm_ffn, wq_t, k1, k2, packed, norm_final) = w
    assert n_sc < x.shape[0]
    qa, ka, va, qb, kb, vb, rg, ga, gb = _project(x, norm_mix, w_in, b_gate, tm)
    oa, k_rows, v_rows = swa_inputs(qa, ka, va, sinks)
    ob, s_fin = _retention(qb, kb, vb, rg, ret_state, n_groups, rows // ret_len, ret_len)
    x1 = _merge(x, oa, ob, ga, gb, wa, wb, wo, tm)
    if n_sc:
        idx_t, gate_t, h2 = _route(x1, norm_ffn, wq_t, k1, k2, tm, 0, n_sc)
        y_sc = _sc_experts(packed, idx_t.T, gate_t.T, h2, 0)
    idx_t, gate_t, _ = _route(x1, norm_ffn, wq_t, k1, k2, tm, n_sc, x.shape[0] - n_sc)
    y = _experts(x1, norm_ffn, norm_final, idx_t.T.reshape(-1), gate_t, packed, n_sc)
    if n_sc:
        y = _finish(x1, y_sc, norm_final, y, tm)
    return y, k_rows, v_rows, s_fin


def kernel(x_prompt, x_sample, cache_swa_k, cache_swa_v, state_ret, norm_mix, w_in, b_gate, attn_sinks,
           w_branch_a, w_branch_b, w_out, norm_ffn, peer_w_q, peer_sub_k1, peer_sub_k2, peer_u, peer_v,
           norm_final):
    batch, seq, _ = x_prompt.shape
    dec_batch, dec_seq, _ = x_sample.shape
    assert norm_mix.shape[0] == 1, "single-layer trunk"
    half = lambda a: lax.bitcast_convert_type(a.astype(BF16), jnp.uint16).astype(jnp.uint32)
    packed = lax.bitcast_convert_type((half(peer_u[0]) << 16) | half(peer_v[0]), I32)[:, None, :]
    table, x_prompt, x_sample = lax.optimization_barrier((packed, x_prompt, x_sample))
    w = (norm_mix[0][None], w_in[0].astype(BF16), b_gate[0][None], attn_sinks[0],
         w_branch_a[0].astype(BF16), w_branch_b[0].astype(BF16), w_out[0].astype(BF16),
         norm_ffn[0][None], peer_w_q[0].T.astype(BF16), peer_sub_k1[0].astype(BF16),
         peer_sub_k2[0].astype(BF16), table, norm_final[None])

    def swa_prompt(qa, ka, va, sinks):
        streams = qa.shape[0] // seq
        tq = 2 * CHUNK
        negdist, i, r = _attention_masks(tq)
        lq, lk = i // CHUNK, r // CHUNK
        band = (lk >= lq) & (lk <= lq + 2)
        allow = jnp.stack([band & (r >= WINDOW), band]).astype(F32)
        oa = _swa(sinks, qa, ka, va, negdist, allow, streams, seq // tq, tq)
        tail = lambda a: a.reshape(streams, seq, N_KV_A, HEAD_DIM_A)[:, seq - WINDOW:]
        return oa, tail(ka), tail(va)

    def swa_sample(qa, ka, va, sinks):
        negdist, i, r = _attention_masks(dec_seq)
        visible = jnp.broadcast_to(r < WINDOW + dec_seq, (dec_seq, ATT_KEYS))
        allow = jnp.stack([visible, visible]).astype(F32)
        pad = jnp.zeros((dec_batch, ATT_KEYS - WINDOW - dec_seq, KV_A), F32)
        k_all = jnp.concatenate([cache_swa_k[0].reshape(dec_batch, WINDOW, KV_A),
                                 ka.reshape(dec_batch, dec_seq, KV_A), pad], axis=1)
        v_all = jnp.concatenate([cache_swa_v[0].reshape(dec_batch, WINDOW, KV_A),
                                 va.reshape(dec_batch, dec_seq, KV_A), pad], axis=1)
        oa = _swa(sinks, qa, k_all.reshape(-1, KV_A), v_all.reshape(-1, KV_A), negdist, allow,
                  dec_batch, 1, dec_seq)
        tail = lambda a: a[:, dec_seq:WINDOW + dec_seq].reshape(dec_batch, WINDOW, N_KV_A, HEAD_DIM_A)
        return oa, tail(k_all), tail(v_all)

    ys, ks, vs, ss = _layer(x_sample.reshape(dec_batch * dec_seq, D_MODEL), dec_batch, dec_seq, swa_sample,
                            state_ret[0], dec_seq, w, 128, 0)
    s0 = jnp.zeros((batch, N_HEADS_B, DK_B, DV_B), F32)
    yp, kp, vp, sp = _layer(x_prompt.reshape(batch * seq, D_MODEL), batch, seq, swa_prompt, s0, 256, w, 256,
                            SC_PROMPT_ROWS)

    return (yp.reshape(batch, seq, D_MODEL), ys.reshape(dec_batch, dec_seq, D_MODEL),
            kp[None], vp[None], sp[None], ks[None], vs[None], ss[None])
```

```python
import functools

import jax
import jax.numpy as jnp
from jax import lax
from jax.experimental import pallas as pl
from jax.experimental.pallas import tpu as pltpu
from jax.experimental.pallas import tpu_sc as plsc

F32 = jnp.float32
BF16 = jnp.bfloat16
I32 = jnp.int32

D_MODEL = 1024
CHUNK = 64
EPS = 1e-6
NEG_INF = -1e30
PAST_LEN = 2048

N_HEADS_A = 8
N_KV_A = 2
GROUP_A = N_HEADS_A // N_KV_A
HEAD_DIM_A = 64
WINDOW = 128
N_HEADS_B = 4
DK_B = 128
DV_B = 256
Q_A = N_HEADS_A * HEAD_DIM_A
KV_A = N_KV_A * HEAD_DIM_A
Q_B = N_HEADS_B * DK_B
V_B = N_HEADS_B * DV_B
D_IN = Q_A + 2 * KV_A + 2 * Q_B + 2 * V_B + 2 * D_MODEL
OFF_QA = 0
OFF_KA = OFF_QA + Q_A
OFF_VA = OFF_KA + KV_A
OFF_QB = OFF_VA + KV_A
OFF_KB = OFF_QB + Q_B
OFF_VB = OFF_KB + Q_B
OFF_RG = OFF_VB + V_B
OFF_GA = OFF_RG + V_B
OFF_GB = OFF_GA + D_MODEL

N_KEYS = 128
N_EXPERTS = N_KEYS * N_KEYS
PEER_HEADS = 8
PEER_QDIM = 256
PEER_HALF = PEER_QDIM // 2
PEER_TOPK = 16
PICKS = PEER_HEADS * PEER_TOPK

LANES = 128
ATT_KEYS = 2 * WINDOW
VMEM_LIMIT = 56 * 1024 * 1024


def _rmsnorm(x, g):
    return x * lax.rsqrt(jnp.mean(x * x, axis=-1, keepdims=True) + EPS) * g


def _dot(a, b):
    return jnp.dot(a, b, preferred_element_type=F32)


def _dot_nt(a, b):
    return lax.dot_general(a, b, (((1,), (1,)), ((), ())), preferred_element_type=F32)


def _dot_tn(a, b):
    return lax.dot_general(a, b, (((0,), (0,)), ((), ())), preferred_element_type=F32)


def _params(sem):
    return pltpu.CompilerParams(dimension_semantics=sem, vmem_limit_bytes=VMEM_LIMIT)


def _proj_body(x_ref, g_ref, w_ref, bg_ref, qa_ref, ka_ref, va_ref, qb_ref, kb_ref, vb_ref,
               rg_ref, ga_ref, gb_ref):
    h = _rmsnorm(x_ref[...], g_ref[...]).astype(BF16)

    def mm(lo, width):
        return _dot(h, w_ref[:, lo:lo + width])

    qa_ref[...] = (mm(OFF_QA, Q_A) * (HEAD_DIM_A ** -0.5)).astype(BF16)
    ka_ref[...] = mm(OFF_KA, KV_A)
    va_ref[...] = mm(OFF_VA, KV_A)
    qb_ref[...] = mm(OFF_QB, Q_B).astype(BF16)
    kb_ref[...] = (mm(OFF_KB, Q_B) * (DK_B ** -0.5)).astype(BF16)
    vb_ref[...] = mm(OFF_VB, V_B).astype(BF16)
    rg_ref[...] = mm(OFF_RG, V_B)
    ga_ref[...] = mm(OFF_GA, D_MODEL) + bg_ref[:, :D_MODEL]
    gb_ref[...] = mm(OFF_GB, D_MODEL) + bg_ref[:, D_MODEL:]


def _project(x, g, w_bf16, b_gate, tm):
    n = x.shape[0]
    row = lambda width: pl.BlockSpec((tm, width), lambda i: (i, 0))
    full = lambda a: pl.BlockSpec(a.shape, lambda i: (0, 0))
    widths = (Q_A, KV_A, KV_A, Q_B, Q_B, V_B, V_B, D_MODEL, D_MODEL)
    dtypes = (BF16, F32, F32, BF16, BF16, BF16, F32, F32, F32)
    return pl.pallas_call(
        _proj_body,
        grid=(n // tm,),
        in_specs=[row(D_MODEL), full(g), full(w_bf16), full(b_gate)],
        out_specs=[row(w) for w in widths],
        out_shape=[jax.ShapeDtypeStruct((n, w), d) for w, d in zip(widths, dtypes)],
        compiler_params=_params(("parallel",)),
        name="projection",
    )(x, g, w_bf16, b_gate)


def _swa_body(sink_ref, q_ref, kp_ref, kc_ref, vp_ref, vc_ref, nd_ref, al_ref, o_ref):
    k = jnp.concatenate([kp_ref[...], kc_ref[...]], axis=0)
    v = jnp.concatenate([vp_ref[...], vc_ref[...]], axis=0)
    lane = lax.broadcasted_iota(I32, k.shape, 1)
    low = lane < HEAD_DIM_A
    k_sw = pltpu.roll(k, HEAD_DIM_A, axis=1)
    v_sw = pltpu.roll(v, HEAD_DIM_A, axis=1)
    negdist = nd_ref[...]
    allowed = al_ref[0] > 0.5
    for kv in range(N_KV_A):
        own = low if kv == 0 else jnp.logical_not(low)
        k_rep = jnp.where(own, k, k_sw)
        v_rep = jnp.where(own, v, v_sw)
        zero = jnp.zeros_like(k_rep)
        kk = jnp.concatenate([jnp.where(low, k_rep, zero), jnp.where(low, zero, k_rep)], axis=0).astype(BF16)
        vv = jnp.concatenate([jnp.where(low, v_rep, zero), jnp.where(low, zero, v_rep)], axis=0).astype(BF16)
        for a in range(GROUP_A // 2):
            col = kv * GROUP_A * HEAD_DIM_A + a * LANES
            s2 = _dot_nt(q_ref[:, col:col + LANES], kk)
            ps = []
            for u in range(2):
                head = kv * GROUP_A + 2 * a + u
                slope = 2.0 ** (-8.0 * (head + 1) / N_HEADS_A)
                sink = sink_ref[head]
                s = s2[:, u * ATT_KEYS:(u + 1) * ATT_KEYS] + slope * negdist
                s = jnp.where(allowed, s, NEG_INF)
                m = jnp.maximum(jnp.max(s, axis=-1, keepdims=True), sink)
                p = jnp.exp(s - m)
                denom = jnp.sum(p, axis=-1, keepdims=True) + jnp.exp(sink - m)
                ps.append((p / denom).astype(BF16))
            o = _dot(jnp.concatenate(ps, axis=1), vv)
            o_ref[:, col:col + LANES] = o.astype(BF16)


def _swa(sinks, q, k, v, negdist, allow, groups, q_blocks, tq):
    n = q.shape[0]
    kblocks = k.shape[0] // LANES // groups
    shift = kblocks - q_blocks
    prev = lambda b, j: (b * kblocks + jnp.maximum(j + shift - 1, 0), 0)
    cur = lambda b, j: (b * kblocks + j + shift, 0)
    kv_spec = lambda f: pl.BlockSpec((LANES, KV_A), f)
    return pl.pallas_call(
        _swa_body,
        grid=(groups, q_blocks),
        in_specs=[pl.BlockSpec(memory_space=pltpu.SMEM),
                  pl.BlockSpec((tq, Q_A), lambda b, j: (b * q_blocks + j, 0)),
                  kv_spec(prev), kv_spec(cur), kv_spec(prev), kv_spec(cur),
                  pl.BlockSpec((tq, ATT_KEYS), lambda b, j: (0, 0)),
                  pl.BlockSpec((1, tq, ATT_KEYS), lambda b, j: (jnp.minimum(j + shift, 1), 0, 0))],
        out_specs=pl.BlockSpec((tq, Q_A), lambda b, j: (b * q_blocks + j, 0)),
        out_shape=jax.ShapeDtypeStruct((n, Q_A), BF16),
        compiler_params=_params(("parallel", "parallel")),
        name="attention",
    )(sinks, q, k, k, v, v, negdist, allow)


def _ret_body(q_ref, k_ref, v_ref, rg_ref, s0_ref, intra_ref, rdec_ref, kdec_ref, sdec_ref,
              o_ref, st_ref):
    c = pl.program_id(1)

    @pl.when(c == 0)
    def _():
        st_ref[...] = s0_ref[...]

    for h in range(N_HEADS_B):
        q = q_ref[:, h * DK_B:(h + 1) * DK_B]
        k = k_ref[:, h * DK_B:(h + 1) * DK_B]
        v = v_ref[:, h * DV_B:(h + 1) * DV_B]
        st = st_ref[0, h]
        scores = (_dot_nt(q, k) * intra_ref[h]).astype(BF16)
        o = _dot(scores, v) + _dot(q, st.astype(BF16)) * rdec_ref[h]
        k_dec = (k.astype(F32) * kdec_ref[h]).astype(BF16)
        st_ref[0, h] = sdec_ref[h] * st + _dot_tn(k_dec, v)
        mu = jnp.mean(o, axis=-1, keepdims=True)
        d = o - mu
        var = jnp.mean(d * d, axis=-1, keepdims=True)
        on = d * lax.rsqrt(var + EPS)
        r = rg_ref[:, h * DV_B:(h + 1) * DV_B]
        o_ref[:, h * DV_B:(h + 1) * DV_B] = (on * (r * jax.nn.sigmoid(r))).astype(BF16)


def _retention_consts(length):
    lg = jnp.log1p(-(2.0 ** (-5.0 - jnp.arange(N_HEADS_B, dtype=F32))))
    n = jnp.arange(length, dtype=F32)
    diff = n[:, None] - n[None, :]
    intra = jnp.where(diff >= 0, jnp.exp(jnp.maximum(diff, 0.0) * lg[:, None, None]), 0.0)
    rdec = jnp.exp((n[None, :] + 1.0) * lg[:, None])[..., None]
    kdec = jnp.exp((length - 1.0 - n)[None, :] * lg[:, None])[..., None]
    sdec = jnp.exp(length * lg)
    return intra, rdec, kdec, sdec


def _retention(q, k, v, rg, s0, streams, chunks, length):
    n = q.shape[0]
    intra, rdec, kdec, sdec = _retention_consts(length)
    row = lambda width: pl.BlockSpec((length, width), lambda b, c: (b * chunks + c, 0))
    const = lambda a: pl.BlockSpec(a.shape, lambda b, c: (0,) * a.ndim)
    st_spec = pl.BlockSpec((1, N_HEADS_B, DK_B, DV_B), lambda b, c: (b, 0, 0, 0))
    return pl.pallas_call(
        _ret_body,
        grid=(streams, chunks),
        in_specs=[row(Q_B), row(Q_B), row(V_B), row(V_B), st_spec,
                  const(intra), const(rdec), const(kdec), pl.BlockSpec(memory_space=pltpu.SMEM)],
        out_specs=[row(V_B), st_spec],
        out_shape=[jax.ShapeDtypeStruct((n, V_B), BF16),
                   jax.ShapeDtypeStruct((streams, N_HEADS_B, DK_B, DV_B), F32)],
        compiler_params=_params(("parallel", "arbitrary")),
        name="retention",
    )(q, k, v, rg, s0, intra, rdec, kdec, sdec)


def _merge_body(x_ref, oa_ref, ob_ref, ga_ref, gb_ref, wa_ref, wb_ref, wo_ref, o_ref):
    ya = _dot(oa_ref[...], wa_ref[...])
    yb = _dot(ob_ref[...], wb_ref[...])
    mix = jax.nn.sigmoid(ga_ref[...]) * ya + jax.nn.sigmoid(gb_ref[...]) * yb
    o_ref[...] = x_ref[...] + _dot(mix.astype(BF16), wo_ref[...])


def _merge(x, oa, ob, ga, gb, wa, wb, wo, tm):
    n = x.shape[0]
    row = lambda width: pl.BlockSpec((tm, width), lambda i: (i, 0))
    full = lambda a: pl.BlockSpec(a.shape, lambda i: (0, 0))
    return pl.pallas_call(
        _merge_body,
        grid=(n // tm,),
        in_specs=[row(D_MODEL), row(Q_A), row(V_B), row(D_MODEL), row(D_MODEL), full(wa), full(wb), full(wo)],
        out_specs=row(D_MODEL),
        out_shape=jax.ShapeDtypeStruct((n, D_MODEL), F32),
        compiler_params=_params(("parallel",)),
        name="merge",
    )(x, oa, ob, ga, gb, wa, wb, wo)


def _oddeven_merge(lo, hi, r):
    step = r * 2
    if step < hi - lo:
        yield from _oddeven_merge(lo, hi, step)
        yield from _oddeven_merge(lo + r, hi, step)
        yield from ((i, i + r) for i in range(lo + r, hi - r, step))
    else:
        yield (lo, lo + r)


def _oddeven_sort(lo, hi):
    if hi - lo >= 1:
        mid = lo + (hi - lo) // 2
        yield from _oddeven_sort(lo, mid)
        yield from _oddeven_sort(mid + 1, hi)
        yield from _oddeven_merge(lo, hi, 1)


SUBLANES = 8
_SORT_PAIRS = tuple(_oddeven_sort(0, N_KEYS // SUBLANES - 1))


def _top16_keys(s):
    t = s.shape[1]
    n_lists = N_KEYS // SUBLANES
    sub = lax.broadcasted_iota(I32, (SUBLANES, t), 0).astype(F32)
    vals = [s[SUBLANES * i:SUBLANES * (i + 1)] for i in range(n_lists)]
    ids = [sub + float(SUBLANES * i) for i in range(n_lists)]
    for i, j in _SORT_PAIRS:
        a, b, ia, ib = vals[i], vals[j], ids[i], ids[j]
        swap = (b > a) | ((b == a) & (ib < ia))
        vals[i], vals[j] = jnp.where(swap, b, a), jnp.where(swap, a, b)
        ids[i], ids[j] = jnp.where(swap, ib, ia), jnp.where(swap, ia, ib)
    rank = lax.broadcasted_iota(I32, (PEER_TOPK, t), 0)
    out_v = jnp.zeros((PEER_TOPK, t), F32)
    out_i = jnp.zeros((PEER_TOPK, t), F32)
    for r in range(PEER_TOPK):
        m = jnp.max(vals[0], axis=0, keepdims=True)
        pick = jnp.min(jnp.where(vals[0] == m, ids[0], 3e38), axis=0, keepdims=True)
        hit = ids[0] == pick
        out_v = jnp.where(rank == r, m, out_v)
        out_i = jnp.where(rank == r, pick, out_i)
        for d in range(PEER_TOPK - 1 - r):
            vals[d] = jnp.where(hit, vals[d + 1], vals[d])
            ids[d] = jnp.where(hit, ids[d + 1], ids[d])
    return out_v, out_i


def _top16_pairs(v1, i1, v2, i2):
    t = v1.shape[1]
    sub = lax.broadcasted_iota(I32, (SUBLANES, t), 0).astype(F32)
    flat_lo = sub * PEER_TOPK
    flat_hi = (sub + SUBLANES) * PEER_TOPK
    lo_v = [v1[0:SUBLANES] + v2[b:b + 1] for b in range(PEER_TOPK)]
    lo_e = [i1[0:SUBLANES] * N_KEYS + i2[b:b + 1] for b in range(PEER_TOPK)]
    hi_v = v1[SUBLANES:] + v2[0:1]
    hi_e = i1[SUBLANES:] * N_KEYS + i2[0:1]
    taken = jnp.zeros((SUBLANES, t), F32)
    rank = lax.broadcasted_iota(I32, (PEER_TOPK, t), 0)
    best = jnp.zeros((PEER_TOPK, t), F32)
    eidx = jnp.zeros((PEER_TOPK, t), F32)
    red = lambda f, x: f(x, axis=0, keepdims=True)
    for r in range(PEER_TOPK):
        m = jnp.maximum(red(jnp.max, lo_v[0]), red(jnp.max, hi_v))
        f_lo = flat_lo + taken
        pick = jnp.minimum(red(jnp.min, jnp.where(lo_v[0] == m, f_lo, 3e38)),
                           red(jnp.min, jnp.where(hi_v == m, flat_hi, 3e38)))
        hit_lo = f_lo == pick
        hit_hi = flat_hi == pick
        e = jnp.maximum(red(jnp.max, jnp.where(hit_lo, lo_e[0], -1.0)),
                        red(jnp.max, jnp.where(hit_hi, hi_e, -1.0)))
        best = jnp.where(rank == r, m, best)
        eidx = jnp.where(rank == r, e, eidx)
        taken = taken + jnp.where(hit_lo, 1.0, 0.0)
        hi_v = jnp.where(hit_hi, -jnp.inf, hi_v)
        for d in range(PEER_TOPK - 1 - r):
            lo_v[d] = jnp.where(hit_lo, lo_v[d + 1], lo_v[d])
            lo_e[d] = jnp.where(hit_lo, lo_e[d + 1], lo_e[d])
    return best, eidx


def _route_body(x_ref, g_ref, wq_ref, k1_ref, k2_ref, idx_ref, gate_ref, h_ref):
    h_ref[...] = _rmsnorm(x_ref[...], g_ref[...])
    h = h_ref[...].astype(BF16)
    q_t = _dot_nt(wq_ref[...], h).astype(BF16)
    idx_rows, gate_rows = [], []
    for hd in range(PEER_HEADS):
        base = hd * PEER_QDIM
        s1 = _dot(k1_ref[...], q_t[base:base + PEER_HALF])
        s2 = _dot(k2_ref[...], q_t[base + PEER_HALF:base + PEER_QDIM])
        v1, i1 = _top16_keys(s1)
        v2, i2 = _top16_keys(s2)
        best, eidx = _top16_pairs(v1, i1, v2, i2)
        e = jnp.exp(best - best[0:1])
        gate_rows.append(e / jnp.sum(e, axis=0, keepdims=True))
        idx_rows.append(eidx.astype(I32))
    idx_ref[...] = jnp.concatenate(idx_rows, axis=0)
    gate_ref[...] = jnp.concatenate(gate_rows, axis=0)


def _route(x, g, wq_t, k1, k2, tm, first_row, n):
    first_block = first_row // tm
    full = lambda a: pl.BlockSpec(a.shape, lambda i: (0, 0))
    col = pl.BlockSpec((PICKS, tm), lambda i: (0, i))
    return pl.pallas_call(
        _route_body,
        grid=(n // tm,),
        in_specs=[pl.BlockSpec((tm, D_MODEL), lambda i: (i + first_block, 0)), full(g), full(wq_t), full(k1),
                  full(k2)],
        out_specs=[col, col, pl.BlockSpec((tm, D_MODEL), lambda i: (i, 0))],
        out_shape=[jax.ShapeDtypeStruct((PICKS, n), I32), jax.ShapeDtypeStruct((PICKS, n), F32),
                   jax.ShapeDtypeStruct((n, D_MODEL), F32)],
        compiler_params=_params(("parallel",)),
        name="routing",
    )(x, g, wq_t, k1, k2)


SC_CORES = 2
SC_SUBCORES = 16
SC_LANES = 16
SC_CHUNK = 16
SC_PROMPT_ROWS = 18432


def _gelu_via_exp(x):
    z = 0.7978845608028654 * (x + 0.044715 * x * x * x)
    return 0.5 * x * (2.0 - 2.0 / (1.0 + jnp.exp(2.0 * z)))


def _sc_experts(table, idx, gate, h, first):
    n = idx.shape[0]
    workers = SC_CORES * SC_SUBCORES
    per_worker = n // workers
    n_chunks = PICKS // SC_CHUNK
    steps = D_MODEL // SC_LANES
    mesh = plsc.VectorSubcoreMesh(core_axis_name="c", subcore_axis_name="s",
                                  num_cores=SC_CORES, num_subcores=SC_SUBCORES)

    def body(tab_hbm, idx_hbm, gate_hbm, h_hbm, y_hbm,
             idx_0, idx_1, gate_0, gate_1, h_0, h_1, y_0, y_1, rows_a, rows_b,
             in_sem_0, in_sem_1, y_sem_0, y_sem_1, sem_a, sem_b):
        base = (lax.axis_index("s") * SC_CORES + lax.axis_index("c")) * per_worker
        last = base + per_worker - 1
        sets = ((idx_0, gate_0, h_0, y_0, in_sem_0, y_sem_0), (idx_1, gate_1, h_1, y_1, in_sem_1, y_sem_1))
        bufs = ((rows_a, sem_a), (rows_b, sem_b))

        def in_copies(t, q):
            idx_v, gate_v, h_v, _, sem, _ = sets[q]
            return (pltpu.make_async_copy(idx_hbm.at[t], idx_v, sem),
                    pltpu.make_async_copy(gate_hbm.at[t], gate_v, sem),
                    pltpu.make_async_copy(h_hbm.at[t + first], h_v, sem))

        def y_copy(t, q):
            return pltpu.make_async_copy(sets[q][3], y_hbm.at[t], sets[q][5])

        def gather(c, b, q):
            rows, sem = bufs[b]
            picks = sets[q][0].at[pl.ds(c * SC_CHUNK, SC_CHUNK)]
            return pltpu.make_async_copy(tab_hbm.at[picks], rows, sem)

        def compute(c, b, q):
            rows, _ = bufs[b]
            _, gate_v, h_v, y_v, _, _ = sets[q]
            u_of = lambda word: lax.bitcast_convert_type(word & jnp.int32(-65536), F32)
            v_of = lambda word: lax.bitcast_convert_type(word << 16, F32)

            def dot_step(j, accs):
                cols = pl.ds(j * SC_LANES, SC_LANES)
                hj = h_v[cols]
                return tuple(accs[k] + u_of(rows[k, 0, cols]) * hj for k in range(SC_CHUNK))

            accs = plsc.parallel_loop(0, steps, carry=tuple(jnp.zeros((SC_LANES,), F32) for _ in range(SC_CHUNK)))(
                dot_step)
            lane = lax.iota(I32, SC_LANES)
            s = jnp.zeros((SC_LANES,), F32)
            for k in range(SC_CHUNK):
                s = jnp.where(lane == k, jnp.sum(accs[k]), s)
            w = _gelu_via_exp(s) * gate_v[pl.ds(c * SC_CHUNK, SC_CHUNK)]
            ws = [jnp.full((SC_LANES,), jnp.sum(jnp.where(lane == k, w, 0.0)), F32) for k in range(SC_CHUNK)]

            @plsc.parallel_loop(0, steps)
            def _(j):
                cols = pl.ds(j * SC_LANES, SC_LANES)
                terms = [ws[k] * v_of(rows[k, 0, cols]) for k in range(SC_CHUNK)]
                while len(terms) > 1:
                    terms = [terms[i] + terms[i + 1] for i in range(0, len(terms), 2)]
                y_v[cols] = y_v[cols] + terms[0]

        def token(t, q, first):
            nxt = jnp.minimum(t + 1, last)
            for cp in in_copies(nxt, 1 - q):
                cp.start()
            y_v = sets[q][3]

            @pl.when(jnp.logical_not(first))
            def _():
                y_copy(t, q).wait()

            @pl.loop(0, steps)
            def _(j):
                y_v[pl.ds(j * SC_LANES, SC_LANES)] = jnp.zeros((SC_LANES,), F32)

            @pl.loop(0, n_chunks // 2)
            def _(p):
                c0 = p * 2
                gather(c0 + 1, 1, q).start()
                gather(c0, 0, q).wait()
                compute(c0, 0, q)

                @pl.when(p + 1 < n_chunks // 2)
                def _():
                    gather(c0 + 2, 0, q).start()

                gather(c0 + 1, 1, q).wait()
                compute(c0 + 1, 1, q)

            for cp in in_copies(nxt, 1 - q):
                cp.wait()
            gather(0, 0, 1 - q).start()
            y_copy(t, q).start()

        for cp in in_copies(base, 0):
            cp.start()
        for cp in in_copies(base, 0):
            cp.wait()
        gather(0, 0, 0).start()

        @pl.loop(0, per_worker // 2)
        def _(i):
            token(base + 2 * i, 0, i == 0)
            token(base + 2 * i + 1, 1, i == 0)

        gather(0, 0, 0).wait()
        y_copy(last - 1, 0).wait()
        y_copy(last, 1).wait()

    vec = lambda n_el, dt: [pltpu.VMEM((n_el,), dt), pltpu.VMEM((n_el,), dt)]
    return pl.kernel(
        body, out_type=jax.ShapeDtypeStruct((n, D_MODEL), F32), mesh=mesh,
        scratch_types=vec(PICKS, I32) + vec(PICKS, F32) + vec(D_MODEL, F32) + vec(D_MODEL, F32)
        + [pltpu.VMEM((SC_CHUNK, 1, D_MODEL), I32) for _ in range(2)]
        + [pltpu.SemaphoreType.DMA] * 6,
        compiler_params=pltpu.CompilerParams(needs_layout_passes=False),
        name="experts_sc",
    )(table, idx, gate, h)


def _finish_body(x_ref, y_ref, gf_ref, *rest_and_out):
    rest_and_out[-1][...] = _rmsnorm(x_ref[...] + y_ref[...], gf_ref[...])


def _finish(x, y, g_final, rest, tm, first_row=0):
    first_block = first_row // tm
    row = pl.BlockSpec((tm, D_MODEL), lambda i: (i + first_block, 0))
    extra = () if rest is None else (rest,)
    return pl.pallas_call(
        _finish_body,
        grid=(y.shape[0] // tm,),
        in_specs=[row, pl.BlockSpec((tm, D_MODEL), lambda i: (i, 0)), pl.BlockSpec(g_final.shape, lambda i: (0, 0))]
        + [pl.BlockSpec(memory_space=pl.ANY) for _ in extra],
        out_specs=row,
        out_shape=jax.ShapeDtypeStruct(x.shape if rest is not None else y.shape, F32),
        input_output_aliases={3: 0} if rest is not None else {},
        compiler_params=_params(("parallel",)),
        name="finish",
    )(x, y, g_final, *extra)


EXP_TOKENS = 256
EXP_SLOTS = 4


def _expert_body(x_ref, g_ref, gf_ref, idx_hbm, gate_ref, tab_hbm, o_ref,
                 idx_smem, buf0, buf1, buf2, buf3, h_buf, y_buf, sems, idx_sem):
    bufs = (buf0, buf1, buf2, buf3)
    ahead = EXP_SLOTS - 1
    i = pl.program_id(0)
    idx_copy = pltpu.make_async_copy(idx_hbm.at[pl.ds(i * (EXP_TOKENS * PICKS), EXP_TOKENS * PICKS)],
                                     idx_smem, idx_sem)
    idx_copy.start()
    h_buf[...] = _rmsnorm(x_ref[...], g_ref[...])
    idx_copy.wait()

    def issue(t, slot):
        for k in range(PICKS):
            e = idx_smem[t * PICKS + k]
            pltpu.async_copy(tab_hbm.at[e], bufs[slot].at[pl.ds(k, 1), :], sems.at[slot], priority=k % 2)

    def wait_slot(slot):
        pltpu.make_async_copy(bufs[(slot + 1) % EXP_SLOTS], bufs[slot], sems.at[slot]).wait()

    lane = lax.broadcasted_iota(I32, (PICKS, EXP_TOKENS), 1)

    def compute(t, slot):
        h = h_buf[pl.ds(t, 1), :]
        u = lax.bitcast_convert_type(bufs[slot][...] & jnp.int32(-65536), F32)
        s = jnp.sum(u * h, axis=-1, keepdims=True)
        gate = jnp.sum(jnp.where(lane == t, gate_ref[...], 0.0), axis=-1, keepdims=True)
        w = gate * jax.nn.gelu(s)
        v = lax.bitcast_convert_type(bufs[slot][...] << 16, F32)
        y_buf[pl.ds(t, 1), :] = jnp.sum(w * v, axis=0, keepdims=True)

    def group(t0, issue_upto):
        for s in range(EXP_SLOTS):
            wait_slot(s)
            if s < issue_upto:
                issue(t0 + s + ahead, (s + ahead) % EXP_SLOTS)
            compute(t0 + s, s)

    for s in range(ahead):
        issue(s, s)

    def steady(q, carry):
        group(q * EXP_SLOTS, EXP_SLOTS)
        return carry

    n_groups = EXP_TOKENS // EXP_SLOTS
    lax.fori_loop(0, n_groups - 1, steady, 0)
    group((n_groups - 1) * EXP_SLOTS, EXP_SLOTS - ahead)
    o_ref[...] = _rmsnorm(x_ref[...] + y_buf[...], gf_ref[...])


def _experts(x, g, g_final, idx_flat, gate_t, table, first_token):
    n = x.shape[0]
    first_block = first_token // EXP_TOKENS
    full = lambda a: pl.BlockSpec(a.shape, lambda i: (0, 0))
    row = pl.BlockSpec((EXP_TOKENS, D_MODEL), lambda i: (i + first_block, 0))
    return pl.pallas_call(
        _expert_body,
        grid=(n // EXP_TOKENS - first_block,),
        in_specs=[row, full(g), full(g_final),
                  pl.BlockSpec(memory_space=pl.ANY),
                  pl.BlockSpec((PICKS, EXP_TOKENS), lambda i: (0, i)),
                  pl.BlockSpec(memory_space=pl.ANY)],
        out_specs=row,
        out_shape=jax.ShapeDtypeStruct((n, D_MODEL), F32),
        scratch_shapes=[pltpu.SMEM((EXP_TOKENS * PICKS,), I32)]
        + [pltpu.VMEM((PICKS, D_MODEL), I32) for _ in range(EXP_SLOTS)]
        + [pltpu.VMEM((EXP_TOKENS, D_MODEL), F32),
           pltpu.VMEM((EXP_TOKENS, D_MODEL), F32),
           pltpu.SemaphoreType.DMA((EXP_SLOTS,)),
           pltpu.SemaphoreType.DMA(())],
        compiler_params=pltpu.CompilerParams(dimension_semantics=("arbitrary",), vmem_limit_bytes=VMEM_LIMIT,
                                             disable_bounds_checks=True),
        name="experts",
    )(x, g, g_final, idx_flat, gate_t, table)


def _attention_masks(tq):
    i = jnp.arange(tq, dtype=I32)[:, None]
    r = jnp.arange(ATT_KEYS, dtype=I32)[None, :]
    negdist = -jnp.abs(WINDOW + i - r).astype(F32)
    return negdist, i, r


def _layer(x, n_groups, rows, swa_inputs, ret_state, ret_len, w, tm, n_sc):
    (norm_mix, w_in, b_gate, sinks, wa, wb, wo, norm_ffn, wq_t, k1, k2, packed, norm_final) = w
    assert n_sc < x.shape[0]
    qa, ka, va, qb, kb, vb, rg, ga, gb = _project(x, norm_mix, w_in, b_gate, tm)
    oa, k_rows, v_rows = swa_inputs(qa, ka, va, sinks)
    ob, s_fin = _retention(qb, kb, vb, rg, ret_state, n_groups, rows // ret_len, ret_len)
    x1 = _merge(x, oa, ob, ga, gb, wa, wb, wo, tm)
    if n_sc:
        idx_t, gate_t, h2 = _route(x1, norm_ffn, wq_t, k1, k2, tm, 0, n_sc)
        y_sc = _sc_experts(packed, idx_t.T, gate_t.T, h2, 0)
    idx_t, gate_t, _ = _route(x1, norm_ffn, wq_t, k1, k2, tm, n_sc, x.shape[0] - n_sc)
    y = _experts(x1, norm_ffn, norm_final, idx_t.T.reshape(-1), gate_t, packed, n_sc)
    if n_sc:
        y = _finish(x1, y_sc, norm_final, y, tm)
    return y, k_rows, v_rows, s_fin


def kernel(x_prompt, x_sample, cache_swa_k, cache_swa_v, state_ret, norm_mix, w_in, b_gate, attn_sinks,
           w_branch_a, w_branch_b, w_out, norm_ffn, peer_w_q, peer_sub_k1, peer_sub_k2, peer_u, peer_v,
           norm_final):
    batch, seq, _ = x_prompt.shape
    dec_batch, dec_seq, _ = x_sample.shape
    assert norm_mix.shape[0] == 1, "single-layer trunk"
    half = lambda a: lax.bitcast_convert_type(a.astype(BF16), jnp.uint16).astype(jnp.uint32)
    packed = lax.bitcast_convert_type((half(peer_u[0]) << 16) | half(peer_v[0]), I32)[:, None, :]
    table, x_prompt, x_sample = lax.optimization_barrier((packed, x_prompt, x_sample))
    w = (norm_mix[0][None], w_in[0].astype(BF16), b_gate[0][None], attn_sinks[0],
         w_branch_a[0].astype(BF16), w_branch_b[0].astype(BF16), w_out[0].astype(BF16),
         norm_ffn[0][None], peer_w_q[0].T.astype(BF16), peer_sub_k1[0].astype(BF16),
         peer_sub_k2[0].astype(BF16), table, norm_final[None])

    def swa_prompt(qa, ka, va, sinks):
        streams = qa.shape[0] // seq
        tq = 2 * CHUNK
        negdist, i, r = _attention_masks(tq)
        lq, lk = i // CHUNK, r // CHUNK
        band = (lk >= lq) & (lk <= lq + 2)
        allow = jnp.stack([band & (r >= WINDOW), band]).astype(F32)
        oa = _swa(sinks, qa, ka, va, negdist, allow, streams, seq // tq, tq)
        tail = lambda a: a.reshape(streams, seq, N_KV_A, HEAD_DIM_A)[:, seq - WINDOW:]
        return oa, tail(ka), tail(va)

    def swa_sample(qa, ka, va, sinks):
        negdist, i, r = _attention_masks(dec_seq)
        visible = jnp.broadcast_to(r < WINDOW + dec_seq, (dec_seq, ATT_KEYS))
        allow = jnp.stack([visible, visible]).astype(F32)
        pad = jnp.zeros((dec_batch, ATT_KEYS - WINDOW - dec_seq, KV_A), F32)
        k_all = jnp.concatenate([cache_swa_k[0].reshape(dec_batch, WINDOW, KV_A),
                                 ka.reshape(dec_batch, dec_seq, KV_A), pad], axis=1)
        v_all = jnp.concatenate([cache_swa_v[0].reshape(dec_batch, WINDOW, KV_A),
                                 va.reshape(dec_batch, dec_seq, KV_A), pad], axis=1)
        oa = _swa(sinks, qa, k_all.reshape(-1, KV_A), v_all.reshape(-1, KV_A), negdist, allow,
                  dec_batch, 1, dec_seq)
        tail = lambda a: a[:, dec_seq:WINDOW + dec_seq].reshape(dec_batch, WINDOW, N_KV_A, HEAD_DIM_A)
        return oa, tail(k_all), tail(v_all)

    ys, ks, vs, ss = _layer(x_sample.reshape(dec_batch * dec_seq, D_MODEL), dec_batch, dec_seq, swa_sample,
                            state_ret[0], dec_seq, w, 128, 0)
    s0 = jnp.zeros((batch, N_HEADS_B, DK_B, DV_B), F32)
    yp, kp, vp, sp = _layer(x_prompt.reshape(batch * seq, D_MODEL), batch, seq, swa_prompt, s0, 256, w, 256,
                            SC_PROMPT_ROWS)

    return (yp.reshape(batch, seq, D_MODEL), ys.reshape(dec_batch, dec_seq, D_MODEL),
            kp[None], vp[None], sp[None], ks[None], vs[None], ss[None])
```

```python
import jax
import jax.numpy as jnp
from jax import lax
from jax.experimental import pallas as pl
from jax.experimental.pallas import tpu as pltpu
from jax.experimental.pallas import tpu_sc as plsc

F32 = jnp.float32
BF16 = jnp.bfloat16
I32 = jnp.int32

D_MODEL = 1024
CHUNK = 64
EPS = 1e-6
NEG_INF = -1e30
PAST_LEN = 2048

N_HEADS_A = 8
N_KV_A = 2
GROUP_A = N_HEADS_A // N_KV_A
HEAD_DIM_A = 64
WINDOW = 128
N_HEADS_B = 4
DK_B = 128
DV_B = 256
Q_A = N_HEADS_A * HEAD_DIM_A
KV_A = N_KV_A * HEAD_DIM_A
Q_B = N_HEADS_B * DK_B
V_B = N_HEADS_B * DV_B
D_IN = Q_A + 2 * KV_A + 2 * Q_B + 2 * V_B + 2 * D_MODEL
OFF_QA = 0
OFF_KA = OFF_QA + Q_A
OFF_VA = OFF_KA + KV_A
OFF_QB = OFF_VA + KV_A
OFF_KB = OFF_QB + Q_B
OFF_VB = OFF_KB + Q_B
OFF_RG = OFF_VB + V_B
OFF_GA = OFF_RG + V_B
OFF_GB = OFF_GA + D_MODEL

N_KEYS = 128
N_EXPERTS = N_KEYS * N_KEYS
PEER_HEADS = 8
PEER_QDIM = 256
PEER_HALF = PEER_QDIM // 2
PEER_TOPK = 16
PICKS = PEER_HEADS * PEER_TOPK

LANES = 128
ATT_KEYS = 2 * WINDOW
VMEM_LIMIT = 56 * 1024 * 1024


def _rmsnorm(x, g):
    return x * lax.rsqrt(jnp.mean(x * x, axis=-1, keepdims=True) + EPS) * g


def _dot(a, b):
    return jnp.dot(a, b, preferred_element_type=F32)


def _dot_nt(a, b):
    return lax.dot_general(a, b, (((1,), (1,)), ((), ())), preferred_element_type=F32)


def _dot_tn(a, b):
    return lax.dot_general(a, b, (((0,), (0,)), ((), ())), preferred_element_type=F32)


def _params(sem):
    return pltpu.CompilerParams(dimension_semantics=sem, vmem_limit_bytes=VMEM_LIMIT)


def _proj_body(x_ref, g_ref, w_ref, bg_ref, qa_ref, ka_ref, va_ref, qb_ref, kb_ref, vb_ref,
               rg_ref, ga_ref, gb_ref):
    h = _rmsnorm(x_ref[...], g_ref[...]).astype(BF16)

    def mm(lo, width):
        return _dot(h, w_ref[:, lo:lo + width])

    qa_ref[...] = (mm(OFF_QA, Q_A) * (HEAD_DIM_A ** -0.5)).astype(BF16)
    ka_ref[...] = mm(OFF_KA, KV_A)
    va_ref[...] = mm(OFF_VA, KV_A)
    qb_ref[...] = mm(OFF_QB, Q_B).astype(BF16)
    kb_ref[...] = (mm(OFF_KB, Q_B) * (DK_B ** -0.5)).astype(BF16)
    vb_ref[...] = mm(OFF_VB, V_B).astype(BF16)
    rg_ref[...] = mm(OFF_RG, V_B)
    ga_ref[...] = mm(OFF_GA, D_MODEL) + bg_ref[:, :D_MODEL]
    gb_ref[...] = mm(OFF_GB, D_MODEL) + bg_ref[:, D_MODEL:]


def _project(x, g, w_bf16, b_gate, tm):
    n = x.shape[0]
    row = lambda width: pl.BlockSpec((tm, width), lambda i: (i, 0))
    full = lambda a: pl.BlockSpec(a.shape, lambda i: (0, 0))
    widths = (Q_A, KV_A, KV_A, Q_B, Q_B, V_B, V_B, D_MODEL, D_MODEL)
    dtypes = (BF16, F32, F32, BF16, BF16, BF16, F32, F32, F32)
    return pl.pallas_call(
        _proj_body,
        grid=(n // tm,),
        in_specs=[row(D_MODEL), full(g), full(w_bf16), full(b_gate)],
        out_specs=[row(w) for w in widths],
        out_shape=[jax.ShapeDtypeStruct((n, w), d) for w, d in zip(widths, dtypes)],
        compiler_params=_params(("parallel",)),
        name="projection",
    )(x, g, w_bf16, b_gate)


def _swa_body(sink_ref, q_ref, kp_ref, kc_ref, vp_ref, vc_ref, nd_ref, al_ref, o_ref):
    k = jnp.concatenate([kp_ref[...], kc_ref[...]], axis=0)
    v = jnp.concatenate([vp_ref[...], vc_ref[...]], axis=0)
    lane = lax.broadcasted_iota(I32, k.shape, 1)
    low = lane < HEAD_DIM_A
    k_sw = pltpu.roll(k, HEAD_DIM_A, axis=1)
    v_sw = pltpu.roll(v, HEAD_DIM_A, axis=1)
    negdist = nd_ref[...]
    allowed = al_ref[0] > 0.5
    for kv in range(N_KV_A):
        own = low if kv == 0 else jnp.logical_not(low)
        k_rep = jnp.where(own, k, k_sw)
        v_rep = jnp.where(own, v, v_sw)
        zero = jnp.zeros_like(k_rep)
        kk = jnp.concatenate([jnp.where(low, k_rep, zero), jnp.where(low, zero, k_rep)], axis=0).astype(BF16)
        vv = jnp.concatenate([jnp.where(low, v_rep, zero), jnp.where(low, zero, v_rep)], axis=0).astype(BF16)
        for a in range(GROUP_A // 2):
            col = kv * GROUP_A * HEAD_DIM_A + a * LANES
            s2 = _dot_nt(q_ref[:, col:col + LANES], kk)
            ps, inv = [], []
            for u in range(2):
                head = kv * GROUP_A + 2 * a + u
                slope = 2.0 ** (-8.0 * (head + 1) / N_HEADS_A)
                sink = sink_ref[head]
                s = s2[:, u * ATT_KEYS:(u + 1) * ATT_KEYS] + slope * negdist
                s = jnp.where(allowed, s, NEG_INF)
                m = jnp.maximum(jnp.max(s, axis=-1, keepdims=True), sink)
                p = jnp.exp(s - m)
                inv.append(1.0 / (jnp.sum(p, axis=-1, keepdims=True) + jnp.exp(sink - m)))
                ps.append(p.astype(BF16))
            o = _dot(jnp.concatenate(ps, axis=1), vv)
            o_lane = lax.broadcasted_iota(I32, o.shape, 1)
            o_ref[:, col:col + LANES] = (o * jnp.where(o_lane < HEAD_DIM_A, inv[0], inv[1])).astype(BF16)


def _swa(sinks, q, k, v, negdist, allow, groups, q_blocks, tq):
    n = q.shape[0]
    kblocks = k.shape[0] // LANES // groups
    shift = kblocks - q_blocks
    prev = lambda b, j: (b * kblocks + jnp.maximum(j + shift - 1, 0), 0)
    cur = lambda b, j: (b * kblocks + j + shift, 0)
    kv_spec = lambda f: pl.BlockSpec((LANES, KV_A), f)
    return pl.pallas_call(
        _swa_body,
        grid=(groups, q_blocks),
        in_specs=[pl.BlockSpec(memory_space=pltpu.SMEM),
                  pl.BlockSpec((tq, Q_A), lambda b, j: (b * q_blocks + j, 0)),
                  kv_spec(prev), kv_spec(cur), kv_spec(prev), kv_spec(cur),
                  pl.BlockSpec((tq, ATT_KEYS), lambda b, j: (0, 0)),
                  pl.BlockSpec((1, tq, ATT_KEYS), lambda b, j: (jnp.minimum(j + shift, 1), 0, 0))],
        out_specs=pl.BlockSpec((tq, Q_A), lambda b, j: (b * q_blocks + j, 0)),
        out_shape=jax.ShapeDtypeStruct((n, Q_A), BF16),
        compiler_params=_params(("parallel", "parallel")),
        name="attention",
    )(sinks, q, k, k, v, v, negdist, allow)


def _ret_body(q_ref, k_ref, v_ref, rg_ref, s0_ref, intra_ref, rdec_ref, kdec_ref, sdec_ref,
              o_ref, st_ref):
    c = pl.program_id(1)

    @pl.when(c == 0)
    def _():
        st_ref[...] = s0_ref[...]

    for h in range(N_HEADS_B):
        q = q_ref[:, h * DK_B:(h + 1) * DK_B]
        k = k_ref[:, h * DK_B:(h + 1) * DK_B]
        v = v_ref[:, h * DV_B:(h + 1) * DV_B]
        st = st_ref[0, h]
        scores = (_dot_nt(q, k) * intra_ref[h]).astype(BF16)
        o = _dot(scores, v) + _dot(q, st.astype(BF16)) * rdec_ref[h]
        k_dec = (k.astype(F32) * kdec_ref[h]).astype(BF16)
        st_ref[0, h] = sdec_ref[h] * st + _dot_tn(k_dec, v)
        mu = jnp.mean(o, axis=-1, keepdims=True)
        d = o - mu
        var = jnp.mean(d * d, axis=-1, keepdims=True)
        on = d * lax.rsqrt(var + EPS)
        r = rg_ref[:, h * DV_B:(h + 1) * DV_B]
        o_ref[:, h * DV_B:(h + 1) * DV_B] = (on * (r * jax.nn.sigmoid(r))).astype(BF16)


def _retention_consts(length):
    lg = jnp.log1p(-(2.0 ** (-5.0 - jnp.arange(N_HEADS_B, dtype=F32))))
    n = jnp.arange(length, dtype=F32)
    diff = n[:, None] - n[None, :]
    intra = jnp.where(diff >= 0, jnp.exp(jnp.maximum(diff, 0.0) * lg[:, None, None]), 0.0)
    rdec = jnp.exp((n[None, :] + 1.0) * lg[:, None])[..., None]
    kdec = jnp.exp((length - 1.0 - n)[None, :] * lg[:, None])[..., None]
    sdec = jnp.exp(length * lg)
    return intra, rdec, kdec, sdec


def _retention(q, k, v, rg, s0, streams, chunks, length):
    n = q.shape[0]
    intra, rdec, kdec, sdec = _retention_consts(length)
    row = lambda width: pl.BlockSpec((length, width), lambda b, c: (b * chunks + c, 0))
    const = lambda a: pl.BlockSpec(a.shape, lambda b, c: (0,) * a.ndim)
    st_spec = pl.BlockSpec((1, N_HEADS_B, DK_B, DV_B), lambda b, c: (b, 0, 0, 0))
    return pl.pallas_call(
        _ret_body,
        grid=(streams, chunks),
        in_specs=[row(Q_B), row(Q_B), row(V_B), row(V_B), st_spec,
                  const(intra), const(rdec), const(kdec), pl.BlockSpec(memory_space=pltpu.SMEM)],
        out_specs=[row(V_B), st_spec],
        out_shape=[jax.ShapeDtypeStruct((n, V_B), BF16),
                   jax.ShapeDtypeStruct((streams, N_HEADS_B, DK_B, DV_B), F32)],
        compiler_params=_params(("parallel", "arbitrary")),
        name="retention",
    )(q, k, v, rg, s0, intra, rdec, kdec, sdec)


def _merge_body(x_ref, oa_ref, ob_ref, ga_ref, gb_ref, wa_ref, wb_ref, wo_ref, o_ref):
    ya = _dot(oa_ref[...], wa_ref[...])
    yb = _dot(ob_ref[...], wb_ref[...])
    mix = jax.nn.sigmoid(ga_ref[...]) * ya + jax.nn.sigmoid(gb_ref[...]) * yb
    o_ref[...] = x_ref[...] + _dot(mix.astype(BF16), wo_ref[...])


def _merge(x, oa, ob, ga, gb, wa, wb, wo, tm):
    n = x.shape[0]
    row = lambda width: pl.BlockSpec((tm, width), lambda i: (i, 0))
    full = lambda a: pl.BlockSpec(a.shape, lambda i: (0, 0))
    return pl.pallas_call(
        _merge_body,
        grid=(n // tm,),
        in_specs=[row(D_MODEL), row(Q_A), row(V_B), row(D_MODEL), row(D_MODEL), full(wa), full(wb), full(wo)],
        out_specs=row(D_MODEL),
        out_shape=jax.ShapeDtypeStruct((n, D_MODEL), F32),
        compiler_params=_params(("parallel",)),
        name="merge",
    )(x, oa, ob, ga, gb, wa, wb, wo)


def _oddeven_merge(lo, hi, r):
    step = r * 2
    if step < hi - lo:
        yield from _oddeven_merge(lo, hi, step)
        yield from _oddeven_merge(lo + r, hi, step)
        yield from ((i, i + r) for i in range(lo + r, hi - r, step))
    else:
        yield (lo, lo + r)


def _oddeven_sort(lo, hi):
    if hi - lo >= 1:
        mid = lo + (hi - lo) // 2
        yield from _oddeven_sort(lo, mid)
        yield from _oddeven_sort(mid + 1, hi)
        yield from _oddeven_merge(lo, hi, 1)


SUBLANES = 8
_SORT_PAIRS = tuple(_oddeven_sort(0, N_KEYS // SUBLANES - 1))


def _top16_keys(s):
    t = s.shape[1]
    n_lists = N_KEYS // SUBLANES
    sub = lax.broadcasted_iota(I32, (SUBLANES, t), 0).astype(F32)
    vals = [s[SUBLANES * i:SUBLANES * (i + 1)] for i in range(n_lists)]
    ids = [sub + float(SUBLANES * i) for i in range(n_lists)]
    for i, j in _SORT_PAIRS:
        a, b, ia, ib = vals[i], vals[j], ids[i], ids[j]
        swap = (b > a) | ((b == a) & (ib < ia))
        vals[i], vals[j] = jnp.where(swap, b, a), jnp.where(swap, a, b)
        ids[i], ids[j] = jnp.where(swap, ib, ia), jnp.where(swap, ia, ib)
    rank = lax.broadcasted_iota(I32, (PEER_TOPK, t), 0)
    out_v = jnp.zeros((PEER_TOPK, t), F32)
    out_i = jnp.zeros((PEER_TOPK, t), F32)
    for r in range(PEER_TOPK):
        m = jnp.max(vals[0], axis=0, keepdims=True)
        pick = jnp.min(jnp.where(vals[0] == m, ids[0], 3e38), axis=0, keepdims=True)
        hit = ids[0] == pick
        out_v = jnp.where(rank == r, m, out_v)
        out_i = jnp.where(rank == r, pick, out_i)
        for d in range(PEER_TOPK - 1 - r):
            vals[d] = jnp.where(hit, vals[d + 1], vals[d])
            ids[d] = jnp.where(hit, ids[d + 1], ids[d])
    return out_v, out_i


def _top16_pairs(v1, i1, v2, i2):
    t = v1.shape[1]
    sub = lax.broadcasted_iota(I32, (SUBLANES, t), 0).astype(F32)
    flat_lo = sub * PEER_TOPK
    flat_hi = (sub + SUBLANES) * PEER_TOPK
    lo_v = [v1[0:SUBLANES] + v2[b:b + 1] for b in range(PEER_TOPK)]
    lo_e = [i1[0:SUBLANES] * N_KEYS + i2[b:b + 1] for b in range(PEER_TOPK)]
    hi_v = v1[SUBLANES:] + v2[0:1]
    hi_e = i1[SUBLANES:] * N_KEYS + i2[0:1]
    taken = jnp.zeros((SUBLANES, t), F32)
    rank = lax.broadcasted_iota(I32, (PEER_TOPK, t), 0)
    best = jnp.zeros((PEER_TOPK, t), F32)
    eidx = jnp.zeros((PEER_TOPK, t), F32)
    red = lambda f, x: f(x, axis=0, keepdims=True)
    for r in range(PEER_TOPK):
        m = jnp.maximum(red(jnp.max, lo_v[0]), red(jnp.max, hi_v))
        f_lo = flat_lo + taken
        pick = jnp.minimum(red(jnp.min, jnp.where(lo_v[0] == m, f_lo, 3e38)),
                           red(jnp.min, jnp.where(hi_v == m, flat_hi, 3e38)))
        hit_lo = f_lo == pick
        hit_hi = flat_hi == pick
        e = jnp.maximum(red(jnp.max, jnp.where(hit_lo, lo_e[0], -1.0)),
                        red(jnp.max, jnp.where(hit_hi, hi_e, -1.0)))
        best = jnp.where(rank == r, m, best)
        eidx = jnp.where(rank == r, e, eidx)
        taken = taken + jnp.where(hit_lo, 1.0, 0.0)
        hi_v = jnp.where(hit_hi, -jnp.inf, hi_v)
        for d in range(PEER_TOPK - 1 - r):
            lo_v[d] = jnp.where(hit_lo, lo_v[d + 1], lo_v[d])
            lo_e[d] = jnp.where(hit_lo, lo_e[d + 1], lo_e[d])
    return best, eidx


def _route_body(x_ref, g_ref, wq_ref, k1_ref, k2_ref, idx_ref, gate_ref, h_ref):
    h_ref[...] = _rmsnorm(x_ref[...], g_ref[...])
    h = h_ref[...].astype(BF16)
    q_t = _dot_nt(wq_ref[...], h).astype(BF16)
    idx_rows, gate_rows = [], []
    for hd in range(PEER_HEADS):
        base = hd * PEER_QDIM
        s1 = _dot(k1_ref[...], q_t[base:base + PEER_HALF])
        s2 = _dot(k2_ref[...], q_t[base + PEER_HALF:base + PEER_QDIM])
        v1, i1 = _top16_keys(s1)
        v2, i2 = _top16_keys(s2)
        best, eidx = _top16_pairs(v1, i1, v2, i2)
        e = jnp.exp(best - best[0:1])
        gate_rows.append(e / jnp.sum(e, axis=0, keepdims=True))
        idx_rows.append(eidx.astype(I32))
    idx_ref[...] = jnp.concatenate(idx_rows, axis=0)
    gate_ref[...] = jnp.concatenate(gate_rows, axis=0)


def _route(x, g, wq_t, k1, k2, tm, first_row, n):
    first_block = first_row // tm
    full = lambda a: pl.BlockSpec(a.shape, lambda i: (0, 0))
    col = pl.BlockSpec((PICKS, tm), lambda i: (0, i))
    return pl.pallas_call(
        _route_body,
        grid=(n // tm,),
        in_specs=[pl.BlockSpec((tm, D_MODEL), lambda i: (i + first_block, 0)), full(g), full(wq_t), full(k1),
                  full(k2)],
        out_specs=[col, col, pl.BlockSpec((tm, D_MODEL), lambda i: (i, 0))],
        out_shape=[jax.ShapeDtypeStruct((PICKS, n), I32), jax.ShapeDtypeStruct((PICKS, n), F32),
                   jax.ShapeDtypeStruct((n, D_MODEL), F32)],
        compiler_params=_params(("parallel",)),
        name="routing",
    )(x, g, wq_t, k1, k2)


SC_CORES = 2
SC_SUBCORES = 16
SC_LANES = 16
SC_CHUNK = 16
SC_PROMPT_ROWS = 18688


def _gelu_via_exp(x):
    z = 0.7978845608028654 * (x + 0.044715 * x * x * x)
    return 0.5 * x * (2.0 - 2.0 / (1.0 + jnp.exp(2.0 * z)))


def _sc_experts(table, idx, gate, h, first):
    n = idx.shape[0]
    workers = SC_CORES * SC_SUBCORES
    per_worker = n // workers
    assert n % (2 * workers) == 0 and SC_CHUNK == SC_LANES
    n_chunks = PICKS // SC_CHUNK
    steps = D_MODEL // SC_LANES
    mesh = plsc.VectorSubcoreMesh(core_axis_name="c", subcore_axis_name="s",
                                  num_cores=SC_CORES, num_subcores=SC_SUBCORES)

    def body(tab_hbm, idx_hbm, gate_hbm, h_hbm, y_hbm,
             idx_0, idx_1, gate_0, gate_1, h_0, h_1, y_0, y_1, rows_a, rows_b,
             in_sem_0, in_sem_1, y_sem_0, y_sem_1, sem_a, sem_b):
        base = (lax.axis_index("s") * SC_CORES + lax.axis_index("c")) * per_worker
        last = base + per_worker - 1
        sets = ((idx_0, gate_0, h_0, y_0, in_sem_0, y_sem_0), (idx_1, gate_1, h_1, y_1, in_sem_1, y_sem_1))
        bufs = ((rows_a, sem_a), (rows_b, sem_b))

        def in_copies(t, q):
            idx_v, gate_v, h_v, _, sem, _ = sets[q]
            return (pltpu.make_async_copy(idx_hbm.at[t], idx_v, sem),
                    pltpu.make_async_copy(gate_hbm.at[t], gate_v, sem),
                    pltpu.make_async_copy(h_hbm.at[t + first], h_v, sem))

        def y_copy(t, q):
            return pltpu.make_async_copy(sets[q][3], y_hbm.at[t], sets[q][5])

        def gather(c, b, q):
            rows, sem = bufs[b]
            picks = sets[q][0].at[pl.ds(c * SC_CHUNK, SC_CHUNK)]
            return pltpu.make_async_copy(tab_hbm.at[picks], rows, sem)

        def compute(c, b, q):
            rows, _ = bufs[b]
            _, gate_v, h_v, y_v, _, _ = sets[q]
            u_of = lambda word: lax.bitcast_convert_type(word & jnp.int32(-65536), F32)
            v_of = lambda word: lax.bitcast_convert_type(word << 16, F32)

            def dot_step(j, accs):
                cols = pl.ds(j * SC_LANES, SC_LANES)
                hj = h_v[cols]
                return tuple(accs[k] + u_of(rows[k, 0, cols]) * hj for k in range(SC_CHUNK))

            accs = plsc.parallel_loop(0, steps, carry=tuple(jnp.zeros((SC_LANES,), F32) for _ in range(SC_CHUNK)))(
                dot_step)
            lane = lax.iota(I32, SC_LANES)
            s = jnp.zeros((SC_LANES,), F32)
            for k in range(SC_CHUNK):
                s = jnp.where(lane == k, jnp.sum(accs[k]), s)
            w = _gelu_via_exp(s) * gate_v[pl.ds(c * SC_CHUNK, SC_CHUNK)]
            ws = [jnp.full((SC_LANES,), jnp.sum(jnp.where(lane == k, w, 0.0)), F32) for k in range(SC_CHUNK)]

            @plsc.parallel_loop(0, steps)
            def _(j):
                cols = pl.ds(j * SC_LANES, SC_LANES)
                terms = [ws[k] * v_of(rows[k, 0, cols]) for k in range(SC_CHUNK)]
                while len(terms) > 1:
                    terms = [terms[i] + terms[i + 1] for i in range(0, len(terms), 2)]
                y_v[cols] = y_v[cols] + terms[0]

        def token(t, q, first):
            nxt = jnp.minimum(t + 1, last)
            for cp in in_copies(nxt, 1 - q):
                cp.start()
            y_v = sets[q][3]

            @pl.when(jnp.logical_not(first))
            def _():
                y_copy(t, q).wait()

            @pl.loop(0, steps)
            def _(j):
                y_v[pl.ds(j * SC_LANES, SC_LANES)] = jnp.zeros((SC_LANES,), F32)

            @pl.loop(0, n_chunks // 2)
            def _(p):
                c0 = p * 2
                gather(c0 + 1, 1, q).start()
                gather(c0, 0, q).wait()
                compute(c0, 0, q)

                @pl.when(p + 1 < n_chunks // 2)
                def _():
                    gather(c0 + 2, 0, q).start()

                gather(c0 + 1, 1, q).wait()
                compute(c0 + 1, 1, q)

            for cp in in_copies(nxt, 1 - q):
                cp.wait()
            gather(0, 0, 1 - q).start()
            y_copy(t, q).start()

        for cp in in_copies(base, 0):
            cp.start()
        for cp in in_copies(base, 0):
            cp.wait()
        gather(0, 0, 0).start()

        @pl.loop(0, per_worker // 2)
        def _(i):
            token(base + 2 * i, 0, i == 0)
            token(base + 2 * i + 1, 1, i == 0)

        gather(0, 0, 0).wait()
        y_copy(last - 1, 0).wait()
        y_copy(last, 1).wait()

    vec = lambda n_el, dt: [pltpu.VMEM((n_el,), dt), pltpu.VMEM((n_el,), dt)]
    return pl.kernel(
        body, out_type=jax.ShapeDtypeStruct((n, D_MODEL), F32), mesh=mesh,
        scratch_types=vec(PICKS, I32) + vec(PICKS, F32) + vec(D_MODEL, F32) + vec(D_MODEL, F32)
        + [pltpu.VMEM((SC_CHUNK, 1, D_MODEL), I32) for _ in range(2)]
        + [pltpu.SemaphoreType.DMA] * 6,
        compiler_params=pltpu.CompilerParams(needs_layout_passes=False),
        name="experts_sc",
    )(table, idx, gate, h)


def _finish_body(x_ref, y_ref, gf_ref, *rest_and_out):
    rest_and_out[-1][...] = _rmsnorm(x_ref[...] + y_ref[...], gf_ref[...])


def _finish(x, y, g_final, rest, tm, first_row=0):
    first_block = first_row // tm
    row = pl.BlockSpec((tm, D_MODEL), lambda i: (i + first_block, 0))
    extra = () if rest is None else (rest,)
    return pl.pallas_call(
        _finish_body,
        grid=(y.shape[0] // tm,),
        in_specs=[row, pl.BlockSpec((tm, D_MODEL), lambda i: (i, 0)), pl.BlockSpec(g_final.shape, lambda i: (0, 0))]
        + [pl.BlockSpec(memory_space=pl.ANY) for _ in extra],
        out_specs=row,
        out_shape=jax.ShapeDtypeStruct(x.shape if rest is not None else y.shape, F32),
        input_output_aliases={3: 0} if rest is not None else {},
        compiler_params=_params(("parallel",)),
        name="finish",
    )(x, y, g_final, *extra)


EXP_TOKENS = 256
EXP_SLOTS = 4


def _expert_body(x_ref, g_ref, gf_ref, idx_hbm, gate_ref, tab_hbm, o_ref,
                 idx_smem, buf0, buf1, buf2, buf3, h_buf, y_buf, sems, idx_sem):
    bufs = (buf0, buf1, buf2, buf3)
    ahead = EXP_SLOTS - 1
    i = pl.program_id(0)
    idx_copy = pltpu.make_async_copy(idx_hbm.at[pl.ds(i * (EXP_TOKENS * PICKS), EXP_TOKENS * PICKS)],
                                     idx_smem, idx_sem)
    idx_copy.start()
    h_buf[...] = _rmsnorm(x_ref[...], g_ref[...])
    idx_copy.wait()

    def issue(t, slot):
        for k in range(PICKS):
            e = idx_smem[t * PICKS + k]
            pltpu.async_copy(tab_hbm.at[e], bufs[slot].at[pl.ds(k, 1), :], sems.at[slot], priority=k % 2)

    def wait_slot(slot):
        pltpu.make_async_copy(bufs[(slot + 1) % EXP_SLOTS], bufs[slot], sems.at[slot]).wait()

    lane = lax.broadcasted_iota(I32, (PICKS, EXP_TOKENS), 1)

    def compute(t, slot):
        h = h_buf[pl.ds(t, 1), :]
        u = lax.bitcast_convert_type(bufs[slot][...] & jnp.int32(-65536), F32)
        s = jnp.sum(u * h, axis=-1, keepdims=True)
        gate = jnp.sum(jnp.where(lane == t, gate_ref[...], 0.0), axis=-1, keepdims=True)
        w = gate * jax.nn.gelu(s)
        v = lax.bitcast_convert_type(bufs[slot][...] << 16, F32)
        y_buf[pl.ds(t, 1), :] = jnp.sum(w * v, axis=0, keepdims=True)

    def group(t0, issue_upto):
        for s in range(EXP_SLOTS):
            wait_slot(s)
            if s < issue_upto:
                issue(t0 + s + ahead, (s + ahead) % EXP_SLOTS)
            compute(t0 + s, s)

    for s in range(ahead):
        issue(s, s)

    def steady(q, carry):
        group(q * EXP_SLOTS, EXP_SLOTS)
        return carry

    n_groups = EXP_TOKENS // EXP_SLOTS
    lax.fori_loop(0, n_groups - 1, steady, 0)
    group((n_groups - 1) * EXP_SLOTS, EXP_SLOTS - ahead)
    o_ref[...] = _rmsnorm(x_ref[...] + y_buf[...], gf_ref[...])


def _experts(x, g, g_final, idx_flat, gate_t, table, first_token):
    n = x.shape[0]
    first_block = first_token // EXP_TOKENS
    full = lambda a: pl.BlockSpec(a.shape, lambda i: (0, 0))
    row = pl.BlockSpec((EXP_TOKENS, D_MODEL), lambda i: (i + first_block, 0))
    return pl.pallas_call(
        _expert_body,
        grid=(n // EXP_TOKENS - first_block,),
        in_specs=[row, full(g), full(g_final),
                  pl.BlockSpec(memory_space=pl.ANY),
                  pl.BlockSpec((PICKS, EXP_TOKENS), lambda i: (0, i)),
                  pl.BlockSpec(memory_space=pl.ANY)],
        out_specs=row,
        out_shape=jax.ShapeDtypeStruct((n, D_MODEL), F32),
        scratch_shapes=[pltpu.SMEM((EXP_TOKENS * PICKS,), I32)]
        + [pltpu.VMEM((PICKS, D_MODEL), I32) for _ in range(EXP_SLOTS)]
        + [pltpu.VMEM((EXP_TOKENS, D_MODEL), F32),
           pltpu.VMEM((EXP_TOKENS, D_MODEL), F32),
           pltpu.SemaphoreType.DMA((EXP_SLOTS,)),
           pltpu.SemaphoreType.DMA(())],
        compiler_params=pltpu.CompilerParams(dimension_semantics=("arbitrary",), vmem_limit_bytes=VMEM_LIMIT,
                                             disable_bounds_checks=True),
        name="experts",
    )(x, g, g_final, idx_flat, gate_t, table)


def _attention_masks(tq):
    i = jnp.arange(tq, dtype=I32)[:, None]
    r = jnp.arange(ATT_KEYS, dtype=I32)[None, :]
    negdist = -jnp.abs(WINDOW + i - r).astype(F32)
    return negdist, i, r


def _layer(x, n_groups, rows, swa_inputs, ret_state, ret_len, w, tm, n_sc):
    (norm_mix, w_in, b_gate, sinks, wa, wb, wo, norm_ffn, wq_t, k1, k2, packed, norm_final) = w
    assert n_sc < x.shape[0]
    qa, ka, va, qb, kb, vb, rg, ga, gb = _project(x, norm_mix, w_in, b_gate, tm)
    oa, k_rows, v_rows = swa_inputs(qa, ka, va, sinks)
    ob, s_fin = _retention(qb, kb, vb, rg, ret_state, n_groups, rows // ret_len, ret_len)
    x1 = _merge(x, oa, ob, ga, gb, wa, wb, wo, tm)
    if n_sc:
        idx_t, gate_t, h2 = _route(x1, norm_ffn, wq_t, k1, k2, tm, 0, n_sc)
        y_sc = _sc_experts(packed, idx_t.T, gate_t.T, h2, 0)
    idx_t, gate_t, _ = _route(x1, norm_ffn, wq_t, k1, k2, tm, n_sc, x.shape[0] - n_sc)
    y = _experts(x1, norm_ffn, norm_final, idx_t.T.reshape(-1), gate_t, packed, n_sc)
    if n_sc:
        y = _finish(x1, y_sc, norm_final, y, tm)
    return y, k_rows, v_rows, s_fin


def kernel(x_prompt, x_sample, cache_swa_k, cache_swa_v, state_ret, norm_mix, w_in, b_gate, attn_sinks,
           w_branch_a, w_branch_b, w_out, norm_ffn, peer_w_q, peer_sub_k1, peer_sub_k2, peer_u, peer_v,
           norm_final):
    batch, seq, _ = x_prompt.shape
    dec_batch, dec_seq, _ = x_sample.shape
    assert norm_mix.shape[0] == 1, "single-layer trunk"
    half = lambda a: lax.bitcast_convert_type(a.astype(BF16), jnp.uint16).astype(jnp.uint32)
    packed = lax.bitcast_convert_type((half(peer_u[0]) << 16) | half(peer_v[0]), I32)[:, None, :]
    table, x_prompt, x_sample = lax.optimization_barrier((packed, x_prompt, x_sample))
    w = (norm_mix[0][None], w_in[0].astype(BF16), b_gate[0][None], attn_sinks[0],
         w_branch_a[0].astype(BF16), w_branch_b[0].astype(BF16), w_out[0].astype(BF16),
         norm_ffn[0][None], peer_w_q[0].T.astype(BF16), peer_sub_k1[0].astype(BF16),
         peer_sub_k2[0].astype(BF16), table, norm_final[None])

    def swa_prompt(qa, ka, va, sinks):
        streams = qa.shape[0] // seq
        tq = 2 * CHUNK
        negdist, i, r = _attention_masks(tq)
        lq, lk = i // CHUNK, r // CHUNK
        band = (lk >= lq) & (lk <= lq + 2)
        allow = jnp.stack([band & (r >= WINDOW), band]).astype(F32)
        oa = _swa(sinks, qa, ka, va, negdist, allow, streams, seq // tq, tq)
        tail = lambda a: a.reshape(streams, seq, N_KV_A, HEAD_DIM_A)[:, seq - WINDOW:]
        return oa, tail(ka), tail(va)

    def swa_sample(qa, ka, va, sinks):
        negdist, i, r = _attention_masks(dec_seq)
        visible = jnp.broadcast_to(r < WINDOW + dec_seq, (dec_seq, ATT_KEYS))
        allow = jnp.stack([visible, visible]).astype(F32)
        pad = jnp.zeros((dec_batch, ATT_KEYS - WINDOW - dec_seq, KV_A), F32)
        k_all = jnp.concatenate([cache_swa_k[0].reshape(dec_batch, WINDOW, KV_A),
                                 ka.reshape(dec_batch, dec_seq, KV_A), pad], axis=1)
        v_all = jnp.concatenate([cache_swa_v[0].reshape(dec_batch, WINDOW, KV_A),
                                 va.reshape(dec_batch, dec_seq, KV_A), pad], axis=1)
        oa = _swa(sinks, qa, k_all.reshape(-1, KV_A), v_all.reshape(-1, KV_A), negdist, allow,
                  dec_batch, 1, dec_seq)
        tail = lambda a: a[:, dec_seq:WINDOW + dec_seq].reshape(dec_batch, WINDOW, N_KV_A, HEAD_DIM_A)
        return oa, tail(k_all), tail(v_all)

    ys, ks, vs, ss = _layer(x_sample.reshape(dec_batch * dec_seq, D_MODEL), dec_batch, dec_seq, swa_sample,
                            state_ret[0], dec_seq, w, 128, 0)
    s0 = jnp.zeros((batch, N_HEADS_B, DK_B, DV_B), F32)
    yp, kp, vp, sp = _layer(x_prompt.reshape(batch * seq, D_MODEL), batch, seq, swa_prompt, s0, 256, w, 256,
                            SC_PROMPT_ROWS)

    return (yp.reshape(batch, seq, D_MODEL), ys.reshape(dec_batch, dec_seq, D_MODEL),
            kp[None], vp[None], sp[None], ks[None], vs[None], ss[None])
```
